```python
import jax, jax.numpy as jnp
from jax import lax
import numpy as np

D_MODEL = 1024
BATCH = 8
SEQ = 2048
DEPTH = 4

GRID_W = 64
CTX_LEN = 256
D_MIX = 1024
FOUR_GROUPS = 4
FOUR_DIM = 64
D_FOUR = FOUR_GROUPS * FOUR_DIM
NA_HEADS = 6
NA_HEAD_DIM = 64
D_NA = NA_HEADS * NA_HEAD_DIM
NA_KH = 8
NA_KW = 16
GLA_HEADS = 4
GLA_DK = 48
GLA_DV = 96
D_GLA_K = GLA_HEADS * GLA_DK
D_GLA_V = GLA_HEADS * GLA_DV
GLA_RANK = 16
GLA_TAU = 16.0
GLA_CHUNK = 64
ROPE_BASE = 10000.0
EPS = 1e-6

IN_SIZES = (D_FOUR, D_FOUR, D_NA, D_NA, D_NA, D_NA, D_GLA_K, D_GLA_K, D_GLA_V, D_GLA_V, GLA_RANK, GLA_RANK)
D_IN = sum(IN_SIZES)
IN_SPLITS = tuple(int(s) for s in np.cumsum(IN_SIZES)[:-1])

kernel_name = "hybrid_fourier_natten_gla_dit"


def rmsnorm(x, w):
    xf = x.astype(jnp.float32)
    y = xf * lax.rsqrt(jnp.mean(xf * xf, axis=-1, keepdims=True) + EPS)
    return (y * w.astype(jnp.float32)).astype(x.dtype)


def split_heads(t, h):
    return t.reshape(t.shape[:-1] + (h, t.shape[-1] // h))


def rope_2d(t):
    L = t.shape[1]
    pos = jnp.arange(L)
    row = (pos // GRID_W).astype(jnp.float32)
    col = (pos % GRID_W).astype(jnp.float32)
    half = t.shape[-1] // 2
    n_freq = half // 2
    inv = ROPE_BASE ** (-jnp.arange(n_freq, dtype=jnp.float32) / n_freq)

    def rot(u, p):
        ang = p[:, None] * inv[None, :]
        cos = jnp.cos(ang)[None, :, None, :]
        sin = jnp.sin(ang)[None, :, None, :]
        u1, u2 = u[..., :n_freq], u[..., n_freq:]
        return jnp.concatenate([u1 * cos - u2 * sin, u1 * sin + u2 * cos], axis=-1)

    tf = t.astype(jnp.float32)
    out = jnp.concatenate([rot(tf[..., :half], row), rot(tf[..., half:], col)], axis=-1)
    return out.astype(t.dtype)


def fourier_mix(u, w_four):
    B, L, _ = u.shape
    ug = u.astype(jnp.float32).reshape(B, L, FOUR_GROUPS, FOUR_DIM)
    f = jnp.fft.fft2(ug, axes=(1, 3), norm="ortho").real
    return f.reshape(B, L, D_FOUR).astype(u.dtype) @ w_four


def neighbourhood_attention(q, k, v, kc, vc, rpb):
    B, L, H, d = q.shape
    rows = L // GRID_W
    kh = min(NA_KH, rows)
    r = np.arange(rows)
    row_start = np.clip(r - kh // 2, 0, rows - kh)
    row_idx = row_start[:, None] + np.arange(kh)[None, :]
    cq = np.arange(GRID_W)
    col_start = np.clip(cq - NA_KW // 2, 0, GRID_W - NA_KW)
    col_mask = (cq[None, :] >= col_start[:, None]) & (cq[None, :] < col_start[:, None] + NA_KW)
    row_off = row_idx - r[:, None] + NA_KH - 1
    col_off = np.clip(cq[None, :] - cq[:, None] + NA_KW - 1, 0, 2 * NA_KW - 2)
    bias = rpb[:, row_off[:, None, :, None], col_off[None, :, None, :]].astype(jnp.float32)
    bias = jnp.where(col_mask[None, None, :, None, :], bias, -jnp.inf)

    scale = d ** -0.5
    qg = q.reshape(B, rows, GRID_W, H, d) * scale
    kg = k.reshape(B, rows, GRID_W, H, d)[:, row_idx]
    vg = v.reshape(B, rows, GRID_W, H, d)[:, row_idx]
    s_loc = jnp.einsum('brqhd,brikhd->bhrqik', qg, kg).astype(jnp.float32) + bias[None]
    s_loc = s_loc.reshape(B, H, rows, GRID_W, kh * GRID_W)
    s_ctx = jnp.einsum('brqhd,bchd->bhrqc', qg, kc).astype(jnp.float32)
    p = jax.nn.softmax(jnp.concatenate([s_loc, s_ctx], axis=-1), axis=-1)
    p_loc = p[..., :kh * GRID_W].reshape(B, H, rows, GRID_W, kh, GRID_W).astype(v.dtype)
    p_ctx = p[..., kh * GRID_W:].astype(v.dtype)
    o = jnp.einsum('bhrqik,brikhd->brqhd', p_loc, vg) + jnp.einsum('bhrqc,bchd->brqhd', p_ctx, vc)
    return o.reshape(B, L, H * d)


def context_attention(qc, kc, vc):
    B, Lc, H, d = qc.shape
    s = jnp.einsum('bqhd,bkhd->bhqk', qc, kc).astype(jnp.float32) * (d ** -0.5)
    p = jax.nn.softmax(s, axis=-1).astype(vc.dtype)
    return jnp.einsum('bhqk,bkhd->bqhd', p, vc).reshape(B, Lc, H * d)


def gla_scan(q, k, v, log_a, s0):
    B, L, H, dk = q.shape
    dv = v.shape[-1]
    n = L // GLA_CHUNK

    def chunks(t):
        return t.astype(jnp.float32).reshape(B, n, GLA_CHUNK, H, t.shape[-1]).transpose(1, 0, 3, 2, 4)

    tri = jnp.tril(jnp.ones((GLA_CHUNK, GLA_CHUNK), dtype=bool))

    def step(state, inp):
        qi, ki, vi, gi = inp
        b = jnp.cumsum(gi, axis=2)
        b_end = b[:, :, -1:, :]
        q_dec = qi * jnp.exp(b)
        k_dec = ki * jnp.exp(-b)
        a = jnp.where(tri, jnp.einsum('bhtk,bhsk->bhts', q_dec, k_dec), 0.0)
        o = jnp.einsum('bhts,bhsv->bhtv', a, vi) + jnp.einsum('bhtk,bhkv->bhtv', q_dec, state)
        k_to_end = ki * jnp.exp(b_end - b)
        state = jnp.exp(b_end[:, :, 0, :])[..., None] * state + jnp.einsum('bhsk,bhsv->bhkv', k_to_end, vi)
        return state, o

    state, o = lax.scan(step, s0.astype(jnp.float32), (chunks(q), chunks(k), chunks(v), chunks(log_a)))
    o = o.transpose(1, 0, 3, 2, 4).reshape(B, L, H, dv)
    return o, state


def gla_log_decay(z, w_a, b_a):
    g = (z @ w_a + b_a).astype(jnp.float32)
    return split_heads(jax.nn.log_sigmoid(g) / GLA_TAU, GLA_HEADS)


def gla_out_norm(o, w, dtype):
    o = o * lax.rsqrt(jnp.mean(o * o, axis=-1, keepdims=True) + EPS) * w.astype(jnp.float32)
    return o.reshape(o.shape[0], o.shape[1], D_GLA_V).astype(dtype)


def setup_inputs(seed: int = 0) -> dict:
    key = jax.random.key(seed)
    ks = jax.random.split(key, 20)
    f32 = jnp.float32
    nrm = lambda k, shape, s: jax.random.normal(k, shape, f32) * s
    return {
        "x": nrm(ks[0], (BATCH, SEQ, D_MODEL), 1.0),
        "c": nrm(ks[1], (BATCH, D_MODEL), 1.0),
        "ctx": nrm(ks[2], (BATCH, CTX_LEN, D_MODEL), 1.0),
        "c_ctx": nrm(ks[3], (D_MODEL,), 1.0),
        "w_ada": nrm(ks[4], (DEPTH, D_MODEL, 3 * D_MODEL), 0.5 * D_MODEL ** -0.5),
        "b_ada": nrm(ks[5], (DEPTH, 3 * D_MODEL), 0.02),
        "norm_w": 1.0 + nrm(ks[6], (DEPTH, D_MODEL), 0.02),
        "w_in": nrm(ks[7], (DEPTH, D_MODEL, D_IN), D_MODEL ** -0.5),
        "w_four": nrm(ks[8], (DEPTH, D_FOUR, D_FOUR), D_FOUR ** -0.5),
        "rpb": nrm(ks[9], (DEPTH, NA_HEADS, 2 * NA_KH - 1, 2 * NA_KW - 1), 0.1),
        "w_alpha_fwd": nrm(ks[10], (DEPTH, GLA_RANK, D_GLA_K), GLA_RANK ** -0.5),
        "b_alpha_fwd": nrm(ks[11], (DEPTH, D_GLA_K), 0.1),
        "w_alpha_bwd": nrm(ks[12], (DEPTH, GLA_RANK, D_GLA_K), GLA_RANK ** -0.5),
        "b_alpha_bwd": nrm(ks[13], (DEPTH, D_GLA_K), 0.1),
        "gla_norm_w": 1.0 + nrm(ks[14], (DEPTH, GLA_DV), 0.02),
        "w_out": nrm(ks[15], (DEPTH, D_MIX, D_MODEL), D_MIX ** -0.5),
        "norm_f": 1.0 + nrm(ks[16], (D_MODEL,), 0.02),
    }


def reference(x, c, ctx, c_ctx, w_ada, b_ada, norm_w, w_in, w_four, rpb,
              w_alpha_fwd, b_alpha_fwd, w_alpha_bwd, b_alpha_bwd, gla_norm_w, w_out, norm_f):
    silu_c = jax.nn.silu(c)
    silu_cc = jax.nn.silu(c_ctx)
    for l in range(DEPTH):
        last = l == DEPTH - 1
        shift_x, scale_x, gate_x = jnp.split(silu_c @ w_ada[l] + b_ada[l], 3, axis=-1)
        shift_c, scale_c, gate_c = jnp.split(silu_cc @ w_ada[l] + b_ada[l], 3, axis=-1)
        hx = rmsnorm(x, norm_w[l]) * (1.0 + scale_x[:, None, :]) + shift_x[:, None, :]
        hc = rmsnorm(ctx, norm_w[l]) * (1.0 + scale_c) + shift_c

        (fx, fgx, nqx, nkx, nvx, ngx, gqx, gkx, gvx, ggx, zfx, zbx) = jnp.split(hx @ w_in[l], IN_SPLITS, axis=-1)
        (fc, fgc, nqc, nkc, nvc, ngc, gqc, gkc, gvc, ggc, zfc, zbc) = jnp.split(hc @ w_in[l], IN_SPLITS, axis=-1)

        nkc_h, nvc_h = split_heads(nkc, NA_HEADS), split_heads(nvc, NA_HEADS)
        na_x = neighbourhood_attention(split_heads(nqx, NA_HEADS), split_heads(nkx, NA_HEADS),
                                       split_heads(nvx, NA_HEADS), nkc_h, nvc_h, rpb[l])

        qx_g = rope_2d(split_heads(gqx, GLA_HEADS)) * (GLA_DK ** -0.5)
        kx_g = rope_2d(split_heads(gkx, GLA_HEADS))
        vx_g = split_heads(gvx, GLA_HEADS)
        qc_g = split_heads(gqc, GLA_HEADS) * (GLA_DK ** -0.5)
        kc_g = split_heads(gkc, GLA_HEADS)
        vc_g = split_heads(gvc, GLA_HEADS)
        la_xf = gla_log_decay(zfx, w_alpha_fwd[l], b_alpha_fwd[l])
        la_cf = gla_log_decay(zfc, w_alpha_fwd[l], b_alpha_fwd[l])
        la_xb = gla_log_decay(zbx, w_alpha_bwd[l], b_alpha_bwd[l])
        la_cb = gla_log_decay(zbc, w_alpha_bwd[l], b_alpha_bwd[l])
        s0 = jnp.zeros((x.shape[0], GLA_HEADS, GLA_DK, GLA_DV), jnp.float32)
        flip = lambda t: jnp.flip(t, axis=1)
        oc_f, sc_f = gla_scan(qc_g, kc_g, vc_g, la_cf, s0)
        ox_f, _ = gla_scan(qx_g, kx_g, vx_g, la_xf, sc_f)
        oc_b, sc_b = gla_scan(flip(qc_g), flip(kc_g), flip(vc_g), flip(la_cb), s0)
        ox_b, _ = gla_scan(flip(qx_g), flip(kx_g), flip(vx_g), flip(la_xb), sc_b)
        gla_x = gla_out_norm(ox_f + flip(ox_b), gla_norm_w[l], x.dtype)

        four_x = fourier_mix(fx, w_four[l])

        mix_x = jnp.concatenate([four_x * jax.nn.silu(fgx), na_x * jax.nn.silu(ngx),
                                 gla_x * jax.nn.silu(ggx)], axis=-1)
        x_new = x + gate_x[:, None, :] * (mix_x @ w_out[l])

        if not last:
            na_c = context_attention(split_heads(nqc, NA_HEADS), nkc_h, nvc_h)
            gla_c = gla_out_norm(oc_f + flip(oc_b), gla_norm_w[l], ctx.dtype)
            four_c = fourier_mix(fc, w_four[l])
            mix_c = jnp.concatenate([four_c * jax.nn.silu(fgc), na_c * jax.nn.silu(ngc),
                                     gla_c * jax.nn.silu(ggc)], axis=-1)
            ctx = ctx + gate_c * (mix_c @ w_out[l])
        x = x_new
    return rmsnorm(x, norm_f)
```

```python
import functools

import numpy as np
import jax
import jax.numpy as jnp
from jax import lax
from jax.experimental import pallas as pl
from jax.experimental.pallas import tpu as pltpu

F32 = jnp.float32
BF16 = jnp.bfloat16

D_MODEL = 1024
SEQ = 2048
CTX_LEN = 256
T_ALL = SEQ + CTX_LEN
GRID_W = 64
GRID_H = SEQ // GRID_W
EPS = 1e-6

FOUR_GROUPS, FOUR_DIM = 4, 64
D_FOUR = FOUR_GROUPS * FOUR_DIM
NA_HEADS, NA_HEAD_DIM = 6, 64
D_NA = NA_HEADS * NA_HEAD_DIM
NA_KH, NA_KW = 8, 16
GLA_HEADS, GLA_DK, GLA_DV = 4, 48, 96
D_GLA_K = GLA_HEADS * GLA_DK
D_GLA_V = GLA_HEADS * GLA_DV
GLA_RANK = 16
GLA_TAU = 16.0
GLA_CHUNK = 64
ROPE_BASE = 10000.0

LANE = 128
GLA_KP = 2 * LANE
Z_PAD = LANE
N_CHUNKS_X = SEQ // GLA_CHUNK
N_CHUNKS_C = CTX_LEN // GLA_CHUNK

NA_QROWS = 4
NA_KROWS = NA_QROWS + NA_KH - 1
NA_QB = NA_QROWS * GRID_W
NA_KB = NA_KROWS * GRID_W
NA_NBLK = GRID_H // NA_QROWS

TOK_TILE = 768
VMEM_LIMIT = 56 * 1024 * 1024

COL_FOUR = 0
COL_NA = COL_FOUR + 2 * D_FOUR
COL_GQ = COL_NA + 4 * D_NA
COL_GK = COL_GQ + GLA_KP
COL_GV = COL_GK + GLA_KP
COL_GG = COL_GV + D_GLA_V
COL_Z = COL_GG + D_GLA_V
N_COLS = COL_Z + Z_PAD


def _dot(a, b):
    return jnp.dot(a, b, preferred_element_type=F32)


def _dot_nt(a, b):
    return lax.dot_general(a, b, (((1,), (1,)), ((), ())), preferred_element_type=F32)


def _silu(x):
    return x / (1.0 + jnp.exp(-x))


def _split_bf16(x):
    hi = x.astype(BF16)
    lo = (x - hi.astype(F32)).astype(BF16)
    return hi, lo


def _gla_qk_perm():
    n_freq = GLA_DK // 4
    dst = np.zeros((D_GLA_K,), np.int64)
    for h in range(GLA_HEADS):
        for half in range(2):
            for part in range(2):
                for f in range(n_freq):
                    src = h * GLA_DK + half * 2 * n_freq + part * n_freq + f
                    dst[src] = part * LANE + h * 2 * n_freq + half * n_freq + f
    return dst


def _rope_tables():
    n_freq = GLA_DK // 4
    inv = ROPE_BASE ** (-np.arange(n_freq, dtype=np.float64) / n_freq)
    pos = np.arange(SEQ)
    prow, pcol = pos // GRID_W, pos % GRID_W
    cos = np.ones((T_ALL, LANE), np.float64)
    sin = np.zeros((T_ALL, LANE), np.float64)
    for h in range(GLA_HEADS):
        for half, p in enumerate((prow, pcol)):
            ang = p[:, None] * inv[None, :]
            lo = h * 2 * n_freq + half * n_freq
            cos[:SEQ, lo:lo + n_freq] = np.cos(ang)
            sin[:SEQ, lo:lo + n_freq] = np.sin(ang)
    return cos.astype(np.float32), sin.astype(np.float32)


def _dft_tables(n):
    idx = (np.arange(n)[:, None] * np.arange(n)[None, :]) % n
    ang = 2.0 * np.pi * idx / n
    return np.cos(ang) / np.sqrt(n), np.sin(ang) / np.sqrt(n)


def _block_diag(m, reps):
    n = m.shape[0]
    out = np.zeros((n * reps, n * reps), m.dtype)
    for r in range(reps):
        out[r * n:(r + 1) * n, r * n:(r + 1) * n] = m
    return out


def _na_bias_index():
    dr = np.zeros((3, NA_QB, NA_KB), np.int64)
    co = np.zeros((3, NA_QB, NA_KB), np.int64)
    ok = np.zeros((3, NA_QB, NA_KB), bool)
    cq = np.arange(GRID_W)
    col_start = np.clip(cq - NA_KW // 2, 0, GRID_W - NA_KW)
    col_ok = (cq[None, :] >= col_start[:, None]) & (cq[None, :] < col_start[:, None] + NA_KW)
    col_off = np.clip(cq[None, :] - cq[:, None] + NA_KW - 1, 0, 2 * NA_KW - 2)
    for typ, r0 in enumerate((0, NA_QROWS, GRID_H - NA_QROWS)):
        k0 = int(np.clip(r0 - NA_KH // 2, 0, GRID_H - NA_KROWS))
        for a in range(NA_QROWS):
            r = r0 + a
            rs = int(np.clip(r - NA_KH // 2, 0, GRID_H - NA_KH))
            for j in range(NA_KROWS):
                kr = k0 + j
                row_ok = rs <= kr < rs + NA_KH
                sl = (typ, slice(a * GRID_W, (a + 1) * GRID_W), slice(j * GRID_W, (j + 1) * GRID_W))
                dr[sl] = np.clip(kr - r + NA_KH - 1, 0, 2 * NA_KH - 2)
                co[sl] = col_off
                ok[sl] = col_ok & row_ok
    return dr, co, ok


def _mod_kernel(c_ref, w_ref, b_ref, o_ref):
    c = c_ref[...]
    s_hi, s_lo = _split_bf16(_silu(c))
    w_hi, w_lo = _split_bf16(w_ref[0])
    acc = _dot(s_hi, w_hi) + _dot(s_lo, w_hi) + _dot(s_hi, w_lo)
    o_ref[0] = acc + b_ref[0]


def _modulation(cc, w_ada, b_ada):
    depth = w_ada.shape[0]
    n_mod = w_ada.shape[2]
    tile = D_MODEL
    return pl.pallas_call(
        _mod_kernel,
        grid=(depth, n_mod // tile),
        in_specs=[
            pl.BlockSpec((16, D_MODEL), lambda l, j: (0, 0)),
            pl.BlockSpec((1, D_MODEL, tile), lambda l, j: (l, 0, j)),
            pl.BlockSpec((1, 1, tile), lambda l, j: (l, 0, j)),
        ],
        out_specs=pl.BlockSpec((1, 16, tile), lambda l, j: (l, 0, j)),
        out_shape=jax.ShapeDtypeStruct((depth, 16, n_mod), F32),
        compiler_params=pltpu.CompilerParams(
            dimension_semantics=("arbitrary", "arbitrary"), vmem_limit_bytes=VMEM_LIMIT),
        name="adaln_mod",
    )(cc, w_ada, b_ada.reshape(depth, 1, n_mod))


def _mod_rows(mod_ref, b, row0, n_rows, col0):
    rows = row0 + lax.broadcasted_iota(jnp.int32, (n_rows, 1), 0)
    m_x = mod_ref[pl.ds(b, 1), col0:col0 + D_MODEL]
    m_c = mod_ref[8:9, col0:col0 + D_MODEL]
    return jnp.where(rows >= SEQ, m_c, m_x)


def _inproj_kernel(h_ref, mod_ref, nw_ref, w_ref, cos_ref, sin_ref,
                   four_ref, nq_ref, nk_ref, nv_ref, ng_ref,
                   gq_ref, gk_ref, gv_ref, gg_ref, gz_ref, hx_scr):
    b = pl.program_id(0)
    j = pl.program_id(1)
    x = h_ref[0]
    ms = jnp.mean(x * x, axis=-1, keepdims=True)
    y = x * lax.rsqrt(ms + EPS) * nw_ref[...]
    shift = _mod_rows(mod_ref, b, j * TOK_TILE, TOK_TILE, 0)
    scale = _mod_rows(mod_ref, b, j * TOK_TILE, TOK_TILE, D_MODEL)
    hx_scr[...] = (y * (1.0 + scale) + shift).astype(BF16)

    def proj(c0, width):
        return _dot(hx_scr[...], w_ref[:, c0:c0 + width])

    four_ref[0] = proj(COL_FOUR, 2 * D_FOUR).astype(BF16)
    nq_ref[0] = proj(COL_NA, D_NA).astype(BF16)
    nk_ref[0] = proj(COL_NA + D_NA, D_NA).astype(BF16)
    nv_ref[0] = proj(COL_NA + 2 * D_NA, D_NA).astype(BF16)
    ng_ref[0] = proj(COL_NA + 3 * D_NA, D_NA).astype(BF16)
    gv_ref[0] = proj(COL_GV, D_GLA_V).astype(BF16)
    gg_ref[0] = proj(COL_GG, D_GLA_V).astype(BF16)
    gz_ref[0] = proj(COL_Z, Z_PAD)

    cos = cos_ref[...]
    sin = sin_ref[...]

    def rope(c0, out_ref, scale_out):
        u = proj(c0, GLA_KP)
        u1, u2 = u[:, :LANE], u[:, LANE:]
        out_ref[0, :, :LANE] = (u1 * cos - u2 * sin) * scale_out
        out_ref[0, :, LANE:] = (u1 * sin + u2 * cos) * scale_out

    rope(COL_GQ, gq_ref, GLA_DK ** -0.5)
    rope(COL_GK, gk_ref, 1.0)


def _inproj(h, mod_l, nw, w_in_p, cos_t, sin_t):
    nb = h.shape[0]
    nt = T_ALL // TOK_TILE
    tok = lambda width: pl.BlockSpec((1, TOK_TILE, width), lambda b, j: (b, j, 0))
    full = lambda shape: pl.BlockSpec(shape, lambda b, j: (0,) * len(shape))
    out_widths = [(2 * D_FOUR, BF16), (D_NA, BF16), (D_NA, BF16), (D_NA, BF16), (D_NA, BF16),
                  (GLA_KP, F32), (GLA_KP, F32), (D_GLA_V, BF16), (D_GLA_V, BF16), (Z_PAD, F32)]
    return pl.pallas_call(
        _inproj_kernel,
        grid=(nb, nt),
        in_specs=[
            tok(D_MODEL),
            full((16, 3 * D_MODEL)),
            full((1, D_MODEL)),
            full((D_MODEL, N_COLS)),
            pl.BlockSpec((TOK_TILE, LANE), lambda b, j: (j, 0)),
            pl.BlockSpec((TOK_TILE, LANE), lambda b, j: (j, 0)),
        ],
        out_specs=[tok(w) for w, _ in out_widths],
        out_shape=[jax.ShapeDtypeStruct((nb, T_ALL, w), dt) for w, dt in out_widths],
        scratch_shapes=[pltpu.VMEM((TOK_TILE, D_MODEL), BF16)],
        compiler_params=pltpu.CompilerParams(
            dimension_semantics=("arbitrary", "arbitrary"), vmem_limit_bytes=VMEM_LIMIT),
        name="inproj",
    )(h, mod_l, nw, w_in_p, cos_t, sin_t)


def _fourier_kernel(p_ref, cbd_ref, sbd_ref, cl_ref, sl_ref, cc_ref, sc_ref, wf_ref, o_ref,
                    uc_scr, us_scr):
    u = p_ref[0, :, :D_FOUR]
    uc_scr[...] = _dot(u, cbd_ref[...]).astype(BF16)
    us_scr[...] = _dot(u, sbd_ref[...]).astype(BF16)

    def finish(f, r0, n):
        four = _dot(f.astype(BF16), wf_ref[...])
        gate = p_ref[0, pl.ds(r0, n), D_FOUR:].astype(F32)
        o_ref[0, pl.ds(r0, n), :] = (four * _silu(gate)).astype(BF16)

    tile = 256

    def body(i, carry):
        r0 = pl.multiple_of(i * tile, tile)
        f = (_dot(cl_ref[pl.ds(r0, tile), :], uc_scr[:SEQ, :])
             - _dot(sl_ref[pl.ds(r0, tile), :], us_scr[:SEQ, :]))
        finish(f, r0, tile)
        return carry

    lax.fori_loop(0, SEQ // tile, body, 0)
    f_c = _dot(cc_ref[...], uc_scr[SEQ:, :]) - _dot(sc_ref[...], us_scr[SEQ:, :])
    finish(f_c, SEQ, CTX_LEN)


def _fourier(p_four, tabs, wf):
    nb = p_four.shape[0]
    full = lambda a: pl.BlockSpec(a.shape, lambda b: (0,) * a.ndim)
    return pl.pallas_call(
        _fourier_kernel,
        grid=(nb,),
        in_specs=[pl.BlockSpec((1, T_ALL, 2 * D_FOUR), lambda b: (b, 0, 0))]
        + [full(t) for t in tabs] + [full(wf)],
        out_specs=pl.BlockSpec((1, T_ALL, D_FOUR), lambda b: (b, 0, 0)),
        out_shape=jax.ShapeDtypeStruct((nb, T_ALL, D_FOUR), BF16),
        scratch_shapes=[pltpu.VMEM((T_ALL, D_FOUR), BF16), pltpu.VMEM((T_ALL, D_FOUR), BF16)],
        compiler_params=pltpu.CompilerParams(
            dimension_semantics=("arbitrary",), vmem_limit_bytes=VMEM_LIMIT),
        name="fourier_mix",
    )(p_four, *tabs, wf)


def _na_kernel(q_ref, k_ref, v_ref, g_ref, bias_ref, o_ref):
    j = pl.program_id(1)
    lane = lax.broadcasted_iota(jnp.int32, (1, LANE), 1)
    first_head = lane < NA_HEAD_DIM

    def attend(local):
        if local:
            k0 = pl.multiple_of(jnp.clip(j * NA_QROWS - NA_KH // 2, 0, GRID_H - NA_KROWS) * GRID_W,
                                GRID_W)
            typ = jnp.where(j == 0, 0, jnp.where(j == NA_NBLK - 1, 2, 1))
        for p in range(NA_HEADS // 2):
            cols = slice(p * LANE, (p + 1) * LANE)
            qp = q_ref[0, :, cols]
            kc = k_ref[0, SEQ:, cols]
            vc = v_ref[0, SEQ:, cols]
            if local:
                kl = k_ref[0, pl.ds(k0, NA_KB), cols]
                vl = v_ref[0, pl.ds(k0, NA_KB), cols]
            acc = None
            for hh in range(2):
                keep = first_head if hh == 0 else jnp.logical_not(first_head)
                qm = jnp.where(keep, qp, jnp.zeros_like(qp))
                s_c = _dot_nt(qm, kc)
                m = jnp.max(s_c, axis=-1, keepdims=True)
                if local:
                    s_l = _dot_nt(qm, kl) + bias_ref[(2 * p + hh) * 3 + typ]
                    m = jnp.maximum(m, jnp.max(s_l, axis=-1, keepdims=True))
                e_c = jnp.exp(s_c - m)
                den = jnp.sum(e_c, axis=-1, keepdims=True)
                o = _dot(e_c.astype(BF16), vc)
                if local:
                    e_l = jnp.exp(s_l - m)
                    den = den + jnp.sum(e_l, axis=-1, keepdims=True)
                    o = o + _dot(e_l.astype(BF16), vl)
                o = o / den
                acc = o if hh == 0 else jnp.where(first_head, acc, o)
            gate = g_ref[0, :, cols].astype(F32)
            o_ref[0, :, cols] = (acc * _silu(gate)).astype(BF16)

    @pl.when(j < NA_NBLK)
    def _():
        attend(True)

    @pl.when(j == NA_NBLK)
    def _():
        attend(False)


def _na(nq, nk, nv, ng, bias):
    nb = nq.shape[0]
    blk = pl.BlockSpec((1, NA_QB, D_NA), lambda b, j: (b, j, 0))
    seq = pl.BlockSpec((1, T_ALL, D_NA), lambda b, j: (b, 0, 0))
    return pl.pallas_call(
        _na_kernel,
        grid=(nb, NA_NBLK + 1),
        in_specs=[blk, seq, seq, blk, pl.BlockSpec(bias.shape, lambda b, j: (0, 0, 0))],
        out_specs=blk,
        out_shape=jax.ShapeDtypeStruct((nb, T_ALL, D_NA), BF16),
        compiler_params=pltpu.CompilerParams(
            dimension_semantics=("arbitrary", "arbitrary"), vmem_limit_bytes=VMEM_LIMIT),
        name="nbr_attention",
    )(nq, nk, nv, ng, bias)


def _gla_kernel(q_ref, k_ref, v_ref, g_ref, z_ref, wa_ref, ba_ref, nw_ref, o_ref,
                st_scr, of_scr, ob_scr):
    C = GLA_CHUNK
    kl = lax.broadcasted_iota(jnp.int32, (1, GLA_KP), 1) % LANE
    k_head = jnp.where(kl < GLA_HEADS * (GLA_DK // 2), kl // (GLA_DK // 2), GLA_HEADS)
    v_head = lax.broadcasted_iota(jnp.int32, (1, D_GLA_V), 1) // GLA_DV
    st_mask = (lax.broadcasted_iota(jnp.int32, (D_GLA_V, 1), 0) // GLA_DV) == k_head
    ti = lax.broadcasted_iota(jnp.int32, (C, C), 0)
    si = lax.broadcasted_iota(jnp.int32, (C, C), 1)
    ti4 = lax.broadcasted_iota(jnp.int32, (C, GLA_HEADS * C), 0)
    si4 = lax.broadcasted_iota(jnp.int32, (C, GLA_HEADS * C), 1) % C

    st_scr[...] = jnp.zeros_like(st_scr)

    def chunk(off, d):
        fwd = d == 0
        zc = z_ref[0, pl.ds(off, C), :].astype(BF16)
        g = _dot(zc, wa_ref[:, d * GLA_KP:(d + 1) * GLA_KP]) + ba_ref[:, d * GLA_KP:(d + 1) * GLA_KP]
        la = (jnp.minimum(g, 0.0) - jnp.log1p(jnp.exp(-jnp.abs(g)))) * (1.0 / GLA_TAU)
        la_hi, la_lo = _split_bf16(la)
        tri = jnp.where((si <= ti) if fwd else (si >= ti), 1.0, 0.0).astype(BF16)
        bcum = _dot(tri, la_hi) + _dot(tri, la_lo)
        b_end = bcum[C - 1:C, :] if fwd else bcum[0:1, :]
        q = q_ref[0, pl.ds(off, C), :]
        k = k_ref[0, pl.ds(off, C), :]
        v = v_ref[0, pl.ds(off, C), :]
        q_dec = (q * jnp.exp(bcum)).astype(BF16)
        k_dec = k * jnp.exp(-bcum)
        k_end = (k * jnp.exp(b_end - bcum)).astype(BF16)
        decay = jnp.exp(b_end)
        k_bd = jnp.concatenate(
            [jnp.where(k_head == h, k_dec, 0.0) for h in range(GLA_HEADS)], axis=0).astype(BF16)
        a = _dot_nt(q_dec, k_bd)
        a = jnp.where((si4 <= ti4) if fwd else (si4 >= ti4), a, 0.0).astype(BF16)
        v_bd = jnp.concatenate(
            [jnp.where(v_head == h, v, jnp.zeros_like(v)) for h in range(GLA_HEADS)], axis=0)
        st = st_scr[d]
        o = _dot(a, v_bd) + _dot_nt(q_dec, st.astype(BF16))
        upd = _dot(v.astype(F32).T.astype(BF16), k_end)
        st_scr[d] = st * decay + jnp.where(st_mask, upd, 0.0)
        return o

    n_all = N_CHUNKS_C + N_CHUNKS_X

    def body(i, carry):
        in_ctx = i < N_CHUNKS_C
        off_f = jnp.where(in_ctx, SEQ + i * C, (i - N_CHUNKS_C) * C)
        off_b = jnp.where(in_ctx, SEQ + (N_CHUNKS_C - 1 - i) * C, (n_all - 1 - i) * C)
        off_f = pl.multiple_of(off_f, C)
        off_b = pl.multiple_of(off_b, C)
        of_scr[pl.ds(off_f, C), :] = chunk(off_f, 0)
        ob_scr[pl.ds(off_b, C), :] = chunk(off_b, 1)
        return carry

    lax.fori_loop(0, n_all, body, 0)

    head_ones = jnp.where(
        (lax.broadcasted_iota(jnp.int32, (D_GLA_V, 1), 0) // GLA_DV) == v_head, 1.0, 0.0).astype(BF16)
    tile = 256

    def fin(i, carry):
        r0 = pl.multiple_of(i * tile, tile)
        o = of_scr[pl.ds(r0, tile), :] + ob_scr[pl.ds(r0, tile), :]
        sq_hi, sq_lo = _split_bf16(o * o)
        ms = (_dot(sq_hi, head_ones) + _dot(sq_lo, head_ones)) * (1.0 / GLA_DV)
        y = o * lax.rsqrt(ms + EPS) * nw_ref[...]
        gate = g_ref[0, pl.ds(r0, tile), :].astype(F32)
        o_ref[0, pl.ds(r0, tile), :] = (y * _silu(gate)).astype(BF16)
        return carry

    lax.fori_loop(0, T_ALL // tile, fin, 0)


def _gla(gq, gk, gv, gg, gz, wa, ba, gnw):
    nb = gq.shape[0]
    seq = lambda w: pl.BlockSpec((1, T_ALL, w), lambda b: (b, 0, 0))
    full = lambda a: pl.BlockSpec(a.shape, lambda b: (0,) * a.ndim)
    return pl.pallas_call(
        _gla_kernel,
        grid=(nb,),
        in_specs=[seq(GLA_KP), seq(GLA_KP), seq(D_GLA_V), seq(D_GLA_V), seq(Z_PAD),
                  full(wa), full(ba), full(gnw)],
        out_specs=seq(D_GLA_V),
        out_shape=jax.ShapeDtypeStruct((nb, T_ALL, D_GLA_V), BF16),
        scratch_shapes=[pltpu.VMEM((2, D_GLA_V, GLA_KP), F32),
                        pltpu.VMEM((T_ALL, D_GLA_V), F32),
                        pltpu.VMEM((T_ALL, D_GLA_V), F32)],
        compiler_params=pltpu.CompilerParams(
            dimension_semantics=("arbitrary",), vmem_limit_bytes=VMEM_LIMIT),
        name="gla_mix",
    )(gq, gk, gv, gg, gz, wa, ba, gnw)


def _outproj_kernel(h_ref, mf_ref, mn_ref, mg_ref, w_ref, mod_ref, nf_ref, o_ref, *, tile, final):
    b = pl.program_id(0)
    j = pl.program_id(1)
    acc = (_dot(mf_ref[0], w_ref[:D_FOUR, :])
           + _dot(mn_ref[0], w_ref[D_FOUR:D_FOUR + D_NA, :])
           + _dot(mg_ref[0], w_ref[D_FOUR + D_NA:, :]))
    gate = _mod_rows(mod_ref, b, j * tile, tile, 2 * D_MODEL)
    x = h_ref[0] + gate * acc
    if final:
        ms = jnp.mean(x * x, axis=-1, keepdims=True)
        x = x * lax.rsqrt(ms + EPS) * nf_ref[...]
    o_ref[0] = x


def _outproj(h, mf, mn, mg, w_out, mod_l, nf, final):
    nb = h.shape[0]
    tile = 512 if final else TOK_TILE
    rows = SEQ if final else T_ALL
    tok = lambda width: pl.BlockSpec((1, tile, width), lambda b, j: (b, j, 0))
    full = lambda shape: pl.BlockSpec(shape, lambda b, j: (0,) * len(shape))
    return pl.pallas_call(
        functools.partial(_outproj_kernel, tile=tile, final=final),
        grid=(nb, rows // tile),
        in_specs=[tok(D_MODEL), tok(D_FOUR), tok(D_NA), tok(D_GLA_V),
                  full((D_MODEL, D_MODEL)), full((16, 3 * D_MODEL)), full((1, D_MODEL))],
        out_specs=tok(D_MODEL),
        out_shape=jax.ShapeDtypeStruct((nb, rows, D_MODEL), F32),
        compiler_params=pltpu.CompilerParams(
            dimension_semantics=("arbitrary", "arbitrary"), vmem_limit_bytes=VMEM_LIMIT),
        name="outproj_final" if final else "outproj",
    )(h, mf, mn, mg, w_out, mod_l, nf)


def _permute_w_in(w_in):
    sizes = (D_FOUR, D_FOUR, D_NA, D_NA, D_NA, D_NA, D_GLA_K, D_GLA_K, D_GLA_V, D_GLA_V,
             GLA_RANK, GLA_RANK)
    offs = np.concatenate([[0], np.cumsum(sizes)])
    part = lambda i: w_in[:, :, offs[i]:offs[i + 1]]
    depth = w_in.shape[0]
    dst = _gla_qk_perm()

    def qk(i):
        return jnp.zeros((depth, D_MODEL, GLA_KP), w_in.dtype).at[:, :, dst].set(part(i))

    z = jnp.zeros((depth, D_MODEL, Z_PAD), w_in.dtype)
    z = z.at[:, :, :GLA_RANK].set(part(10)).at[:, :, GLA_RANK:2 * GLA_RANK].set(part(11))
    cols = [part(0), part(1), part(2) * (NA_HEAD_DIM ** -0.5), part(3), part(4), part(5),
            qk(6), qk(7), part(8), part(9), z]
    return jnp.concatenate(cols, axis=-1).astype(BF16)


def _decay_weights(w_f, b_f, w_b, b_b):
    depth = w_f.shape[0]
    dst = _gla_qk_perm()
    wa = jnp.zeros((depth, Z_PAD, 2 * GLA_KP), F32)
    wa = wa.at[:, :GLA_RANK, dst].set(w_f)
    wa = wa.at[:, GLA_RANK:2 * GLA_RANK, GLA_KP + dst].set(w_b)
    ba = jnp.zeros((depth, 1, 2 * GLA_KP), F32)
    ba = ba.at[:, 0, dst].set(b_f).at[:, 0, GLA_KP + dst].set(b_b)
    return wa.astype(BF16), ba


def kernel(x, c, ctx, c_ctx, w_ada, b_ada, norm_w, w_in, w_four, rpb, w_alpha_fwd, b_alpha_fwd,
           w_alpha_bwd, b_alpha_bwd, gla_norm_w, w_out, norm_f):
    nb = x.shape[0]
    depth = w_in.shape[0]
    assert x.shape == (nb, SEQ, D_MODEL) and ctx.shape == (nb, CTX_LEN, D_MODEL) and nb <= 8

    w_in_p = _permute_w_in(w_in)
    w_out_b = w_out.astype(BF16)
    w_four_b = w_four.astype(BF16)
    wa, ba = _decay_weights(w_alpha_fwd, b_alpha_fwd, w_alpha_bwd, b_alpha_bwd)
    gnw = jnp.tile(gla_norm_w, (1, GLA_HEADS)).reshape(depth, 1, D_GLA_V)
    dr, co, ok = _na_bias_index()
    bias = jnp.where(ok[None, None], rpb[:, :, dr, co], -jnp.inf)
    bias = bias.reshape(depth, NA_HEADS * 3, NA_QB, NA_KB)

    cos_t, sin_t = (jnp.asarray(t) for t in _rope_tables())
    c64, s64 = _dft_tables(FOUR_DIM)
    cl, sl = _dft_tables(SEQ)
    ccx, scx = _dft_tables(CTX_LEN)
    tabs = [jnp.asarray(t, dtype=F32).astype(BF16) for t in
            (_block_diag(c64, FOUR_GROUPS), _block_diag(s64, FOUR_GROUPS), cl, sl, ccx, scx)]

    cc = jnp.zeros((16, D_MODEL), F32).at[:nb].set(c).at[8].set(c_ctx)
    mod = _modulation(cc, w_ada, b_ada)

    h = jnp.concatenate([x, ctx], axis=1)
    for l in range(depth):
        final = l == depth - 1
        (p_four, nq, nk, nv, ng, gq, gk, gv, gg, gz) = _inproj(
            h, mod[l], norm_w[l].reshape(1, D_MODEL), w_in_p[l], cos_t, sin_t)
        mf = _fourier(p_four, tabs, w_four_b[l])
        mn = _na(nq, nk, nv, ng, bias[l])
        mg = _gla(gq, gk, gv, gg, gz, wa[l], ba[l], gnw[l])
        h = _outproj(h, mf, mn, mg, w_out_b[l], mod[l], norm_f.reshape(1, D_MODEL), final)
    return h
```

```python
import functools

import numpy as np
import jax
import jax.numpy as jnp
from jax import lax
from jax.experimental import pallas as pl
from jax.experimental.pallas import tpu as pltpu

F32 = jnp.float32
BF16 = jnp.bfloat16

D_MODEL = 1024
SEQ = 2048
CTX_LEN = 256
T_ALL = SEQ + CTX_LEN
GRID_W = 64
GRID_H = SEQ // GRID_W
EPS = 1e-6

FOUR_GROUPS, FOUR_DIM = 4, 64
D_FOUR = FOUR_GROUPS * FOUR_DIM
NA_HEADS, NA_HEAD_DIM = 6, 64
D_NA = NA_HEADS * NA_HEAD_DIM
NA_KH, NA_KW = 8, 16
GLA_HEADS, GLA_DK, GLA_DV = 4, 48, 96
D_GLA_K = GLA_HEADS * GLA_DK
D_GLA_V = GLA_HEADS * GLA_DV
GLA_RANK = 16
GLA_TAU = 16.0
GLA_CHUNK = 64
ROPE_BASE = 10000.0

LANE = 128
GLA_KP = 2 * LANE
Z_PAD = LANE
N_CHUNKS_X = SEQ // GLA_CHUNK
N_CHUNKS_C = CTX_LEN // GLA_CHUNK

NA_QROWS = 4
NA_KROWS = NA_QROWS + NA_KH - 1
NA_QB = NA_QROWS * GRID_W
NA_KB = NA_KROWS * GRID_W
NA_NBLK = GRID_H // NA_QROWS

TOK_TILE = 768
VMEM_LIMIT = 56 * 1024 * 1024

COL_FOUR = 0
COL_NA = COL_FOUR + 2 * D_FOUR
COL_GQ = COL_NA + 4 * D_NA
COL_GK = COL_GQ + GLA_KP
COL_GV = COL_GK + GLA_KP
COL_GG = COL_GV + D_GLA_V
COL_Z = COL_GG + D_GLA_V
N_COLS = COL_Z + Z_PAD


def _dot(a, b):
    return jnp.dot(a, b, preferred_element_type=F32)


def _dot_nt(a, b):
    return lax.dot_general(a, b, (((1,), (1,)), ((), ())), preferred_element_type=F32)


def _silu(x):
    return x / (1.0 + jnp.exp(-x))


def _split_bf16(x):
    hi = x.astype(BF16)
    lo = (x - hi.astype(F32)).astype(BF16)
    return hi, lo


def _rope_tables():
    n_freq = GLA_DK // 4
    inv = ROPE_BASE ** (-np.arange(n_freq, dtype=np.float64) / n_freq)
    pos = np.arange(SEQ)
    prow, pcol = pos // GRID_W, pos % GRID_W
    cos = np.ones((T_ALL, LANE), np.float64)
    sin = np.zeros((T_ALL, LANE), np.float64)
    for h in range(GLA_HEADS):
        for half, p in enumerate((prow, pcol)):
            ang = p[:, None] * inv[None, :]
            lo = h * 2 * n_freq + half * n_freq
            cos[:SEQ, lo:lo + n_freq] = np.cos(ang)
            sin[:SEQ, lo:lo + n_freq] = np.sin(ang)
    return cos.astype(np.float32), sin.astype(np.float32)


def _dft_tables(n):
    idx = (np.arange(n)[:, None] * np.arange(n)[None, :]) % n
    ang = 2.0 * np.pi * idx / n
    return np.cos(ang) / np.sqrt(n), np.sin(ang) / np.sqrt(n)


def _block_diag(m, reps):
    n = m.shape[0]
    out = np.zeros((n * reps, n * reps), m.dtype)
    for r in range(reps):
        out[r * n:(r + 1) * n, r * n:(r + 1) * n] = m
    return out


def _na_row_offsets():
    out = []
    for r0 in (0, NA_QROWS, GRID_H - NA_QROWS):
        k0 = int(np.clip(r0 - NA_KH // 2, 0, GRID_H - NA_KROWS))
        per_type = []
        for a in range(NA_QROWS):
            r = r0 + a
            rs = int(np.clip(r - NA_KH // 2, 0, GRID_H - NA_KH))
            per_type.append([(k0 + j) - r + NA_KH - 1 if rs <= k0 + j < rs + NA_KH else None
                             for j in range(NA_KROWS)])
        out.append(per_type)
    return out


def _mod_kernel(c_ref, w_ref, b_ref, o_ref):
    c = c_ref[...]
    s_hi, s_lo = _split_bf16(_silu(c))
    w_hi, w_lo = _split_bf16(w_ref[0])
    acc = _dot(s_hi, w_hi) + _dot(s_lo, w_hi) + _dot(s_hi, w_lo)
    o_ref[0] = acc + b_ref[0]


def _modulation(cc, w_ada, b_ada):
    depth = w_ada.shape[0]
    n_mod = w_ada.shape[2]
    tile = D_MODEL
    return pl.pallas_call(
        _mod_kernel,
        grid=(depth, n_mod // tile),
        in_specs=[
            pl.BlockSpec((16, D_MODEL), lambda l, j: (0, 0)),
            pl.BlockSpec((1, D_MODEL, tile), lambda l, j: (l, 0, j)),
            pl.BlockSpec((1, 1, tile), lambda l, j: (l, 0, j)),
        ],
        out_specs=pl.BlockSpec((1, 16, tile), lambda l, j: (l, 0, j)),
        out_shape=jax.ShapeDtypeStruct((depth, 16, n_mod), F32),
        compiler_params=pltpu.CompilerParams(
            dimension_semantics=("arbitrary", "arbitrary"), vmem_limit_bytes=VMEM_LIMIT),
        name="adaln_mod",
    )(cc, w_ada, b_ada.reshape(depth, 1, n_mod))


def _mod_rows(mod_ref, b, row0, n_rows, col0):
    rows = row0 + lax.broadcasted_iota(jnp.int32, (n_rows, 1), 0)
    m_x = mod_ref[pl.ds(b, 1), col0:col0 + D_MODEL]
    m_c = mod_ref[8:9, col0:col0 + D_MODEL]
    return jnp.where(rows >= SEQ, m_c, m_x)


def _inproj_kernel(h_ref, mod_ref, nw_ref, w_ref, cos_ref, sin_ref,
                   four_ref, nq_ref, nk_ref, nv_ref, ng_ref,
                   gq_ref, gk_ref, gv_ref, gg_ref, gz_ref, hx_scr):
    b = pl.program_id(0)
    j = pl.program_id(1)
    x = h_ref[0]
    ms = jnp.mean(x * x, axis=-1, keepdims=True)
    y = x * lax.rsqrt(ms + EPS) * nw_ref[...]
    shift = _mod_rows(mod_ref, b, j * TOK_TILE, TOK_TILE, 0)
    scale = _mod_rows(mod_ref, b, j * TOK_TILE, TOK_TILE, D_MODEL)
    hx_scr[...] = (y * (1.0 + scale) + shift).astype(BF16)

    def proj(c0, width):
        return _dot(hx_scr[...], w_ref[:, c0:c0 + width])

    four_ref[0] = proj(COL_FOUR, 2 * D_FOUR).astype(BF16)
    nq_ref[0] = proj(COL_NA, D_NA).astype(BF16)
    nk_ref[0] = proj(COL_NA + D_NA, D_NA).astype(BF16)
    nv_ref[0] = proj(COL_NA + 2 * D_NA, D_NA).astype(BF16)
    ng_ref[0] = proj(COL_NA + 3 * D_NA, D_NA).astype(BF16)
    gv_ref[0] = proj(COL_GV, D_GLA_V).astype(BF16)
    gg_ref[0] = proj(COL_GG, D_GLA_V).astype(BF16)
    gz_ref[0] = proj(COL_Z, Z_PAD)

    cos = cos_ref[...]
    sin = sin_ref[...]

    def rope(c0, out_ref, scale_out):
        u = proj(c0, GLA_KP)
        u1, u2 = u[:, :LANE], u[:, LANE:]
        out_ref[0, :, :LANE] = (u1 * cos - u2 * sin) * scale_out
        out_ref[0, :, LANE:] = (u1 * sin + u2 * cos) * scale_out

    rope(COL_GQ, gq_ref, GLA_DK ** -0.5)
    rope(COL_GK, gk_ref, 1.0)


def _inproj(h, mod_l, nw, w_in_p, cos_t, sin_t):
    nb = h.shape[0]
    nt = T_ALL // TOK_TILE
    tok = lambda width: pl.BlockSpec((1, TOK_TILE, width), lambda b, j: (b, j, 0))
    full = lambda shape: pl.BlockSpec(shape, lambda b, j: (0,) * len(shape))
    out_widths = [(2 * D_FOUR, BF16), (D_NA, BF16), (D_NA, BF16), (D_NA, BF16), (D_NA, BF16),
                  (GLA_KP, F32), (GLA_KP, F32), (D_GLA_V, BF16), (D_GLA_V, BF16), (Z_PAD, F32)]
    return pl.pallas_call(
        _inproj_kernel,
        grid=(nb, nt),
        in_specs=[
            tok(D_MODEL),
            full((16, 3 * D_MODEL)),
            full((1, D_MODEL)),
            full((D_MODEL, N_COLS)),
            pl.BlockSpec((TOK_TILE, LANE), lambda b, j: (j, 0)),
            pl.BlockSpec((TOK_TILE, LANE), lambda b, j: (j, 0)),
        ],
        out_specs=[tok(w) for w, _ in out_widths],
        out_shape=[jax.ShapeDtypeStruct((nb, T_ALL, w), dt) for w, dt in out_widths],
        scratch_shapes=[pltpu.VMEM((TOK_TILE, D_MODEL), BF16)],
        compiler_params=pltpu.CompilerParams(
            dimension_semantics=("arbitrary", "arbitrary"), vmem_limit_bytes=VMEM_LIMIT),
        name="inproj",
    )(h, mod_l, nw, w_in_p, cos_t, sin_t)


def _fourier_kernel(p_ref, cbd_ref, sbd_ref, cl_ref, sl_ref, cc_ref, sc_ref, wf_ref, o_ref,
                    uc_scr, us_scr):
    u = p_ref[0, :, :D_FOUR]
    uc_scr[...] = _dot(u, cbd_ref[...]).astype(BF16)
    us_scr[...] = _dot(u, sbd_ref[...]).astype(BF16)

    def finish(f, r0, n):
        four = _dot(f.astype(BF16), wf_ref[...])
        gate = p_ref[0, pl.ds(r0, n), D_FOUR:].astype(F32)
        o_ref[0, pl.ds(r0, n), :] = (four * _silu(gate)).astype(BF16)

    tile = 256

    def body(i, carry):
        r0 = pl.multiple_of(i * tile, tile)
        f = (_dot(cl_ref[pl.ds(r0, tile), :], uc_scr[:SEQ, :])
             - _dot(sl_ref[pl.ds(r0, tile), :], us_scr[:SEQ, :]))
        finish(f, r0, tile)
        return carry

    lax.fori_loop(0, SEQ // tile, body, 0)
    f_c = _dot(cc_ref[...], uc_scr[SEQ:, :]) - _dot(sc_ref[...], us_scr[SEQ:, :])
    finish(f_c, SEQ, CTX_LEN)


def _fourier(p_four, tabs, wf):
    nb = p_four.shape[0]
    full = lambda a: pl.BlockSpec(a.shape, lambda b: (0,) * a.ndim)
    return pl.pallas_call(
        _fourier_kernel,
        grid=(nb,),
        in_specs=[pl.BlockSpec((1, T_ALL, 2 * D_FOUR), lambda b: (b, 0, 0))]
        + [full(t) for t in tabs] + [full(wf)],
        out_specs=pl.BlockSpec((1, T_ALL, D_FOUR), lambda b: (b, 0, 0)),
        out_shape=jax.ShapeDtypeStruct((nb, T_ALL, D_FOUR), BF16),
        scratch_shapes=[pltpu.VMEM((T_ALL, D_FOUR), BF16), pltpu.VMEM((T_ALL, D_FOUR), BF16)],
        compiler_params=pltpu.CompilerParams(
            dimension_semantics=("arbitrary",), vmem_limit_bytes=VMEM_LIMIT),
        name="fourier_mix",
    )(p_four, *tabs, wf)


def _na_build_bias(ra_ref, rb_ref, tbl_ref):
    w = GRID_W
    cq = lax.broadcasted_iota(jnp.int32, (w, 2 * w), 0)
    ln = lax.broadcasted_iota(jnp.int32, (w, 2 * w), 1)
    ck = ln % w
    cs = jnp.clip(cq - NA_KW // 2, 0, w - NA_KW)
    col_ok = (ck >= cs) & (ck < cs + NA_KW)
    left = ln < w
    neg = jnp.full((w, 2 * w), -jnp.inf, F32)
    offs = _na_row_offsets()
    cache = {}
    for typ in range(3):
        for a in range(NA_QROWS):
            for jp in range((NA_KROWS + 1) // 2):
                d0 = offs[typ][a][2 * jp]
                d1 = offs[typ][a][2 * jp + 1] if 2 * jp + 1 < NA_KROWS else None
                width = min(2 * w, NA_KB - jp * 2 * w)
                for h in range(NA_HEADS):
                    key = (h, d0, d1)
                    if key not in cache:
                        if d0 is None and d1 is None:
                            cache[key] = neg
                        else:
                            ok = col_ok
                            if d0 is None:
                                row = rb_ref[h, d1:d1 + 1, :]
                                ok = ok & jnp.logical_not(left)
                            elif d1 is None:
                                row = ra_ref[h, d0:d0 + 1, :]
                                ok = ok & left
                            else:
                                row = ra_ref[h, d0:d0 + 1, :] + rb_ref[h, d1:d1 + 1, :]
                            skew = pltpu.roll(jnp.broadcast_to(row, (w, 2 * w)), 2 * w - (NA_KW - 1), 1,
                                              stride=1, stride_axis=0)
                            cache[key] = jnp.where(ok, skew, neg)
                    tbl_ref[h * 3 + typ, a * w:(a + 1) * w, jp * 2 * w:jp * 2 * w + width] = (
                        cache[key][:, :width])


def _na_kernel(q_ref, k_ref, v_ref, g_ref, ra_ref, rb_ref, o_ref, bias_ref):
    j = pl.program_id(1)
    lane = lax.broadcasted_iota(jnp.int32, (1, LANE), 1)
    first_head = lane < NA_HEAD_DIM

    @pl.when((pl.program_id(0) == 0) & (j == 0))
    def _():
        _na_build_bias(ra_ref, rb_ref, bias_ref)

    def attend(local):
        if local:
            k0 = pl.multiple_of(jnp.clip(j * NA_QROWS - NA_KH // 2, 0, GRID_H - NA_KROWS) * GRID_W,
                                GRID_W)
            typ = jnp.where(j == 0, 0, jnp.where(j == NA_NBLK - 1, 2, 1))
        for p in range(NA_HEADS // 2):
            cols = slice(p * LANE, (p + 1) * LANE)
            qp = q_ref[0, :, cols]
            kc = k_ref[0, SEQ:, cols]
            vc = v_ref[0, SEQ:, cols]
            if local:
                kl = k_ref[0, pl.ds(k0, NA_KB), cols]
                vl = v_ref[0, pl.ds(k0, NA_KB), cols]
            acc = None
            for hh in range(2):
                keep = first_head if hh == 0 else jnp.logical_not(first_head)
                qm = jnp.where(keep, qp, jnp.zeros_like(qp))
                s_c = _dot_nt(qm, kc)
                m = jnp.max(s_c, axis=-1, keepdims=True)
                if local:
                    s_l = _dot_nt(qm, kl) + bias_ref[(2 * p + hh) * 3 + typ]
                    m = jnp.maximum(m, jnp.max(s_l, axis=-1, keepdims=True))
                e_c = jnp.exp(s_c - m)
                den = jnp.sum(e_c, axis=-1, keepdims=True)
                o = _dot(e_c.astype(BF16), vc)
                if local:
                    e_l = jnp.exp(s_l - m)
                    den = den + jnp.sum(e_l, axis=-1, keepdims=True)
                    o = o + _dot(e_l.astype(BF16), vl)
                o = o / den
                acc = o if hh == 0 else jnp.where(first_head, acc, o)
            gate = g_ref[0, :, cols].astype(F32)
            o_ref[0, :, cols] = (acc * _silu(gate)).astype(BF16)

    @pl.when(j < NA_NBLK)
    def _():
        attend(True)

    @pl.when(j == NA_NBLK)
    def _():
        attend(False)


def _na(nq, nk, nv, ng, rpb_a, rpb_b):
    nb = nq.shape[0]
    blk = pl.BlockSpec((1, NA_QB, D_NA), lambda b, j: (b, j, 0))
    seq = pl.BlockSpec((1, T_ALL, D_NA), lambda b, j: (b, 0, 0))
    tab = pl.BlockSpec(rpb_a.shape, lambda b, j: (0, 0, 0))
    return pl.pallas_call(
        _na_kernel,
        grid=(nb, NA_NBLK + 1),
        in_specs=[blk, seq, seq, blk, tab, tab],
        out_specs=blk,
        out_shape=jax.ShapeDtypeStruct((nb, T_ALL, D_NA), BF16),
        scratch_shapes=[pltpu.VMEM((NA_HEADS * 3, NA_QB, NA_KB), F32)],
        compiler_params=pltpu.CompilerParams(
            dimension_semantics=("arbitrary", "arbitrary"), vmem_limit_bytes=VMEM_LIMIT),
        name="nbr_attention",
    )(nq, nk, nv, ng, rpb_a, rpb_b)


def _gla_kernel(q_ref, k_ref, v_ref, g_ref, z_ref, wa_ref, ba_ref, nw_ref, o_ref,
                st_scr, of_scr, ob_scr):
    C = GLA_CHUNK
    kl = lax.broadcasted_iota(jnp.int32, (1, GLA_KP), 1) % LANE
    k_head = jnp.where(kl < GLA_HEADS * (GLA_DK // 2), kl // (GLA_DK // 2), GLA_HEADS)
    v_head = lax.broadcasted_iota(jnp.int32, (1, D_GLA_V), 1) // GLA_DV
    st_mask = (lax.broadcasted_iota(jnp.int32, (D_GLA_V, 1), 0) // GLA_DV) == k_head
    ti = lax.broadcasted_iota(jnp.int32, (C, C), 0)
    si = lax.broadcasted_iota(jnp.int32, (C, C), 1)
    ti4 = lax.broadcasted_iota(jnp.int32, (C, GLA_HEADS * C), 0)
    si4 = lax.broadcasted_iota(jnp.int32, (C, GLA_HEADS * C), 1) % C

    st_scr[...] = jnp.zeros_like(st_scr)

    def chunk(off, d):
        fwd = d == 0
        zc = z_ref[0, pl.ds(off, C), :].astype(BF16)
        g = _dot(zc, wa_ref[:, d * GLA_KP:(d + 1) * GLA_KP]) + ba_ref[:, d * GLA_KP:(d + 1) * GLA_KP]
        la = (jnp.minimum(g, 0.0) - jnp.log1p(jnp.exp(-jnp.abs(g)))) * (1.0 / GLA_TAU)
        la_hi, la_lo = _split_bf16(la)
        tri = jnp.where((si <= ti) if fwd else (si >= ti), 1.0, 0.0).astype(BF16)
        bcum = _dot(tri, la_hi) + _dot(tri, la_lo)
        b_end = bcum[C - 1:C, :] if fwd else bcum[0:1, :]
        q = q_ref[0, pl.ds(off, C), :]
        k = k_ref[0, pl.ds(off, C), :]
        v = v_ref[0, pl.ds(off, C), :]
        q_dec = (q * jnp.exp(bcum)).astype(BF16)
        k_dec = k * jnp.exp(-bcum)
        k_end = (k * jnp.exp(b_end - bcum)).astype(BF16)
        decay = jnp.exp(b_end)
        k_bd = jnp.concatenate(
            [jnp.where(k_head == h, k_dec, 0.0) for h in range(GLA_HEADS)], axis=0).astype(BF16)
        a = _dot_nt(q_dec, k_bd)
        a = jnp.where((si4 <= ti4) if fwd else (si4 >= ti4), a, 0.0).astype(BF16)
        v_bd = jnp.concatenate(
            [jnp.where(v_head == h, v, jnp.zeros_like(v)) for h in range(GLA_HEADS)], axis=0)
        st = st_scr[d]
        o = _dot(a, v_bd) + _dot_nt(q_dec, st.astype(BF16))
        upd = _dot(v.astype(F32).T.astype(BF16), k_end)
        st_scr[d] = st * decay + jnp.where(st_mask, upd, 0.0)
        return o

    n_all = N_CHUNKS_C + N_CHUNKS_X

    def body(i, carry):
        in_ctx = i < N_CHUNKS_C
        off_f = jnp.where(in_ctx, SEQ + i * C, (i - N_CHUNKS_C) * C)
        off_b = jnp.where(in_ctx, SEQ + (N_CHUNKS_C - 1 - i) * C, (n_all - 1 - i) * C)
        off_f = pl.multiple_of(off_f, C)
        off_b = pl.multiple_of(off_b, C)
        of_scr[pl.ds(off_f, C), :] = chunk(off_f, 0)
        ob_scr[pl.ds(off_b, C), :] = chunk(off_b, 1)
        return carry

    lax.fori_loop(0, n_all, body, 0)

    head_ones = jnp.where(
        (lax.broadcasted_iota(jnp.int32, (D_GLA_V, 1), 0) // GLA_DV) == v_head, 1.0, 0.0).astype(BF16)
    tile = 256

    def fin(i, carry):
        r0 = pl.multiple_of(i * tile, tile)
        o = of_scr[pl.ds(r0, tile), :] + ob_scr[pl.ds(r0, tile), :]
        sq_hi, sq_lo = _split_bf16(o * o)
        ms = (_dot(sq_hi, head_ones) + _dot(sq_lo, head_ones)) * (1.0 / GLA_DV)
        y = o * lax.rsqrt(ms + EPS) * nw_ref[...]
        gate = g_ref[0, pl.ds(r0, tile), :].astype(F32)
        o_ref[0, pl.ds(r0, tile), :] = (y * _silu(gate)).astype(BF16)
        return carry

    lax.fori_loop(0, T_ALL // tile, fin, 0)


def _gla(gq, gk, gv, gg, gz, wa, ba, gnw):
    nb = gq.shape[0]
    seq = lambda w: pl.BlockSpec((1, T_ALL, w), lambda b: (b, 0, 0))
    full = lambda a: pl.BlockSpec(a.shape, lambda b: (0,) * a.ndim)
    return pl.pallas_call(
        _gla_kernel,
        grid=(nb,),
        in_specs=[seq(GLA_KP), seq(GLA_KP), seq(D_GLA_V), seq(D_GLA_V), seq(Z_PAD),
                  full(wa), full(ba), full(gnw)],
        out_specs=seq(D_GLA_V),
        out_shape=jax.ShapeDtypeStruct((nb, T_ALL, D_GLA_V), BF16),
        scratch_shapes=[pltpu.VMEM((2, D_GLA_V, GLA_KP), F32),
                        pltpu.VMEM((T_ALL, D_GLA_V), F32),
                        pltpu.VMEM((T_ALL, D_GLA_V), F32)],
        compiler_params=pltpu.CompilerParams(
            dimension_semantics=("arbitrary",), vmem_limit_bytes=VMEM_LIMIT),
        name="gla_mix",
    )(gq, gk, gv, gg, gz, wa, ba, gnw)


def _outproj_kernel(h_ref, mf_ref, mn_ref, mg_ref, w_ref, mod_ref, nf_ref, o_ref, *, tile, final):
    b = pl.program_id(0)
    j = pl.program_id(1)
    acc = (_dot(mf_ref[0], w_ref[:D_FOUR, :])
           + _dot(mn_ref[0], w_ref[D_FOUR:D_FOUR + D_NA, :])
           + _dot(mg_ref[0], w_ref[D_FOUR + D_NA:, :]))
    gate = _mod_rows(mod_ref, b, j * tile, tile, 2 * D_MODEL)
    x = h_ref[0] + gate * acc
    if final:
        ms = jnp.mean(x * x, axis=-1, keepdims=True)
        x = x * lax.rsqrt(ms + EPS) * nf_ref[...]
    o_ref[0] = x


def _outproj(h, mf, mn, mg, w_out, mod_l, nf, final):
    nb = h.shape[0]
    tile = 512 if final else TOK_TILE
    rows = SEQ if final else T_ALL
    tok = lambda width: pl.BlockSpec((1, tile, width), lambda b, j: (b, j, 0))
    full = lambda shape: pl.BlockSpec(shape, lambda b, j: (0,) * len(shape))
    return pl.pallas_call(
        functools.partial(_outproj_kernel, tile=tile, final=final),
        grid=(nb, rows // tile),
        in_specs=[tok(D_MODEL), tok(D_FOUR), tok(D_NA), tok(D_GLA_V),
                  full((D_MODEL, D_MODEL)), full((16, 3 * D_MODEL)), full((1, D_MODEL))],
        out_specs=tok(D_MODEL),
        out_shape=jax.ShapeDtypeStruct((nb, rows, D_MODEL), F32),
        compiler_params=pltpu.CompilerParams(
            dimension_semantics=("arbitrary", "arbitrary"), vmem_limit_bytes=VMEM_LIMIT),
        name="outproj_final" if final else "outproj",
    )(h, mf, mn, mg, w_out, mod_l, nf)


def _qk_lanes(a):
    n_freq = GLA_DK // 4
    blocks = []
    for part in range(2):
        for h in range(GLA_HEADS):
            for half in range(2):
                c0 = h * GLA_DK + half * 2 * n_freq + part * n_freq
                blocks.append(a[..., c0:c0 + n_freq])
        blocks.append(jnp.zeros(a.shape[:-1] + (LANE - GLA_HEADS * 2 * n_freq,), a.dtype))
    return jnp.concatenate(blocks, axis=-1)


def _permute_w_in(w_in):
    sizes = (D_FOUR, D_FOUR, D_NA, D_NA, D_NA, D_NA, D_GLA_K, D_GLA_K, D_GLA_V, D_GLA_V,
             GLA_RANK, GLA_RANK)
    offs = np.concatenate([[0], np.cumsum(sizes)])
    part = lambda i: w_in[:, :, offs[i]:offs[i + 1]].astype(BF16)
    z_pad = jnp.zeros(w_in.shape[:2] + (Z_PAD - 2 * GLA_RANK,), BF16)
    cols = [part(0), part(1), (w_in[:, :, offs[2]:offs[3]] * (NA_HEAD_DIM ** -0.5)).astype(BF16),
            part(3), part(4), part(5), _qk_lanes(part(6)), _qk_lanes(part(7)), part(8), part(9),
            part(10), part(11), z_pad]
    return jnp.concatenate(cols, axis=-1)


def _decay_weights(w_f, b_f, w_b, b_b):
    zero = jnp.zeros_like(_qk_lanes(w_f))
    top = jnp.concatenate([_qk_lanes(w_f), zero], axis=-1)
    bot = jnp.concatenate([zero, _qk_lanes(w_b)], axis=-1)
    pad = jnp.zeros((w_f.shape[0], Z_PAD - 2 * GLA_RANK, 2 * GLA_KP), w_f.dtype)
    wa = jnp.concatenate([top, bot, pad], axis=1).astype(BF16)
    ba = jnp.concatenate([_qk_lanes(b_f), _qk_lanes(b_b)], axis=-1)[:, None, :]
    return wa, ba


def kernel(x, c, ctx, c_ctx, w_ada, b_ada, norm_w, w_in, w_four, rpb, w_alpha_fwd, b_alpha_fwd,
           w_alpha_bwd, b_alpha_bwd, gla_norm_w, w_out, norm_f):
    nb = x.shape[0]
    depth = w_in.shape[0]
    assert x.shape == (nb, SEQ, D_MODEL) and ctx.shape == (nb, CTX_LEN, D_MODEL) and nb <= 8

    w_in_p = _permute_w_in(w_in)
    w_out_b = w_out.astype(BF16)
    w_four_b = w_four.astype(BF16)
    wa, ba = _decay_weights(w_alpha_fwd, b_alpha_fwd, w_alpha_bwd, b_alpha_bwd)
    gnw = jnp.tile(gla_norm_w, (1, GLA_HEADS)).reshape(depth, 1, D_GLA_V)
    n_off = 2 * NA_KW - 1
    rpb_a = jnp.pad(rpb, ((0, 0), (0, 0), (0, 1), (0, LANE - n_off)))
    rpb_b = jnp.pad(rpb, ((0, 0), (0, 0), (0, 1), (GRID_W, LANE - GRID_W - n_off)))

    cos_t, sin_t = (jnp.asarray(t) for t in _rope_tables())
    c64, s64 = _dft_tables(FOUR_DIM)
    cl, sl = _dft_tables(SEQ)
    ccx, scx = _dft_tables(CTX_LEN)
    tabs = [jnp.asarray(t, dtype=F32).astype(BF16) for t in
            (_block_diag(c64, FOUR_GROUPS), _block_diag(s64, FOUR_GROUPS), cl, sl, ccx, scx)]

    cc = jnp.zeros((16, D_MODEL), F32).at[:nb].set(c).at[8].set(c_ctx)
    mod = _modulation(cc, w_ada, b_ada)

    h = jnp.concatenate([x, ctx], axis=1)
    for l in range(depth):
        final = l == depth - 1
        (p_four, nq, nk, nv, ng, gq, gk, gv, gg, gz) = _inproj(
            h, mod[l], norm_w[l].reshape(1, D_MODEL), w_in_p[l], cos_t, sin_t)
        mf = _fourier(p_four, tabs, w_four_b[l])
        mn = _na(nq, nk, nv, ng, rpb_a[l], rpb_b[l])
        mg = _gla(gq, gk, gv, gg, gz, wa[l], ba[l], gnw[l])
        h = _outproj(h, mf, mn, mg, w_out_b[l], mod[l], norm_f.reshape(1, D_MODEL), final)
    return h
```

```python
import functools

import numpy as np
import jax
import jax.numpy as jnp
from jax import lax
from jax.experimental import pallas as pl
from jax.experimental.pallas import tpu as pltpu

F32 = jnp.float32
BF16 = jnp.bfloat16

D_MODEL = 1024
SEQ = 2048
CTX_LEN = 256
T_ALL = SEQ + CTX_LEN
GRID_W = 64
GRID_H = SEQ // GRID_W
EPS = 1e-6

FOUR_GROUPS, FOUR_DIM = 4, 64
D_FOUR = FOUR_GROUPS * FOUR_DIM
NA_HEADS, NA_HEAD_DIM = 6, 64
D_NA = NA_HEADS * NA_HEAD_DIM
NA_KH, NA_KW = 8, 16
GLA_HEADS, GLA_DK, GLA_DV = 4, 48, 96
D_GLA_K = GLA_HEADS * GLA_DK
D_GLA_V = GLA_HEADS * GLA_DV
GLA_RANK = 16
GLA_TAU = 16.0
GLA_CHUNK = 64
ROPE_BASE = 10000.0

LANE = 128
GLA_KP = 2 * LANE
Z_PAD = LANE
N_CHUNKS_X = SEQ // GLA_CHUNK
N_CHUNKS_C = CTX_LEN // GLA_CHUNK
GLA_TILE = 256

NA_QROWS = 4
NA_KROWS = NA_QROWS + NA_KH - 1
NA_QB = NA_QROWS * GRID_W
NA_KB = NA_KROWS * GRID_W
NA_NBLK = GRID_H // NA_QROWS

TOK_TILE = 768
VMEM_LIMIT = 56 * 1024 * 1024

COL_FOUR = 0
COL_NA = COL_FOUR + 2 * D_FOUR
COL_GQ = COL_NA + 4 * D_NA
COL_GK = COL_GQ + GLA_KP
COL_GV = COL_GK + GLA_KP
COL_GG = COL_GV + D_GLA_V
COL_Z = COL_GG + D_GLA_V
N_COLS = COL_Z + Z_PAD


def _dot(a, b):
    return jnp.dot(a, b, preferred_element_type=F32)


def _dot_nt(a, b):
    return lax.dot_general(a, b, (((1,), (1,)), ((), ())), preferred_element_type=F32)


def _silu(x):
    return x / (1.0 + jnp.exp(-x))


def _split_bf16(x):
    hi = x.astype(BF16)
    lo = (x - hi.astype(F32)).astype(BF16)
    return hi, lo


def _rope_tables():
    n_freq = GLA_DK // 4
    inv = ROPE_BASE ** (-np.arange(n_freq, dtype=np.float64) / n_freq)
    pos = np.arange(SEQ)
    prow, pcol = pos // GRID_W, pos % GRID_W
    cos = np.ones((T_ALL, LANE), np.float64)
    sin = np.zeros((T_ALL, LANE), np.float64)
    for h in range(GLA_HEADS):
        for half, p in enumerate((prow, pcol)):
            ang = p[:, None] * inv[None, :]
            lo = h * 2 * n_freq + half * n_freq
            cos[:SEQ, lo:lo + n_freq] = np.cos(ang)
            sin[:SEQ, lo:lo + n_freq] = np.sin(ang)
    return cos.astype(np.float32), sin.astype(np.float32)


def _dft_tables(n):
    idx = (np.arange(n)[:, None] * np.arange(n)[None, :]) % n
    ang = 2.0 * np.pi * idx / n
    return np.cos(ang) / np.sqrt(n), np.sin(ang) / np.sqrt(n)


def _block_diag(m, reps):
    n = m.shape[0]
    out = np.zeros((n * reps, n * reps), m.dtype)
    for r in range(reps):
        out[r * n:(r + 1) * n, r * n:(r + 1) * n] = m
    return out


def _na_row_offsets():
    out = []
    for r0 in (0, NA_QROWS, GRID_H - NA_QROWS):
        k0 = int(np.clip(r0 - NA_KH // 2, 0, GRID_H - NA_KROWS))
        per_type = []
        for a in range(NA_QROWS):
            r = r0 + a
            rs = int(np.clip(r - NA_KH // 2, 0, GRID_H - NA_KH))
            per_type.append([(k0 + j) - r + NA_KH - 1 if rs <= k0 + j < rs + NA_KH else None
                             for j in range(NA_KROWS)])
        out.append(per_type)
    return out


def _mod_kernel(c_ref, w_ref, b_ref, o_ref):
    c = c_ref[...]
    s_hi, s_lo = _split_bf16(_silu(c))
    w_hi, w_lo = _split_bf16(w_ref[0])
    acc = _dot(s_hi, w_hi) + _dot(s_lo, w_hi) + _dot(s_hi, w_lo)
    o_ref[0] = acc + b_ref[0]


def _modulation(cc, w_ada, b_ada):
    depth = w_ada.shape[0]
    n_mod = w_ada.shape[2]
    tile = D_MODEL
    return pl.pallas_call(
        _mod_kernel,
        grid=(depth, n_mod // tile),
        in_specs=[
            pl.BlockSpec((16, D_MODEL), lambda l, j: (0, 0)),
            pl.BlockSpec((1, D_MODEL, tile), lambda l, j: (l, 0, j)),
            pl.BlockSpec((1, 1, tile), lambda l, j: (l, 0, j)),
        ],
        out_specs=pl.BlockSpec((1, 16, tile), lambda l, j: (l, 0, j)),
        out_shape=jax.ShapeDtypeStruct((depth, 16, n_mod), F32),
        compiler_params=pltpu.CompilerParams(
            dimension_semantics=("arbitrary", "arbitrary"), vmem_limit_bytes=VMEM_LIMIT),
        name="adaln_mod",
    )(cc, w_ada, b_ada.reshape(depth, 1, n_mod))


def _mod_rows(mod_ref, b, row0, n_rows, col0):
    rows = row0 + lax.broadcasted_iota(jnp.int32, (n_rows, 1), 0)
    m_x = mod_ref[pl.ds(b, 1), col0:col0 + D_MODEL]
    m_c = mod_ref[8:9, col0:col0 + D_MODEL]
    return jnp.where(rows >= SEQ, m_c, m_x)


def _inproj_kernel(h_ref, mod_ref, nw_ref, w_ref, cos_ref, sin_ref,
                   four_ref, nq_ref, nk_ref, nv_ref, ng_ref,
                   gq_ref, gk_ref, gv_ref, gg_ref, gz_ref, hx_scr):
    b = pl.program_id(0)
    j = pl.program_id(1)
    x = h_ref[0]
    ms = jnp.mean(x * x, axis=-1, keepdims=True)
    y = x * lax.rsqrt(ms + EPS) * nw_ref[...]
    shift = _mod_rows(mod_ref, b, j * TOK_TILE, TOK_TILE, 0)
    scale = _mod_rows(mod_ref, b, j * TOK_TILE, TOK_TILE, D_MODEL)
    hx_scr[...] = (y * (1.0 + scale) + shift).astype(BF16)

    def proj(c0, width):
        return _dot(hx_scr[...], w_ref[:, c0:c0 + width])

    four_ref[0] = proj(COL_FOUR, 2 * D_FOUR).astype(BF16)
    nq_ref[0] = proj(COL_NA, D_NA).astype(BF16)
    nk_ref[0] = proj(COL_NA + D_NA, D_NA).astype(BF16)
    nv_ref[0] = proj(COL_NA + 2 * D_NA, D_NA).astype(BF16)
    ng_ref[0] = proj(COL_NA + 3 * D_NA, D_NA).astype(BF16)
    gv_ref[0] = proj(COL_GV, D_GLA_V).astype(BF16)
    gg_ref[0] = proj(COL_GG, D_GLA_V).astype(BF16)
    gz_ref[0] = proj(COL_Z, Z_PAD)

    cos = cos_ref[...]
    sin = sin_ref[...]

    def rope(c0, out_ref, scale_out):
        u = proj(c0, GLA_KP)
        u1, u2 = u[:, :LANE], u[:, LANE:]
        out_ref[0, :, :LANE] = (u1 * cos - u2 * sin) * scale_out
        out_ref[0, :, LANE:] = (u1 * sin + u2 * cos) * scale_out

    rope(COL_GQ, gq_ref, GLA_DK ** -0.5)
    rope(COL_GK, gk_ref, 1.0)


def _inproj(h, mod_l, nw, w_in_p, cos_t, sin_t):
    nb = h.shape[0]
    nt = T_ALL // TOK_TILE
    tok = lambda width: pl.BlockSpec((1, TOK_TILE, width), lambda b, j: (b, j, 0))
    full = lambda shape: pl.BlockSpec(shape, lambda b, j: (0,) * len(shape))
    out_widths = [(2 * D_FOUR, BF16), (D_NA, BF16), (D_NA, BF16), (D_NA, BF16), (D_NA, BF16),
                  (GLA_KP, F32), (GLA_KP, F32), (D_GLA_V, BF16), (D_GLA_V, BF16), (Z_PAD, F32)]
    return pl.pallas_call(
        _inproj_kernel,
        grid=(nb, nt),
        in_specs=[
            tok(D_MODEL),
            full((16, 3 * D_MODEL)),
            full((1, D_MODEL)),
            full((D_MODEL, N_COLS)),
            pl.BlockSpec((TOK_TILE, LANE), lambda b, j: (j, 0)),
            pl.BlockSpec((TOK_TILE, LANE), lambda b, j: (j, 0)),
        ],
        out_specs=[tok(w) for w, _ in out_widths],
        out_shape=[jax.ShapeDtypeStruct((nb, T_ALL, w), dt) for w, dt in out_widths],
        scratch_shapes=[pltpu.VMEM((TOK_TILE, D_MODEL), BF16)],
        compiler_params=pltpu.CompilerParams(
            dimension_semantics=("arbitrary", "arbitrary"), vmem_limit_bytes=VMEM_LIMIT),
        name="inproj",
    )(h, mod_l, nw, w_in_p, cos_t, sin_t)


def _fourier_kernel(p_ref, cbd_ref, sbd_ref, cl_ref, sl_ref, cc_ref, sc_ref, wf_ref, o_ref,
                    uc_scr, us_scr):
    u = p_ref[0, :, :D_FOUR]
    uc_scr[...] = _dot(u, cbd_ref[...]).astype(BF16)
    us_scr[...] = _dot(u, sbd_ref[...]).astype(BF16)

    def finish(f, r0, n):
        four = _dot(f.astype(BF16), wf_ref[...])
        gate = p_ref[0, pl.ds(r0, n), D_FOUR:].astype(F32)
        o_ref[0, pl.ds(r0, n), :] = (four * _silu(gate)).astype(BF16)

    tile = 256

    def body(i, carry):
        r0 = pl.multiple_of(i * tile, tile)
        f = (_dot(cl_ref[pl.ds(r0, tile), :], uc_scr[:SEQ, :])
             - _dot(sl_ref[pl.ds(r0, tile), :], us_scr[:SEQ, :]))
        finish(f, r0, tile)
        return carry

    lax.fori_loop(0, SEQ // tile, body, 0)
    f_c = _dot(cc_ref[...], uc_scr[SEQ:, :]) - _dot(sc_ref[...], us_scr[SEQ:, :])
    finish(f_c, SEQ, CTX_LEN)


def _fourier(p_four, tabs, wf):
    nb = p_four.shape[0]
    full = lambda a: pl.BlockSpec(a.shape, lambda b: (0,) * a.ndim)
    return pl.pallas_call(
        _fourier_kernel,
        grid=(nb,),
        in_specs=[pl.BlockSpec((1, T_ALL, 2 * D_FOUR), lambda b: (b, 0, 0))]
        + [full(t) for t in tabs] + [full(wf)],
        out_specs=pl.BlockSpec((1, T_ALL, D_FOUR), lambda b: (b, 0, 0)),
        out_shape=jax.ShapeDtypeStruct((nb, T_ALL, D_FOUR), BF16),
        scratch_shapes=[pltpu.VMEM((T_ALL, D_FOUR), BF16), pltpu.VMEM((T_ALL, D_FOUR), BF16)],
        compiler_params=pltpu.CompilerParams(
            dimension_semantics=("arbitrary",), vmem_limit_bytes=VMEM_LIMIT),
        name="fourier_mix",
    )(p_four, *tabs, wf)


def _na_build_bias(ra_ref, rb_ref, tbl_ref):
    w = GRID_W
    cq = lax.broadcasted_iota(jnp.int32, (w, 2 * w), 0)
    ln = lax.broadcasted_iota(jnp.int32, (w, 2 * w), 1)
    ck = ln % w
    cs = jnp.clip(cq - NA_KW // 2, 0, w - NA_KW)
    col_ok = (ck >= cs) & (ck < cs + NA_KW)
    left = ln < w
    neg = jnp.full((w, 2 * w), -jnp.inf, F32)
    offs = _na_row_offsets()
    cache = {}
    for typ in range(3):
        for a in range(NA_QROWS):
            for jp in range((NA_KROWS + 1) // 2):
                d0 = offs[typ][a][2 * jp]
                d1 = offs[typ][a][2 * jp + 1] if 2 * jp + 1 < NA_KROWS else None
                width = min(2 * w, NA_KB - jp * 2 * w)
                for h in range(NA_HEADS):
                    key = (h, d0, d1)
                    if key not in cache:
                        if d0 is None and d1 is None:
                            cache[key] = neg
                        else:
                            ok = col_ok
                            if d0 is None:
                                row = rb_ref[h, d1:d1 + 1, :]
                                ok = ok & jnp.logical_not(left)
                            elif d1 is None:
                                row = ra_ref[h, d0:d0 + 1, :]
                                ok = ok & left
                            else:
                                row = ra_ref[h, d0:d0 + 1, :] + rb_ref[h, d1:d1 + 1, :]
                            skew = pltpu.roll(jnp.broadcast_to(row, (w, 2 * w)), 2 * w - (NA_KW - 1), 1,
                                              stride=1, stride_axis=0)
                            cache[key] = jnp.where(ok, skew, neg)
                    tbl_ref[h * 3 + typ, a * w:(a + 1) * w, jp * 2 * w:jp * 2 * w + width] = (
                        cache[key][:, :width])


def _na_kernel(q_ref, k_ref, v_ref, g_ref, ra_ref, rb_ref, o_ref, bias_ref):
    j = pl.program_id(1)
    lane = lax.broadcasted_iota(jnp.int32, (1, LANE), 1)
    first_head = lane < NA_HEAD_DIM

    @pl.when((pl.program_id(0) == 0) & (j == 0))
    def _():
        _na_build_bias(ra_ref, rb_ref, bias_ref)

    def attend(local):
        if local:
            k0 = pl.multiple_of(jnp.clip(j * NA_QROWS - NA_KH // 2, 0, GRID_H - NA_KROWS) * GRID_W,
                                GRID_W)
            typ = jnp.where(j == 0, 0, jnp.where(j == NA_NBLK - 1, 2, 1))
        for p in range(NA_HEADS // 2):
            cols = slice(p * LANE, (p + 1) * LANE)
            qp = q_ref[0, :, cols]
            kc = k_ref[0, SEQ:, cols]
            vc = v_ref[0, SEQ:, cols]
            if local:
                kl = k_ref[0, pl.ds(k0, NA_KB), cols]
                vl = v_ref[0, pl.ds(k0, NA_KB), cols]
            acc = None
            for hh in range(2):
                keep = first_head if hh == 0 else jnp.logical_not(first_head)
                qm = jnp.where(keep, qp, jnp.zeros_like(qp))
                s_c = _dot_nt(qm, kc)
                m = jnp.max(s_c, axis=-1, keepdims=True)
                if local:
                    s_l = _dot_nt(qm, kl) + bias_ref[(2 * p + hh) * 3 + typ]
                    m = jnp.maximum(m, jnp.max(s_l, axis=-1, keepdims=True))
                e_c = jnp.exp(s_c - m)
                den = jnp.sum(e_c, axis=-1, keepdims=True)
                o = _dot(e_c.astype(BF16), vc)
                if local:
                    e_l = jnp.exp(s_l - m)
                    den = den + jnp.sum(e_l, axis=-1, keepdims=True)
                    o = o + _dot(e_l.astype(BF16), vl)
                o = o / den
                acc = o if hh == 0 else jnp.where(first_head, acc, o)
            gate = g_ref[0, :, cols].astype(F32)
            o_ref[0, :, cols] = (acc * _silu(gate)).astype(BF16)

    @pl.when(j < NA_NBLK)
    def _():
        attend(True)

    @pl.when(j == NA_NBLK)
    def _():
        attend(False)


def _na(nq, nk, nv, ng, rpb_a, rpb_b):
    nb = nq.shape[0]
    blk = pl.BlockSpec((1, NA_QB, D_NA), lambda b, j: (b, j, 0))
    seq = pl.BlockSpec((1, T_ALL, D_NA), lambda b, j: (b, 0, 0))
    tab = pl.BlockSpec(rpb_a.shape, lambda b, j: (0, 0, 0))
    return pl.pallas_call(
        _na_kernel,
        grid=(nb, NA_NBLK + 1),
        in_specs=[blk, seq, seq, blk, tab, tab],
        out_specs=blk,
        out_shape=jax.ShapeDtypeStruct((nb, T_ALL, D_NA), BF16),
        scratch_shapes=[pltpu.VMEM((NA_HEADS * 3, NA_QB, NA_KB), F32)],
        compiler_params=pltpu.CompilerParams(
            dimension_semantics=("arbitrary", "arbitrary"), vmem_limit_bytes=VMEM_LIMIT),
        name="nbr_attention",
    )(nq, nk, nv, ng, rpb_a, rpb_b)


def _gla_kernel(q_ref, k_ref, v_ref, g_ref, z_ref, wa_ref, ba_ref, nw_ref, o_ref,
                st_scr, o_scr, qd_scr, ke_scr, vt_scr, dec_scr):
    C = GLA_CHUNK
    TR = GLA_TILE
    n_sub = TR // C
    kl = lax.broadcasted_iota(jnp.int32, (1, GLA_KP), 1) % LANE
    k_head = jnp.where(kl < GLA_HEADS * (GLA_DK // 2), kl // (GLA_DK // 2), GLA_HEADS)
    v_head = lax.broadcasted_iota(jnp.int32, (1, D_GLA_V), 1) // GLA_DV
    st_mask = (lax.broadcasted_iota(jnp.int32, (D_GLA_V, 1), 0) // GLA_DV) == k_head
    ti4 = lax.broadcasted_iota(jnp.int32, (C, GLA_HEADS * C), 0)
    si4 = lax.broadcasted_iota(jnp.int32, (C, GLA_HEADS * C), 1) % C
    tr = lax.broadcasted_iota(jnp.int32, (TR, TR), 0)
    sr = lax.broadcasted_iota(jnp.int32, (TR, TR), 1)
    same_chunk = (tr // C) == (sr // C)
    tri = [jnp.where(same_chunk & (sr <= tr), 1.0, 0.0).astype(BF16),
           jnp.where(same_chunk & (sr >= tr), 1.0, 0.0).astype(BF16)]

    def tile_body(i, carry):
        r0 = pl.multiple_of(i * TR, TR)
        rows = pl.ds(r0, TR)
        z = z_ref[0, rows, :].astype(BF16)
        g = _dot(z, wa_ref[...]) + ba_ref[...]
        la = (jnp.minimum(g, 0.0) - jnp.log1p(jnp.exp(-jnp.abs(g)))) * (1.0 / GLA_TAU)
        q = q_ref[0, rows, :]
        k = k_ref[0, rows, :]
        v = v_ref[0, rows, :]
        v_bd = []
        for c in range(n_sub):
            v_c = v[c * C:(c + 1) * C]
            v_bd.append(jnp.concatenate(
                [jnp.where(v_head == h, v_c, jnp.zeros_like(v_c)) for h in range(GLA_HEADS)], axis=0))
            vt_scr[i * n_sub + c] = v_c.astype(F32).T.astype(BF16)
        scores = []
        for d in range(2):
            fwd = d == 0
            la_hi, la_lo = _split_bf16(la[:, d * GLA_KP:(d + 1) * GLA_KP])
            bcum = _dot(tri[d], la_hi) + _dot(tri[d], la_lo)
            ends = [bcum[(c + 1) * C - 1:(c + 1) * C] if fwd else bcum[c * C:c * C + 1]
                    for c in range(n_sub)]
            b_end = jnp.concatenate([jnp.broadcast_to(e, (C, GLA_KP)) for e in ends], axis=0)
            q_dec = (q * jnp.exp(bcum)).astype(BF16)
            k_dec = k * jnp.exp(-bcum)
            qd_scr[d, rows, :] = q_dec
            ke_scr[d, rows, :] = (k * jnp.exp(b_end - bcum)).astype(BF16)
            for c in range(n_sub):
                dec_scr[d, pl.ds(i * n_sub + c, 1), :] = jnp.exp(ends[c])
                k_c = k_dec[c * C:(c + 1) * C]
                k_bd = jnp.concatenate(
                    [jnp.where(k_head == h, k_c, 0.0) for h in range(GLA_HEADS)], axis=0).astype(BF16)
                scores.append(_dot_nt(q_dec[c * C:(c + 1) * C], k_bd))
        for d in range(2):
            causal = (si4 <= ti4) if d == 0 else (si4 >= ti4)
            for c in range(n_sub):
                a = jnp.where(causal, scores[d * n_sub + c], 0.0).astype(BF16)
                o_scr[d, pl.ds(r0 + c * C, C), :] = _dot(a, v_bd[c])
        return carry

    lax.fori_loop(0, T_ALL // TR, tile_body, 0)

    st_scr[...] = jnp.zeros_like(st_scr)
    n_all = N_CHUNKS_C + N_CHUNKS_X

    def step(i, carry):
        in_ctx = i < N_CHUNKS_C
        idx_f = jnp.where(in_ctx, N_CHUNKS_X + i, i - N_CHUNKS_C)
        idx_b = jnp.where(in_ctx, N_CHUNKS_X + N_CHUNKS_C - 1 - i, n_all - 1 - i)
        for d, idx in ((0, idx_f), (1, idx_b)):
            off = pl.ds(pl.multiple_of(idx * C, C), C)
            st = st_scr[d]
            o_scr[d, off, :] += _dot_nt(qd_scr[d, off, :], st.astype(BF16))
            half = D_GLA_V // 2
            ke = ke_scr[d, off, :]
            upd = jnp.concatenate([_dot(vt_scr[idx, :half, :], ke), _dot(vt_scr[idx, half:, :], ke)], axis=0)
            st_scr[d] = st * dec_scr[d, pl.ds(idx, 1), :] + jnp.where(st_mask, upd, 0.0)
        return carry

    lax.fori_loop(0, n_all, step, 0, unroll=4)

    head_ones = jnp.where(
        (lax.broadcasted_iota(jnp.int32, (D_GLA_V, 1), 0) // GLA_DV) == v_head, 1.0, 0.0).astype(BF16)
    tile = 256

    def fin(i, carry):
        r0 = pl.multiple_of(i * tile, tile)
        o = o_scr[0, pl.ds(r0, tile), :] + o_scr[1, pl.ds(r0, tile), :]
        sq_hi, sq_lo = _split_bf16(o * o)
        ms = (_dot(sq_hi, head_ones) + _dot(sq_lo, head_ones)) * (1.0 / GLA_DV)
        y = o * lax.rsqrt(ms + EPS) * nw_ref[...]
        gate = g_ref[0, pl.ds(r0, tile), :].astype(F32)
        o_ref[0, pl.ds(r0, tile), :] = (y * _silu(gate)).astype(BF16)
        return carry

    lax.fori_loop(0, T_ALL // tile, fin, 0)


def _gla(gq, gk, gv, gg, gz, wa, ba, gnw):
    nb = gq.shape[0]
    seq = lambda w: pl.BlockSpec((1, T_ALL, w), lambda b: (b, 0, 0))
    full = lambda a: pl.BlockSpec(a.shape, lambda b: (0,) * a.ndim)
    return pl.pallas_call(
        _gla_kernel,
        grid=(nb,),
        in_specs=[seq(GLA_KP), seq(GLA_KP), seq(D_GLA_V), seq(D_GLA_V), seq(Z_PAD),
                  full(wa), full(ba), full(gnw)],
        out_specs=seq(D_GLA_V),
        out_shape=jax.ShapeDtypeStruct((nb, T_ALL, D_GLA_V), BF16),
        scratch_shapes=[pltpu.VMEM((2, D_GLA_V, GLA_KP), F32),
                        pltpu.VMEM((2, T_ALL, D_GLA_V), F32),
                        pltpu.VMEM((2, T_ALL, GLA_KP), BF16),
                        pltpu.VMEM((2, T_ALL, GLA_KP), BF16),
                        pltpu.VMEM((T_ALL // GLA_CHUNK, D_GLA_V, GLA_CHUNK), BF16),
                        pltpu.VMEM((2, T_ALL // GLA_CHUNK, GLA_KP), F32)],

        compiler_params=pltpu.CompilerParams(
            dimension_semantics=("arbitrary",), vmem_limit_bytes=VMEM_LIMIT),
        name="gla_mix",
    )(gq, gk, gv, gg, gz, wa, ba, gnw)


def _outproj_kernel(h_ref, mf_ref, mn_ref, mg_ref, w_ref, mod_ref, nf_ref, o_ref, *, tile, final):
    b = pl.program_id(0)
    j = pl.program_id(1)
    acc = (_dot(mf_ref[0], w_ref[:D_FOUR, :])
           + _dot(mn_ref[0], w_ref[D_FOUR:D_FOUR + D_NA, :])
           + _dot(mg_ref[0], w_ref[D_FOUR + D_NA:, :]))
    gate = _mod_rows(mod_ref, b, j * tile, tile, 2 * D_MODEL)
    x = h_ref[0] + gate * acc
    if final:
        ms = jnp.mean(x * x, axis=-1, keepdims=True)
        x = x * lax.rsqrt(ms + EPS) * nf_ref[...]
    o_ref[0] = x


def _outproj(h, mf, mn, mg, w_out, mod_l, nf, final):
    nb = h.shape[0]
    tile = 512 if final else TOK_TILE
    rows = SEQ if final else T_ALL
    tok = lambda width: pl.BlockSpec((1, tile, width), lambda b, j: (b, j, 0))
    full = lambda shape: pl.BlockSpec(shape, lambda b, j: (0,) * len(shape))
    return pl.pallas_call(
        functools.partial(_outproj_kernel, tile=tile, final=final),
        grid=(nb, rows // tile),
        in_specs=[tok(D_MODEL), tok(D_FOUR), tok(D_NA), tok(D_GLA_V),
                  full((D_MODEL, D_MODEL)), full((16, 3 * D_MODEL)), full((1, D_MODEL))],
        out_specs=tok(D_MODEL),
        out_shape=jax.ShapeDtypeStruct((nb, rows, D_MODEL), F32),
        compiler_params=pltpu.CompilerParams(
            dimension_semantics=("arbitrary", "arbitrary"), vmem_limit_bytes=VMEM_LIMIT),
        name="outproj_final" if final else "outproj",
    )(h, mf, mn, mg, w_out, mod_l, nf)


def _qk_lanes(a):
    n_freq = GLA_DK // 4
    blocks = []
    for part in range(2):
        for h in range(GLA_HEADS):
            for half in range(2):
                c0 = h * GLA_DK + half * 2 * n_freq + part * n_freq
                blocks.append(a[..., c0:c0 + n_freq])
        blocks.append(jnp.zeros(a.shape[:-1] + (LANE - GLA_HEADS * 2 * n_freq,), a.dtype))
    return jnp.concatenate(blocks, axis=-1)


def _permute_w_in(w_in):
    sizes = (D_FOUR, D_FOUR, D_NA, D_NA, D_NA, D_NA, D_GLA_K, D_GLA_K, D_GLA_V, D_GLA_V,
             GLA_RANK, GLA_RANK)
    offs = np.concatenate([[0], np.cumsum(sizes)])
    part = lambda i: w_in[:, :, offs[i]:offs[i + 1]].astype(BF16)
    z_pad = jnp.zeros(w_in.shape[:2] + (Z_PAD - 2 * GLA_RANK,), BF16)
    cols = [part(0), part(1), (w_in[:, :, offs[2]:offs[3]] * (NA_HEAD_DIM ** -0.5)).astype(BF16),
            part(3), part(4), part(5), _qk_lanes(part(6)), _qk_lanes(part(7)), part(8), part(9),
            part(10), part(11), z_pad]
    return jnp.concatenate(cols, axis=-1)


def _decay_weights(w_f, b_f, w_b, b_b):
    zero = jnp.zeros_like(_qk_lanes(w_f))
    top = jnp.concatenate([_qk_lanes(w_f), zero], axis=-1)
    bot = jnp.concatenate([zero, _qk_lanes(w_b)], axis=-1)
    pad = jnp.zeros((w_f.shape[0], Z_PAD - 2 * GLA_RANK, 2 * GLA_KP), w_f.dtype)
    wa = jnp.concatenate([top, bot, pad], axis=1).astype(BF16)
    ba = jnp.concatenate([_qk_lanes(b_f), _qk_lanes(b_b)], axis=-1)[:, None, :]
    return wa, ba


def kernel(x, c, ctx, c_ctx, w_ada, b_ada, norm_w, w_in, w_four, rpb, w_alpha_fwd, b_alpha_fwd,
           w_alpha_bwd, b_alpha_bwd, gla_norm_w, w_out, norm_f):
    nb = x.shape[0]
    depth = w_in.shape[0]
    assert x.shape == (nb, SEQ, D_MODEL) and ctx.shape == (nb, CTX_LEN, D_MODEL) and nb <= 8

    w_in_p = _permute_w_in(w_in)
    w_out_b = w_out.astype(BF16)
    w_four_b = w_four.astype(BF16)
    wa, ba = _decay_weights(w_alpha_fwd, b_alpha_fwd, w_alpha_bwd, b_alpha_bwd)
    gnw = jnp.tile(gla_norm_w, (1, GLA_HEADS)).reshape(depth, 1, D_GLA_V)
    n_off = 2 * NA_KW - 1
    rpb_a = jnp.pad(rpb, ((0, 0), (0, 0), (0, 1), (0, LANE - n_off)))
    rpb_b = jnp.pad(rpb, ((0, 0), (0, 0), (0, 1), (GRID_W, LANE - GRID_W - n_off)))

    cos_t, sin_t = (jnp.asarray(t) for t in _rope_tables())
    c64, s64 = _dft_tables(FOUR_DIM)
    cl, sl = _dft_tables(SEQ)
    ccx, scx = _dft_tables(CTX_LEN)
    tabs = [jnp.asarray(t, dtype=F32).astype(BF16) for t in
            (_block_diag(c64, FOUR_GROUPS), _block_diag(s64, FOUR_GROUPS), cl, sl, ccx, scx)]

    cc = jnp.zeros((16, D_MODEL), F32).at[:nb].set(c).at[8].set(c_ctx)
    mod = _modulation(cc, w_ada, b_ada)

    h = jnp.concatenate([x, ctx], axis=1)
    for l in range(depth):
        final = l == depth - 1
        (p_four, nq, nk, nv, ng, gq, gk, gv, gg, gz) = _inproj(
            h, mod[l], norm_w[l].reshape(1, D_MODEL), w_in_p[l], cos_t, sin_t)
        mf = _fourier(p_four, tabs, w_four_b[l])
        mn = _na(nq, nk, nv, ng, rpb_a[l], rpb_b[l])
        mg = _gla(gq, gk, gv, gg, gz, wa[l], ba[l], gnw[l])
        h = _outproj(h, mf, mn, mg, w_out_b[l], mod[l], norm_f.reshape(1, D_MODEL), final)
    return h
```

```python
import functools

import numpy as np
import jax
import jax.numpy as jnp
from jax import lax
from jax.experimental import pallas as pl
from jax.experimental.pallas import tpu as pltpu

F32 = jnp.float32
BF16 = jnp.bfloat16

D_MODEL = 1024
SEQ = 2048
CTX_LEN = 256
T_ALL = SEQ + CTX_LEN
GRID_W = 64
GRID_H = SEQ // GRID_W
EPS = 1e-6

FOUR_GROUPS, FOUR_DIM = 4, 64
D_FOUR = FOUR_GROUPS * FOUR_DIM
NA_HEADS, NA_HEAD_DIM = 6, 64
D_NA = NA_HEADS * NA_HEAD_DIM
NA_KH, NA_KW = 8, 16
GLA_HEADS, GLA_DK, GLA_DV = 4, 48, 96
D_GLA_K = GLA_HEADS * GLA_DK
D_GLA_V = GLA_HEADS * GLA_DV
GLA_RANK = 16
GLA_TAU = 16.0
GLA_CHUNK = 64
ROPE_BASE = 10000.0
LOG2E = 1.4426950408889634

LANE = 128
GLA_KP = 2 * LANE
Z_PAD = LANE
N_CHUNKS_X = SEQ // GLA_CHUNK
N_CHUNKS_C = CTX_LEN // GLA_CHUNK
GLA_TILE = 256

NA_QROWS = 4
NA_KROWS = NA_QROWS + NA_KH
NA_QB = NA_QROWS * GRID_W
NA_KB = NA_KROWS * GRID_W
NA_NBLK = GRID_H // NA_QROWS

TOK_TILE = 768
VMEM_LIMIT = 56 * 1024 * 1024

COL_FOUR = 0
COL_NA = COL_FOUR + 2 * D_FOUR
COL_GQ = COL_NA + 3 * D_NA
COL_GK = COL_GQ + GLA_KP
COL_GV = COL_GK + GLA_KP
COL_GG = COL_GV + D_GLA_V
COL_Z = COL_GG + D_GLA_V
N_COLS = COL_Z + Z_PAD


def _dot(a, b):
    return jnp.dot(a, b, preferred_element_type=F32)


def _dot_nt(a, b):
    return lax.dot_general(a, b, (((1,), (1,)), ((), ())), preferred_element_type=F32)


def _silu(x):
    return x / (1.0 + jnp.exp(-x))


def _reduce_rows(x, op):
    r, n = x.shape
    part = op(x.reshape(r // 8, 8, n), axis=0)
    return op(part, axis=0, keepdims=True)


def _split_bf16(x):
    hi = x.astype(BF16)
    lo = (x - hi.astype(F32)).astype(BF16)
    return hi, lo


def _rope_tables():
    n_freq = GLA_DK // 4
    inv = ROPE_BASE ** (-np.arange(n_freq, dtype=np.float64) / n_freq)
    pos = np.arange(SEQ)
    prow, pcol = pos // GRID_W, pos % GRID_W
    cos = np.ones((T_ALL, LANE), np.float64)
    sin = np.zeros((T_ALL, LANE), np.float64)
    for h in range(GLA_HEADS):
        for half, p in enumerate((prow, pcol)):
            ang = p[:, None] * inv[None, :]
            lo = h * 2 * n_freq + half * n_freq
            cos[:SEQ, lo:lo + n_freq] = np.cos(ang)
            sin[:SEQ, lo:lo + n_freq] = np.sin(ang)
    return cos.astype(np.float32), sin.astype(np.float32)


def _dft_tables(n):
    idx = (np.arange(n)[:, None] * np.arange(n)[None, :]) % n
    ang = 2.0 * np.pi * idx / n
    return np.cos(ang) / np.sqrt(n), np.sin(ang) / np.sqrt(n)


def _block_diag(m, reps):
    n = m.shape[0]
    out = np.zeros((n * reps, n * reps), m.dtype)
    for r in range(reps):
        out[r * n:(r + 1) * n, r * n:(r + 1) * n] = m
    return out


def _na_row_offsets():
    out = []
    for r0 in (0, NA_QROWS, GRID_H - NA_QROWS):
        k0 = int(np.clip(r0 - NA_KH // 2, 0, GRID_H - NA_KROWS))
        per_type = []
        for a in range(NA_QROWS):
            r = r0 + a
            rs = int(np.clip(r - NA_KH // 2, 0, GRID_H - NA_KH))
            per_type.append([(k0 + j) - r + NA_KH - 1 if rs <= k0 + j < rs + NA_KH else None
                             for j in range(NA_KROWS)])
        out.append(per_type)
    return out


def _mod_kernel(c_ref, w_ref, b_ref, o_ref):
    c = c_ref[...]
    s_hi, s_lo = _split_bf16(_silu(c))
    w_hi, w_lo = _split_bf16(w_ref[0])
    acc = _dot(s_hi, w_hi) + _dot(s_lo, w_hi) + _dot(s_hi, w_lo)
    o_ref[0] = acc + b_ref[0]


def _modulation(cc, w_ada, b_ada):
    depth = w_ada.shape[0]
    n_mod = w_ada.shape[2]
    tile = D_MODEL
    return pl.pallas_call(
        _mod_kernel,
        grid=(depth, n_mod // tile),
        in_specs=[
            pl.BlockSpec((16, D_MODEL), lambda l, j: (0, 0)),
            pl.BlockSpec((1, D_MODEL, tile), lambda l, j: (l, 0, j)),
            pl.BlockSpec((1, 1, tile), lambda l, j: (l, 0, j)),
        ],
        out_specs=pl.BlockSpec((1, 16, tile), lambda l, j: (l, 0, j)),
        out_shape=jax.ShapeDtypeStruct((depth, 16, n_mod), F32),
        compiler_params=pltpu.CompilerParams(
            dimension_semantics=("arbitrary", "arbitrary"), vmem_limit_bytes=VMEM_LIMIT),
        name="adaln_mod",
    )(cc, w_ada, b_ada.reshape(depth, 1, n_mod))


def _mod_rows(mod_ref, b, row0, n_rows, col0):
    rows = row0 + lax.broadcasted_iota(jnp.int32, (n_rows, 1), 0)
    m_x = mod_ref[pl.ds(b, 1), col0:col0 + D_MODEL]
    m_c = mod_ref[8:9, col0:col0 + D_MODEL]
    return jnp.where(rows >= SEQ, m_c, m_x)


def _inproj_kernel(h_ref, mod_ref, nw_ref, w_ref, wvt_ref, cos_ref, sin_ref,
                   four_ref, nq_ref, nk_ref, nvt_ref, ng_ref,
                   gq_ref, gk_ref, gv_ref, gg_ref, gz_ref, hx_scr):
    b = pl.program_id(0)
    j = pl.program_id(1)
    x = h_ref[0]
    ms = jnp.mean(x * x, axis=-1, keepdims=True)
    y = x * lax.rsqrt(ms + EPS) * nw_ref[...]
    shift = _mod_rows(mod_ref, b, j * TOK_TILE, TOK_TILE, 0)
    scale = _mod_rows(mod_ref, b, j * TOK_TILE, TOK_TILE, D_MODEL)
    hx_scr[...] = (y * (1.0 + scale) + shift).astype(BF16)

    def proj(c0, width):
        return _dot(hx_scr[...], w_ref[:, c0:c0 + width])

    four_ref[0] = proj(COL_FOUR, 2 * D_FOUR).astype(BF16)
    nq_ref[0] = proj(COL_NA, D_NA).astype(BF16)
    nk_ref[0] = proj(COL_NA + D_NA, D_NA).astype(BF16)
    nvt_ref[0] = _dot_nt(wvt_ref[...], hx_scr[...]).astype(BF16)
    ng_ref[0] = proj(COL_NA + 2 * D_NA, D_NA).astype(BF16)
    gv_ref[0] = proj(COL_GV, D_GLA_V).astype(BF16)
    gg_ref[0] = proj(COL_GG, D_GLA_V).astype(BF16)
    gz_ref[0] = proj(COL_Z, Z_PAD)

    cos = cos_ref[...]
    sin = sin_ref[...]

    def rope(c0, out_ref, scale_out):
        u = proj(c0, GLA_KP)
        u1, u2 = u[:, :LANE], u[:, LANE:]
        out_ref[0, :, :LANE] = (u1 * cos - u2 * sin) * scale_out
        out_ref[0, :, LANE:] = (u1 * sin + u2 * cos) * scale_out

    rope(COL_GQ, gq_ref, GLA_DK ** -0.5)
    rope(COL_GK, gk_ref, 1.0)


def _inproj(h, mod_l, nw, w_in_p, w_nvt, cos_t, sin_t):
    nb = h.shape[0]
    nt = T_ALL // TOK_TILE
    tok = lambda width: pl.BlockSpec((1, TOK_TILE, width), lambda b, j: (b, j, 0))
    full = lambda shape: pl.BlockSpec(shape, lambda b, j: (0,) * len(shape))
    out_widths = [(2 * D_FOUR, BF16), (D_NA, BF16), (D_NA, BF16), (None, BF16), (D_NA, BF16),
                  (GLA_KP, F32), (GLA_KP, F32), (D_GLA_V, BF16), (D_GLA_V, BF16), (Z_PAD, F32)]
    vt_spec = pl.BlockSpec((1, D_NA, TOK_TILE), lambda b, j: (b, 0, j))
    vt_shape = jax.ShapeDtypeStruct((nb, D_NA, T_ALL), BF16)
    return pl.pallas_call(
        _inproj_kernel,
        grid=(nb, nt),
        in_specs=[
            tok(D_MODEL),
            full((16, 3 * D_MODEL)),
            full((1, D_MODEL)),
            full((D_MODEL, N_COLS)),
            full((D_NA, D_MODEL)),
            pl.BlockSpec((TOK_TILE, LANE), lambda b, j: (j, 0)),
            pl.BlockSpec((TOK_TILE, LANE), lambda b, j: (j, 0)),
        ],
        out_specs=[vt_spec if w is None else tok(w) for w, _ in out_widths],
        out_shape=[vt_shape if w is None else jax.ShapeDtypeStruct((nb, T_ALL, w), dt)
                   for w, dt in out_widths],
        scratch_shapes=[pltpu.VMEM((TOK_TILE, D_MODEL), BF16)],
        compiler_params=pltpu.CompilerParams(
            dimension_semantics=("arbitrary", "arbitrary"), vmem_limit_bytes=VMEM_LIMIT),
        name="inproj",
    )(h, mod_l, nw, w_in_p, w_nvt, cos_t, sin_t)


def _fourier_kernel(p_ref, cbd_ref, sbd_ref, cl_ref, sl_ref, cc_ref, sc_ref, wf_ref, o_ref,
                    uc_scr, us_scr):
    u = p_ref[0, :, :D_FOUR]
    uc_scr[...] = _dot(u, cbd_ref[...]).astype(BF16)
    us_scr[...] = _dot(u, sbd_ref[...]).astype(BF16)

    def finish(f, r0, n):
        four = _dot(f.astype(BF16), wf_ref[...])
        gate = p_ref[0, pl.ds(r0, n), D_FOUR:].astype(F32)
        o_ref[0, pl.ds(r0, n), :] = (four * _silu(gate)).astype(BF16)

    tile = 256

    def body(i, carry):
        r0 = pl.multiple_of(i * tile, tile)
        f = (_dot(cl_ref[pl.ds(r0, tile), :], uc_scr[:SEQ, :])
             - _dot(sl_ref[pl.ds(r0, tile), :], us_scr[:SEQ, :]))
        finish(f, r0, tile)
        return carry

    lax.fori_loop(0, SEQ // tile, body, 0)
    f_c = _dot(cc_ref[...], uc_scr[SEQ:, :]) - _dot(sc_ref[...], us_scr[SEQ:, :])
    finish(f_c, SEQ, CTX_LEN)


def _fourier(p_four, tabs, wf):
    nb = p_four.shape[0]
    full = lambda a: pl.BlockSpec(a.shape, lambda b: (0,) * a.ndim)
    return pl.pallas_call(
        _fourier_kernel,
        grid=(nb,),
        in_specs=[pl.BlockSpec((1, T_ALL, 2 * D_FOUR), lambda b: (b, 0, 0))]
        + [full(t) for t in tabs] + [full(wf)],
        out_specs=pl.BlockSpec((1, T_ALL, D_FOUR), lambda b: (b, 0, 0)),
        out_shape=jax.ShapeDtypeStruct((nb, T_ALL, D_FOUR), BF16),
        scratch_shapes=[pltpu.VMEM((T_ALL, D_FOUR), BF16), pltpu.VMEM((T_ALL, D_FOUR), BF16)],
        compiler_params=pltpu.CompilerParams(
            dimension_semantics=("arbitrary",), vmem_limit_bytes=VMEM_LIMIT),
        name="fourier_mix",
    )(p_four, *tabs, wf)


def _na_build_bias(ra_ref, rb_ref, tbl_ref):
    w = GRID_W
    ck = lax.broadcasted_iota(jnp.int32, (w, 2 * w), 0)
    ln = lax.broadcasted_iota(jnp.int32, (w, 2 * w), 1)
    cq = ln % w
    cs = jnp.clip(cq - NA_KW // 2, 0, w - NA_KW)
    col_ok = (ck >= cs) & (ck < cs + NA_KW)
    left = ln < w
    neg = jnp.full((w, 2 * w), -jnp.inf, F32)
    offs = _na_row_offsets()
    cache = {}
    for typ in range(3):
        for ap in range(NA_QROWS // 2):
            for jr in range(NA_KROWS):
                d0 = offs[typ][2 * ap][jr]
                d1 = offs[typ][2 * ap + 1][jr]
                for h in range(NA_HEADS):
                    key = (h, d0, d1)
                    if key not in cache:
                        if d0 is None and d1 is None:
                            cache[key] = neg
                        else:
                            ok = col_ok
                            if d0 is None:
                                row = rb_ref[h, d1:d1 + 1, :]
                                ok = ok & jnp.logical_not(left)
                            elif d1 is None:
                                row = ra_ref[h, d0:d0 + 1, :]
                                ok = ok & left
                            else:
                                row = ra_ref[h, d0:d0 + 1, :] + rb_ref[h, d1:d1 + 1, :]
                            skew = pltpu.roll(jnp.broadcast_to(row * LOG2E, (w, 2 * w)),
                                              2 * w - (NA_KW - 1), 1, stride=1, stride_axis=0)
                            cache[key] = jnp.where(ok, skew, neg)
                    tbl_ref[h * 3 + typ, jr * w:(jr + 1) * w, ap * 2 * w:(ap + 1) * 2 * w] = cache[key]


def _na_kernel(q_ref, k_ref, vt_ref, g_ref, ra_ref, rb_ref, o_ref, bias_ref):
    j = pl.program_id(1)
    lane = lax.broadcasted_iota(jnp.int32, (1, LANE), 1)
    first_head_lanes = lane < NA_HEAD_DIM
    first_head_rows = lax.broadcasted_iota(jnp.int32, (LANE, 1), 0) < NA_HEAD_DIM

    @pl.when((pl.program_id(0) == 0) & (j == 0))
    def _():
        _na_build_bias(ra_ref, rb_ref, bias_ref)

    def attend(local):
        if local:
            k0 = pl.multiple_of(jnp.clip(j * NA_QROWS - NA_KH // 2, 0, GRID_H - NA_KROWS) * GRID_W,
                                NA_QB)
            typ = jnp.where(j == 0, 0, jnp.where(j == NA_NBLK - 1, 2, 1))
        heads = range(NA_HEADS)
        cols = [slice(h // 2 * LANE, (h // 2 + 1) * LANE) for h in heads]
        s_c, s_l = [], []
        for h in heads:
            qp = q_ref[0, :, cols[h]]
            keep = first_head_lanes if h % 2 == 0 else jnp.logical_not(first_head_lanes)
            qm = jnp.where(keep, qp, jnp.zeros_like(qp))
            s_c.append(_dot_nt(k_ref[0, SEQ:, cols[h]], qm))
            if local:
                s_l.append(_dot_nt(k_ref[0, pl.ds(k0, NA_KB), cols[h]], qm) + bias_ref[h * 3 + typ])
        e_c, e_l, den = [], [], []
        for h in heads:
            m = _reduce_rows(s_c[h], jnp.max)
            if local:
                m = jnp.maximum(m, _reduce_rows(s_l[h], jnp.max))
            e = jnp.exp2(s_c[h] - m)
            d = _reduce_rows(e, jnp.sum)
            e_c.append(e.astype(BF16))
            if local:
                e = jnp.exp2(s_l[h] - m)
                d = d + _reduce_rows(e, jnp.sum)
                e_l.append(e.astype(BF16))
            den.append(d)
        for p in range(NA_HEADS // 2):
            acc = None
            for h in (2 * p, 2 * p + 1):
                o = _dot(vt_ref[0, cols[h], SEQ:], e_c[h])
                if local:
                    o = o + _dot(vt_ref[0, cols[h], pl.ds(k0, NA_KB)], e_l[h])
                o = o / den[h]
                acc = o if h % 2 == 0 else jnp.where(first_head_rows, acc, o)
            gate = g_ref[0, :, cols[2 * p]].astype(F32)
            o_ref[0, :, cols[2 * p]] = (acc.T * _silu(gate)).astype(BF16)

    @pl.when(j < NA_NBLK)
    def _():
        attend(True)

    @pl.when(j == NA_NBLK)
    def _():
        attend(False)


def _na(nq, nk, nvt, ng, rpb_a, rpb_b):
    nb = nq.shape[0]
    blk = pl.BlockSpec((1, NA_QB, D_NA), lambda b, j: (b, j, 0))
    seq = pl.BlockSpec((1, T_ALL, D_NA), lambda b, j: (b, 0, 0))
    seq_t = pl.BlockSpec((1, D_NA, T_ALL), lambda b, j: (b, 0, 0))
    tab = pl.BlockSpec(rpb_a.shape, lambda b, j: (0, 0, 0))
    return pl.pallas_call(
        _na_kernel,
        grid=(nb, NA_NBLK + 1),
        in_specs=[blk, seq, seq_t, blk, tab, tab],
        out_specs=blk,
        out_shape=jax.ShapeDtypeStruct((nb, T_ALL, D_NA), BF16),
        scratch_shapes=[pltpu.VMEM((NA_HEADS * 3, NA_KB, NA_QB), F32)],
        compiler_params=pltpu.CompilerParams(
            dimension_semantics=("arbitrary", "arbitrary"), vmem_limit_bytes=VMEM_LIMIT),
        name="nbr_attention",
    )(nq, nk, nvt, ng, rpb_a, rpb_b)


def _gla_kernel(q_ref, k_ref, v_ref, g_ref, z_ref, wa_ref, ba_ref, nw_ref, o_ref,
                st_scr, o_scr, qd_scr, ke_scr, vt_scr, dec_scr):
    C = GLA_CHUNK
    TR = GLA_TILE
    n_sub = TR // C
    kl = lax.broadcasted_iota(jnp.int32, (1, GLA_KP), 1) % LANE
    k_head = jnp.where(kl < GLA_HEADS * (GLA_DK // 2), kl // (GLA_DK // 2), GLA_HEADS)
    v_head = lax.broadcasted_iota(jnp.int32, (1, D_GLA_V), 1) // GLA_DV
    st_mask = (lax.broadcasted_iota(jnp.int32, (D_GLA_V, 1), 0) // GLA_DV) == k_head
    ti4 = lax.broadcasted_iota(jnp.int32, (C, GLA_HEADS * C), 0)
    si4 = lax.broadcasted_iota(jnp.int32, (C, GLA_HEADS * C), 1) % C
    tr = lax.broadcasted_iota(jnp.int32, (TR, TR), 0)
    sr = lax.broadcasted_iota(jnp.int32, (TR, TR), 1)
    same_chunk = (tr // C) == (sr // C)
    tri = [jnp.where(same_chunk & (sr <= tr), 1.0, 0.0).astype(BF16),
           jnp.where(same_chunk & (sr >= tr), 1.0, 0.0).astype(BF16)]

    def tile_body(i, carry):
        r0 = pl.multiple_of(i * TR, TR)
        rows = pl.ds(r0, TR)
        z = z_ref[0, rows, :].astype(BF16)
        g = _dot(z, wa_ref[...]) + ba_ref[...]
        la = (jnp.minimum(g, 0.0) - jnp.log1p(jnp.exp(-jnp.abs(g)))) * (1.0 / GLA_TAU)
        q = q_ref[0, rows, :]
        k = k_ref[0, rows, :]
        v = v_ref[0, rows, :]
        v_bd = []
        for c in range(n_sub):
            v_c = v[c * C:(c + 1) * C]
            v_bd.append(jnp.concatenate(
                [jnp.where(v_head == h, v_c, jnp.zeros_like(v_c)) for h in range(GLA_HEADS)], axis=0))
            vt_scr[i * n_sub + c] = v_c.astype(F32).T.astype(BF16)
        scores = []
        for d in range(2):
            fwd = d == 0
            la_hi, la_lo = _split_bf16(la[:, d * GLA_KP:(d + 1) * GLA_KP])
            bcum = _dot(tri[d], la_hi) + _dot(tri[d], la_lo)
            ends = [bcum[(c + 1) * C - 1:(c + 1) * C] if fwd else bcum[c * C:c * C + 1]
                    for c in range(n_sub)]
            b_end = jnp.concatenate([jnp.broadcast_to(e, (C, GLA_KP)) for e in ends], axis=0)
            q_dec = (q * jnp.exp(bcum)).astype(BF16)
            k_dec = k * jnp.exp(-bcum)
            qd_scr[d, rows, :] = q_dec
            ke_scr[d, rows, :] = (k * jnp.exp(b_end - bcum)).astype(BF16)
            for c in range(n_sub):
                dec_scr[d, pl.ds(i * n_sub + c, 1), :] = jnp.exp(ends[c])
                k_c = k_dec[c * C:(c + 1) * C]
                k_bd = jnp.concatenate(
                    [jnp.where(k_head == h, k_c, 0.0) for h in range(GLA_HEADS)], axis=0).astype(BF16)
                scores.append(_dot_nt(q_dec[c * C:(c + 1) * C], k_bd))
        for d in range(2):
            causal = (si4 <= ti4) if d == 0 else (si4 >= ti4)
            for c in range(n_sub):
                a = jnp.where(causal, scores[d * n_sub + c], 0.0).astype(BF16)
                o_scr[d, pl.ds(r0 + c * C, C), :] = _dot(a, v_bd[c])
        return carry

    lax.fori_loop(0, T_ALL // TR, tile_body, 0)

    st_scr[...] = jnp.zeros_like(st_scr)
    n_all = N_CHUNKS_C + N_CHUNKS_X

    def step(i, carry):
        in_ctx = i < N_CHUNKS_C
        idx_f = jnp.where(in_ctx, N_CHUNKS_X + i, i - N_CHUNKS_C)
        idx_b = jnp.where(in_ctx, N_CHUNKS_X + N_CHUNKS_C - 1 - i, n_all - 1 - i)
        for d, idx in ((0, idx_f), (1, idx_b)):
            off = pl.ds(pl.multiple_of(idx * C, C), C)
            st = st_scr[d]
            o_scr[d, off, :] += _dot_nt(qd_scr[d, off, :], st.astype(BF16))
            half = D_GLA_V // 2
            ke = ke_scr[d, off, :]
            upd = jnp.concatenate([_dot(vt_scr[idx, :half, :], ke), _dot(vt_scr[idx, half:, :], ke)], axis=0)
            st_scr[d] = st * dec_scr[d, pl.ds(idx, 1), :] + jnp.where(st_mask, upd, 0.0)
        return carry

    lax.fori_loop(0, n_all, step, 0, unroll=4)

    head_ones = jnp.where(
        (lax.broadcasted_iota(jnp.int32, (D_GLA_V, 1), 0) // GLA_DV) == v_head, 1.0, 0.0).astype(BF16)
    tile = 256

    def fin(i, carry):
        r0 = pl.multiple_of(i * tile, tile)
        o = o_scr[0, pl.ds(r0, tile), :] + o_scr[1, pl.ds(r0, tile), :]
        sq_hi, sq_lo = _split_bf16(o * o)
        ms = (_dot(sq_hi, head_ones) + _dot(sq_lo, head_ones)) * (1.0 / GLA_DV)
        y = o * lax.rsqrt(ms + EPS) * nw_ref[...]
        gate = g_ref[0, pl.ds(r0, tile), :].astype(F32)
        o_ref[0, pl.ds(r0, tile), :] = (y * _silu(gate)).astype(BF16)
        return carry

    lax.fori_loop(0, T_ALL // tile, fin, 0)


def _gla(gq, gk, gv, gg, gz, wa, ba, gnw):
    nb = gq.shape[0]
    seq = lambda w: pl.BlockSpec((1, T_ALL, w), lambda b: (b, 0, 0))
    full = lambda a: pl.BlockSpec(a.shape, lambda b: (0,) * a.ndim)
    return pl.pallas_call(
        _gla_kernel,
        grid=(nb,),
        in_specs=[seq(GLA_KP), seq(GLA_KP), seq(D_GLA_V), seq(D_GLA_V), seq(Z_PAD),
                  full(wa), full(ba), full(gnw)],
        out_specs=seq(D_GLA_V),
        out_shape=jax.ShapeDtypeStruct((nb, T_ALL, D_GLA_V), BF16),
        scratch_shapes=[pltpu.VMEM((2, D_GLA_V, GLA_KP), F32),
                        pltpu.VMEM((2, T_ALL, D_GLA_V), F32),
                        pltpu.VMEM((2, T_ALL, GLA_KP), BF16),
                        pltpu.VMEM((2, T_ALL, GLA_KP), BF16),
                        pltpu.VMEM((T_ALL // GLA_CHUNK, D_GLA_V, GLA_CHUNK), BF16),
                        pltpu.VMEM((2, T_ALL // GLA_CHUNK, GLA_KP), F32)],

        compiler_params=pltpu.CompilerParams(
            dimension_semantics=("arbitrary",), vmem_limit_bytes=VMEM_LIMIT),
        name="gla_mix",
    )(gq, gk, gv, gg, gz, wa, ba, gnw)


def _outproj_kernel(h_ref, mf_ref, mn_ref, mg_ref, w_ref, mod_ref, nf_ref, o_ref, *, tile, final):
    b = pl.program_id(0)
    j = pl.program_id(1)
    acc = (_dot(mf_ref[0], w_ref[:D_FOUR, :])
           + _dot(mn_ref[0], w_ref[D_FOUR:D_FOUR + D_NA, :])
           + _dot(mg_ref[0], w_ref[D_FOUR + D_NA:, :]))
    gate = _mod_rows(mod_ref, b, j * tile, tile, 2 * D_MODEL)
    x = h_ref[0] + gate * acc
    if final:
        ms = jnp.mean(x * x, axis=-1, keepdims=True)
        x = x * lax.rsqrt(ms + EPS) * nf_ref[...]
    o_ref[0] = x


def _outproj(h, mf, mn, mg, w_out, mod_l, nf, final):
    nb = h.shape[0]
    tile = 512 if final else TOK_TILE
    rows = SEQ if final else T_ALL
    tok = lambda width: pl.BlockSpec((1, tile, width), lambda b, j: (b, j, 0))
    full = lambda shape: pl.BlockSpec(shape, lambda b, j: (0,) * len(shape))
    return pl.pallas_call(
        functools.partial(_outproj_kernel, tile=tile, final=final),
        grid=(nb, rows // tile),
        in_specs=[tok(D_MODEL), tok(D_FOUR), tok(D_NA), tok(D_GLA_V),
                  full((D_MODEL, D_MODEL)), full((16, 3 * D_MODEL)), full((1, D_MODEL))],
        out_specs=tok(D_MODEL),
        out_shape=jax.ShapeDtypeStruct((nb, rows, D_MODEL), F32),
        compiler_params=pltpu.CompilerParams(
            dimension_semantics=("arbitrary", "arbitrary"), vmem_limit_bytes=VMEM_LIMIT),
        name="outproj_final" if final else "outproj",
    )(h, mf, mn, mg, w_out, mod_l, nf)


def _qk_lanes(a):
    n_freq = GLA_DK // 4
    blocks = []
    for part in range(2):
        for h in range(GLA_HEADS):
            for half in range(2):
                c0 = h * GLA_DK + half * 2 * n_freq + part * n_freq
                blocks.append(a[..., c0:c0 + n_freq])
        blocks.append(jnp.zeros(a.shape[:-1] + (LANE - GLA_HEADS * 2 * n_freq,), a.dtype))
    return jnp.concatenate(blocks, axis=-1)


def _permute_w_in(w_in):
    sizes = (D_FOUR, D_FOUR, D_NA, D_NA, D_NA, D_NA, D_GLA_K, D_GLA_K, D_GLA_V, D_GLA_V,
             GLA_RANK, GLA_RANK)
    offs = np.concatenate([[0], np.cumsum(sizes)])
    part = lambda i: w_in[:, :, offs[i]:offs[i + 1]].astype(BF16)
    z_pad = jnp.zeros(w_in.shape[:2] + (Z_PAD - 2 * GLA_RANK,), BF16)
    q_scale = NA_HEAD_DIM ** -0.5 * LOG2E
    cols = [part(0), part(1), (w_in[:, :, offs[2]:offs[3]] * q_scale).astype(BF16),
            part(3), part(5), _qk_lanes(part(6)), _qk_lanes(part(7)), part(8), part(9),
            part(10), part(11), z_pad]
    w_nvt = jnp.swapaxes(part(4), 1, 2)
    return jnp.concatenate(cols, axis=-1), w_nvt


def _decay_weights(w_f, b_f, w_b, b_b):
    zero = jnp.zeros_like(_qk_lanes(w_f))
    top = jnp.concatenate([_qk_lanes(w_f), zero], axis=-1)
    bot = jnp.concatenate([zero, _qk_lanes(w_b)], axis=-1)
    pad = jnp.zeros((w_f.shape[0], Z_PAD - 2 * GLA_RANK, 2 * GLA_KP), w_f.dtype)
    wa = jnp.concatenate([top, bot, pad], axis=1).astype(BF16)
    ba = jnp.concatenate([_qk_lanes(b_f), _qk_lanes(b_b)], axis=-1)[:, None, :]
    return wa, ba


def kernel(x, c, ctx, c_ctx, w_ada, b_ada, norm_w, w_in, w_four, rpb, w_alpha_fwd, b_alpha_fwd,
           w_alpha_bwd, b_alpha_bwd, gla_norm_w, w_out, norm_f):
    nb = x.shape[0]
    depth = w_in.shape[0]
    assert x.shape == (nb, SEQ, D_MODEL) and ctx.shape == (nb, CTX_LEN, D_MODEL) and nb <= 8

    w_in_p, w_nvt = _permute_w_in(w_in)
    w_out_b = w_out.astype(BF16)
    w_four_b = w_four.astype(BF16)
    wa, ba = _decay_weights(w_alpha_fwd, b_alpha_fwd, w_alpha_bwd, b_alpha_bwd)
    gnw = jnp.tile(gla_norm_w, (1, GLA_HEADS)).reshape(depth, 1, D_GLA_V)
    n_off = 2 * NA_KW - 1
    rpb_rev = rpb[..., ::-1]
    rpb_a = jnp.pad(rpb_rev, ((0, 0), (0, 0), (0, 1), (0, LANE - n_off)))
    rpb_b = jnp.pad(rpb_rev, ((0, 0), (0, 0), (0, 1), (GRID_W, LANE - GRID_W - n_off)))

    cos_t, sin_t = (jnp.asarray(t) for t in _rope_tables())
    c64, s64 = _dft_tables(FOUR_DIM)
    cl, sl = _dft_tables(SEQ)
    ccx, scx = _dft_tables(CTX_LEN)
    tabs = [jnp.asarray(t, dtype=F32).astype(BF16) for t in
            (_block_diag(c64, FOUR_GROUPS), _block_diag(s64, FOUR_GROUPS), cl, sl, ccx, scx)]

    cc = jnp.zeros((16, D_MODEL), F32).at[:nb].set(c).at[8].set(c_ctx)
    mod = _modulation(cc, w_ada, b_ada)

    h = jnp.concatenate([x, ctx], axis=1)
    for l in range(depth):
        final = l == depth - 1
        (p_four, nq, nk, nvt, ng, gq, gk, gv, gg, gz) = _inproj(
            h, mod[l], norm_w[l].reshape(1, D_MODEL), w_in_p[l], w_nvt[l], cos_t, sin_t)
        mf = _fourier(p_four, tabs, w_four_b[l])
        mn = _na(nq, nk, nvt, ng, rpb_a[l], rpb_b[l])
        mg = _gla(gq, gk, gv, gg, gz, wa[l], ba[l], gnw[l])
        h = _outproj(h, mf, mn, mg, w_out_b[l], mod[l], norm_f.reshape(1, D_MODEL), final)
    return h
```

```python
import functools

import numpy as np
import jax
import jax.numpy as jnp
from jax import lax
from jax.experimental import pallas as pl
from jax.experimental.pallas import tpu as pltpu

F32 = jnp.float32
BF16 = jnp.bfloat16

D_MODEL = 1024
SEQ = 2048
CTX_LEN = 256
T_ALL = SEQ + CTX_LEN
GRID_W = 64
GRID_H = SEQ // GRID_W
EPS = 1e-6

FOUR_GROUPS, FOUR_DIM = 4, 64
D_FOUR = FOUR_GROUPS * FOUR_DIM
NA_HEADS, NA_HEAD_DIM = 6, 64
D_NA = NA_HEADS * NA_HEAD_DIM
NA_KH, NA_KW = 8, 16
GLA_HEADS, GLA_DK, GLA_DV = 4, 48, 96
D_GLA_K = GLA_HEADS * GLA_DK
D_GLA_V = GLA_HEADS * GLA_DV
GLA_RANK = 16
GLA_TAU = 16.0
GLA_CHUNK = 64
ROPE_BASE = 10000.0
LOG2E = 1.4426950408889634

IN_SIZES = (D_FOUR, D_FOUR, D_NA, D_NA, D_NA, D_NA, D_GLA_K, D_GLA_K, D_GLA_V, D_GLA_V,
            GLA_RANK, GLA_RANK)
D_IN = sum(IN_SIZES)

LANE = 128
GLA_KP = 2 * LANE
Z_PAD = LANE
N_CHUNKS_X = SEQ // GLA_CHUNK
N_CHUNKS_C = CTX_LEN // GLA_CHUNK
GLA_TILE = 256

NA_QROWS = 4
NA_KROWS = NA_QROWS + NA_KH
NA_QB = NA_QROWS * GRID_W
NA_KB = NA_KROWS * GRID_W
NA_NBLK = GRID_H // NA_QROWS

TOK_TILE = 768
VMEM_LIMIT = 56 * 1024 * 1024

COL_FOUR = 0
COL_NA = COL_FOUR + 2 * D_FOUR
COL_GQ = COL_NA + 3 * D_NA
COL_GK = COL_GQ + GLA_KP
COL_GV = COL_GK + GLA_KP
COL_GG = COL_GV + D_GLA_V
COL_Z = COL_GG + D_GLA_V
N_COLS = COL_Z + Z_PAD


def _dot(a, b):
    return jnp.dot(a, b, preferred_element_type=F32)


def _dot_nt(a, b):
    return lax.dot_general(a, b, (((1,), (1,)), ((), ())), preferred_element_type=F32)


def _silu(x):
    return x / (1.0 + jnp.exp(-x))


def _reduce_rows(x, op):
    r, n = x.shape
    part = op(x.reshape(r // 8, 8, n), axis=0)
    return op(part, axis=0, keepdims=True)


def _split_bf16(x):
    hi = x.astype(BF16)
    lo = (x - hi.astype(F32)).astype(BF16)
    return hi, lo


def _qk_pieces():
    n_freq = GLA_DK // 4
    return [[h * GLA_DK + half * 2 * n_freq + part * n_freq
             for h in range(GLA_HEADS) for half in range(2)] for part in range(2)]


def _rope_tables():
    n_freq = GLA_DK // 4
    inv = ROPE_BASE ** (-np.arange(n_freq, dtype=np.float64) / n_freq)
    pos = np.arange(SEQ)
    prow, pcol = pos // GRID_W, pos % GRID_W
    cos = np.ones((T_ALL, LANE), np.float64)
    sin = np.zeros((T_ALL, LANE), np.float64)
    for h in range(GLA_HEADS):
        for half, p in enumerate((prow, pcol)):
            ang = p[:, None] * inv[None, :]
            lo = h * 2 * n_freq + half * n_freq
            cos[:SEQ, lo:lo + n_freq] = np.cos(ang)
            sin[:SEQ, lo:lo + n_freq] = np.sin(ang)
    return cos.astype(np.float32), sin.astype(np.float32)


def _dft_tables(n):
    idx = (np.arange(n)[:, None] * np.arange(n)[None, :]) % n
    ang = 2.0 * np.pi * idx / n
    return np.cos(ang) / np.sqrt(n), np.sin(ang) / np.sqrt(n)


def _block_diag(m, reps):
    n = m.shape[0]
    out = np.zeros((n * reps, n * reps), m.dtype)
    for r in range(reps):
        out[r * n:(r + 1) * n, r * n:(r + 1) * n] = m
    return out


def _na_row_offsets():
    out = []
    for r0 in (0, NA_QROWS, GRID_H - NA_QROWS):
        k0 = int(np.clip(r0 - NA_KH // 2, 0, GRID_H - NA_KROWS))
        per_type = []
        for a in range(NA_QROWS):
            r = r0 + a
            rs = int(np.clip(r - NA_KH // 2, 0, GRID_H - NA_KH))
            per_type.append([(k0 + j) - r + NA_KH - 1 if rs <= k0 + j < rs + NA_KH else None
                             for j in range(NA_KROWS)])
        out.append(per_type)
    return out


def _mod_kernel(c_ref, w_ref, b_ref, o_ref):
    c = c_ref[...]
    s_hi, s_lo = _split_bf16(_silu(c))
    w_hi, w_lo = _split_bf16(w_ref[0])
    acc = _dot(s_hi, w_hi) + _dot(s_lo, w_hi) + _dot(s_hi, w_lo)
    o_ref[0] = acc + b_ref[0]


def _modulation(cc, w_ada, b_ada):
    depth = w_ada.shape[0]
    n_mod = w_ada.shape[2]
    tile = D_MODEL
    return pl.pallas_call(
        _mod_kernel,
        grid=(depth, n_mod // tile),
        in_specs=[
            pl.BlockSpec((16, D_MODEL), lambda l, j: (0, 0)),
            pl.BlockSpec((1, D_MODEL, tile), lambda l, j: (l, 0, j)),
            pl.BlockSpec((1, 1, tile), lambda l, j: (l, 0, j)),
        ],
        out_specs=pl.BlockSpec((1, 16, tile), lambda l, j: (l, 0, j)),
        out_shape=jax.ShapeDtypeStruct((depth, 16, n_mod), F32),
        compiler_params=pltpu.CompilerParams(
            dimension_semantics=("arbitrary", "arbitrary"), vmem_limit_bytes=VMEM_LIMIT),
        name="adaln_mod",
    )(cc, w_ada, b_ada.reshape(depth, 1, n_mod))


def _mod_rows(mod_ref, b, row0, n_rows, col0):
    rows = row0 + lax.broadcasted_iota(jnp.int32, (n_rows, 1), 0)
    m_x = mod_ref[0, pl.ds(b, 1), col0:col0 + D_MODEL]
    m_c = mod_ref[0, 8:9, col0:col0 + D_MODEL]
    return jnp.where(rows >= SEQ, m_c, m_x)


def _fill_weight(w_ref, w_scr, dst0, src0, width, scale=None):
    rows = w_scr.shape[0]
    for c in range(0, width, LANE):
        n = min(LANE, width - c)
        blk = w_ref[0, :, src0 + c:src0 + c + n]
        if scale is not None:
            blk = blk * scale
        if n < LANE:
            blk = jnp.concatenate([blk, jnp.zeros((rows, LANE - n), F32)], axis=1)
        w_scr[:, dst0 + c:dst0 + c + LANE] = blk.astype(BF16)


def _inproj_kernel(h_ref, mod_ref, nw_ref, w_ref, wvt_ref, cos_ref, sin_ref,
                   four_ref, nq_ref, nk_ref, nvt_ref, ng_ref,
                   gq_ref, gk_ref, gv_ref, gg_ref, gz_ref, hx_scr, w_scr):
    b = pl.program_id(0)
    j = pl.program_id(1)

    @pl.when((b == 0) & (j == 0))
    def _():
        src = np.concatenate([[0], np.cumsum(IN_SIZES)])
        _fill_weight(w_ref, w_scr, COL_FOUR, src[0], 2 * D_FOUR)
        _fill_weight(w_ref, w_scr, COL_NA, src[2], D_NA, NA_HEAD_DIM ** -0.5 * LOG2E)
        _fill_weight(w_ref, w_scr, COL_NA + D_NA, src[3], D_NA)
        _fill_weight(w_ref, w_scr, COL_NA + 2 * D_NA, src[5], D_NA)
        n_freq = GLA_DK // 4
        pad = jnp.zeros((D_MODEL, LANE - GLA_HEADS * 2 * n_freq), F32)
        for dst0, src0 in ((COL_GQ, src[6]), (COL_GK, src[7])):
            for part, offs in enumerate(_qk_pieces()):
                tile = jnp.concatenate(
                    [w_ref[0, :, src0 + o:src0 + o + n_freq] for o in offs] + [pad], axis=1)
                w_scr[:, dst0 + part * LANE:dst0 + (part + 1) * LANE] = tile.astype(BF16)
        _fill_weight(w_ref, w_scr, COL_GV, src[8], D_GLA_V)
        _fill_weight(w_ref, w_scr, COL_GG, src[9], D_GLA_V)
        _fill_weight(w_ref, w_scr, COL_Z, src[10], 2 * GLA_RANK)

    x = h_ref[0]
    ms = jnp.mean(x * x, axis=-1, keepdims=True)
    y = x * lax.rsqrt(ms + EPS) * nw_ref[0]
    shift = _mod_rows(mod_ref, b, j * TOK_TILE, TOK_TILE, 0)
    scale = _mod_rows(mod_ref, b, j * TOK_TILE, TOK_TILE, D_MODEL)
    hx_scr[...] = (y * (1.0 + scale) + shift).astype(BF16)

    def proj(c0, width):
        return _dot(hx_scr[...], w_scr[:, c0:c0 + width])

    four_ref[0] = proj(COL_FOUR, 2 * D_FOUR).astype(BF16)
    nq_ref[0] = proj(COL_NA, D_NA).astype(BF16)
    nk_ref[0] = proj(COL_NA + D_NA, D_NA).astype(BF16)
    nvt_ref[0] = _dot_nt(wvt_ref[0], hx_scr[...]).astype(BF16)
    ng_ref[0] = proj(COL_NA + 2 * D_NA, D_NA).astype(BF16)
    gv_ref[0] = proj(COL_GV, D_GLA_V).astype(BF16)
    gg_ref[0] = proj(COL_GG, D_GLA_V).astype(BF16)
    gz_ref[0] = proj(COL_Z, Z_PAD)

    cos = cos_ref[...]
    sin = sin_ref[...]

    def rope(c0, out_ref, scale_out):
        u = proj(c0, GLA_KP)
        u1, u2 = u[:, :LANE], u[:, LANE:]
        out_ref[0, :, :LANE] = (u1 * cos - u2 * sin) * scale_out
        out_ref[0, :, LANE:] = (u1 * sin + u2 * cos) * scale_out

    rope(COL_GQ, gq_ref, GLA_DK ** -0.5)
    rope(COL_GK, gk_ref, 1.0)


def _inproj(h, mod, l, norm_w, w_in, w_nvt, cos_t, sin_t):
    nb = h.shape[0]
    nt = T_ALL // TOK_TILE
    tok = lambda width: pl.BlockSpec((1, TOK_TILE, width), lambda b, j: (b, j, 0))
    layer = lambda a: pl.BlockSpec((1,) + a.shape[1:], lambda b, j: (l,) + (0,) * (a.ndim - 1))
    out_widths = [(2 * D_FOUR, BF16), (D_NA, BF16), (D_NA, BF16), (None, BF16), (D_NA, BF16),
                  (GLA_KP, F32), (GLA_KP, F32), (D_GLA_V, BF16), (D_GLA_V, BF16), (Z_PAD, F32)]
    vt_spec = pl.BlockSpec((1, D_NA, TOK_TILE), lambda b, j: (b, 0, j))
    vt_shape = jax.ShapeDtypeStruct((nb, D_NA, T_ALL), BF16)
    return pl.pallas_call(
        _inproj_kernel,
        grid=(nb, nt),
        in_specs=[
            tok(D_MODEL),
            layer(mod),
            layer(norm_w),
            layer(w_in),
            layer(w_nvt),
            pl.BlockSpec((TOK_TILE, LANE), lambda b, j: (j, 0)),
            pl.BlockSpec((TOK_TILE, LANE), lambda b, j: (j, 0)),
        ],
        out_specs=[vt_spec if w is None else tok(w) for w, _ in out_widths],
        out_shape=[vt_shape if w is None else jax.ShapeDtypeStruct((nb, T_ALL, w), dt)
                   for w, dt in out_widths],
        scratch_shapes=[pltpu.VMEM((TOK_TILE, D_MODEL), BF16), pltpu.VMEM((D_MODEL, N_COLS), BF16)],
        compiler_params=pltpu.CompilerParams(
            dimension_semantics=("arbitrary", "arbitrary"), vmem_limit_bytes=VMEM_LIMIT),
        name="inproj",
    )(h, mod, norm_w, w_in, w_nvt, cos_t, sin_t)


def _fourier_kernel(p_ref, cbd_ref, sbd_ref, cl_ref, sl_ref, cc_ref, sc_ref, wf_ref, o_ref,
                    uc_scr, us_scr):
    u = p_ref[0, :, :D_FOUR]
    uc_scr[...] = _dot(u, cbd_ref[...]).astype(BF16)
    us_scr[...] = _dot(u, sbd_ref[...]).astype(BF16)

    def finish(f, r0, n):
        four = _dot(f.astype(BF16), wf_ref[...])
        gate = p_ref[0, pl.ds(r0, n), D_FOUR:].astype(F32)
        o_ref[0, pl.ds(r0, n), :] = (four * _silu(gate)).astype(BF16)

    tile = 256

    def body(i, carry):
        r0 = pl.multiple_of(i * tile, tile)
        f = (_dot(cl_ref[pl.ds(r0, tile), :], uc_scr[:SEQ, :])
             - _dot(sl_ref[pl.ds(r0, tile), :], us_scr[:SEQ, :]))
        finish(f, r0, tile)
        return carry

    lax.fori_loop(0, SEQ // tile, body, 0)
    f_c = _dot(cc_ref[...], uc_scr[SEQ:, :]) - _dot(sc_ref[...], us_scr[SEQ:, :])
    finish(f_c, SEQ, CTX_LEN)


def _fourier(p_four, tabs, wf):
    nb = p_four.shape[0]
    full = lambda a: pl.BlockSpec(a.shape, lambda b: (0,) * a.ndim)
    return pl.pallas_call(
        _fourier_kernel,
        grid=(nb,),
        in_specs=[pl.BlockSpec((1, T_ALL, 2 * D_FOUR), lambda b: (b, 0, 0))]
        + [full(t) for t in tabs] + [full(wf)],
        out_specs=pl.BlockSpec((1, T_ALL, D_FOUR), lambda b: (b, 0, 0)),
        out_shape=jax.ShapeDtypeStruct((nb, T_ALL, D_FOUR), BF16),
        scratch_shapes=[pltpu.VMEM((T_ALL, D_FOUR), BF16), pltpu.VMEM((T_ALL, D_FOUR), BF16)],
        compiler_params=pltpu.CompilerParams(
            dimension_semantics=("arbitrary",), vmem_limit_bytes=VMEM_LIMIT),
        name="fourier_mix",
    )(p_four, *tabs, wf)


def _na_build_bias(ra_ref, rb_ref, tbl_ref):
    w = GRID_W
    ck = lax.broadcasted_iota(jnp.int32, (w, 2 * w), 0)
    ln = lax.broadcasted_iota(jnp.int32, (w, 2 * w), 1)
    cq = ln % w
    cs = jnp.clip(cq - NA_KW // 2, 0, w - NA_KW)
    col_ok = (ck >= cs) & (ck < cs + NA_KW)
    left = ln < w
    neg = jnp.full((w, 2 * w), -jnp.inf, F32)
    offs = _na_row_offsets()
    cache = {}
    for typ in range(3):
        for ap in range(NA_QROWS // 2):
            for jr in range(NA_KROWS):
                d0 = offs[typ][2 * ap][jr]
                d1 = offs[typ][2 * ap + 1][jr]
                for h in range(NA_HEADS):
                    key = (h, d0, d1)
                    if key not in cache:
                        if d0 is None and d1 is None:
                            cache[key] = neg
                        else:
                            ok = col_ok
                            if d0 is None:
                                row = rb_ref[h, d1:d1 + 1, :]
                                ok = ok & jnp.logical_not(left)
                            elif d1 is None:
                                row = ra_ref[h, d0:d0 + 1, :]
                                ok = ok & left
                            else:
                                row = ra_ref[h, d0:d0 + 1, :] + rb_ref[h, d1:d1 + 1, :]
                            skew = pltpu.roll(jnp.broadcast_to(row * LOG2E, (w, 2 * w)),
                                              2 * w - (NA_KW - 1), 1, stride=1, stride_axis=0)
                            cache[key] = jnp.where(ok, skew, neg)
                    tbl_ref[h * 3 + typ, jr * w:(jr + 1) * w, ap * 2 * w:(ap + 1) * 2 * w] = cache[key]


def _na_kernel(q_ref, k_ref, vt_ref, g_ref, ra_ref, rb_ref, o_ref, bias_ref):
    j = pl.program_id(1)
    lane = lax.broadcasted_iota(jnp.int32, (1, LANE), 1)
    first_head_lanes = lane < NA_HEAD_DIM
    first_head_rows = lax.broadcasted_iota(jnp.int32, (LANE, 1), 0) < NA_HEAD_DIM

    @pl.when((pl.program_id(0) == 0) & (j == 0))
    def _():
        _na_build_bias(ra_ref, rb_ref, bias_ref)

    def attend(local):
        if local:
            k0 = pl.multiple_of(jnp.clip(j * NA_QROWS - NA_KH // 2, 0, GRID_H - NA_KROWS) * GRID_W,
                                NA_QB)
            typ = jnp.where(j == 0, 0, jnp.where(j == NA_NBLK - 1, 2, 1))
        heads = range(NA_HEADS)
        cols = [slice(h // 2 * LANE, (h // 2 + 1) * LANE) for h in heads]
        s_c, s_l = [], []
        for h in heads:
            qp = q_ref[0, :, cols[h]]
            keep = first_head_lanes if h % 2 == 0 else jnp.logical_not(first_head_lanes)
            qm = jnp.where(keep, qp, jnp.zeros_like(qp))
            s_c.append(_dot_nt(k_ref[0, SEQ:, cols[h]], qm))
            if local:
                s_l.append(_dot_nt(k_ref[0, pl.ds(k0, NA_KB), cols[h]], qm) + bias_ref[h * 3 + typ])
        e_c, e_l, den = [], [], []
        for h in heads:
            m = _reduce_rows(s_c[h], jnp.max)
            if local:
                m = jnp.maximum(m, _reduce_rows(s_l[h], jnp.max))
            e = jnp.exp2(s_c[h] - m)
            d = _reduce_rows(e, jnp.sum)
            e_c.append(e.astype(BF16))
            if local:
                e = jnp.exp2(s_l[h] - m)
                d = d + _reduce_rows(e, jnp.sum)
                e_l.append(e.astype(BF16))
            den.append(d)
        for p in range(NA_HEADS // 2):
            acc = None
            for h in (2 * p, 2 * p + 1):
                o = _dot(vt_ref[0, cols[h], SEQ:], e_c[h])
                if local:
                    o = o + _dot(vt_ref[0, cols[h], pl.ds(k0, NA_KB)], e_l[h])
                o = o / den[h]
                acc = o if h % 2 == 0 else jnp.where(first_head_rows, acc, o)
            gate = g_ref[0, :, cols[2 * p]].astype(F32)
            o_ref[0, :, cols[2 * p]] = (acc.T * _silu(gate)).astype(BF16)

    @pl.when(j < NA_NBLK)
    def _():
        attend(True)

    @pl.when(j == NA_NBLK)
    def _():
        attend(False)


def _na(nq, nk, nvt, ng, rpb_a, rpb_b):
    nb = nq.shape[0]
    blk = pl.BlockSpec((1, NA_QB, D_NA), lambda b, j: (b, j, 0))
    seq = pl.BlockSpec((1, T_ALL, D_NA), lambda b, j: (b, 0, 0))
    seq_t = pl.BlockSpec((1, D_NA, T_ALL), lambda b, j: (b, 0, 0))
    tab = pl.BlockSpec(rpb_a.shape, lambda b, j: (0, 0, 0))
    return pl.pallas_call(
        _na_kernel,
        grid=(nb, NA_NBLK + 1),
        in_specs=[blk, seq, seq_t, blk, tab, tab],
        out_specs=blk,
        out_shape=jax.ShapeDtypeStruct((nb, T_ALL, D_NA), BF16),
        scratch_shapes=[pltpu.VMEM((NA_HEADS * 3, NA_KB, NA_QB), F32)],
        compiler_params=pltpu.CompilerParams(
            dimension_semantics=("arbitrary", "arbitrary"), vmem_limit_bytes=VMEM_LIMIT),
        name="nbr_attention",
    )(nq, nk, nvt, ng, rpb_a, rpb_b)


def _gla_kernel(q_ref, k_ref, v_ref, g_ref, z_ref, wa_ref, ba_ref, nw_ref, o_ref,
                st_scr, o_scr, qd_scr, ke_scr, vt_scr, dec_scr):
    C = GLA_CHUNK
    TR = GLA_TILE
    n_sub = TR // C
    kl = lax.broadcasted_iota(jnp.int32, (1, GLA_KP), 1) % LANE
    k_head = jnp.where(kl < GLA_HEADS * (GLA_DK // 2), kl // (GLA_DK // 2), GLA_HEADS)
    v_head = lax.broadcasted_iota(jnp.int32, (1, D_GLA_V), 1) // GLA_DV
    st_mask = (lax.broadcasted_iota(jnp.int32, (D_GLA_V, 1), 0) // GLA_DV) == k_head
    ti4 = lax.broadcasted_iota(jnp.int32, (C, GLA_HEADS * C), 0)
    si4 = lax.broadcasted_iota(jnp.int32, (C, GLA_HEADS * C), 1) % C
    tr = lax.broadcasted_iota(jnp.int32, (TR, TR), 0)
    sr = lax.broadcasted_iota(jnp.int32, (TR, TR), 1)
    same_chunk = (tr // C) == (sr // C)
    tri = [jnp.where(same_chunk & (sr <= tr), 1.0, 0.0).astype(BF16),
           jnp.where(same_chunk & (sr >= tr), 1.0, 0.0).astype(BF16)]

    def tile_body(i, carry):
        r0 = pl.multiple_of(i * TR, TR)
        rows = pl.ds(r0, TR)
        z = z_ref[0, rows, :].astype(BF16)
        g = _dot(z, wa_ref[...]) + ba_ref[...]
        la = (jnp.minimum(g, 0.0) - jnp.log1p(jnp.exp(-jnp.abs(g)))) * (1.0 / GLA_TAU)
        q = q_ref[0, rows, :]
        k = k_ref[0, rows, :]
        v = v_ref[0, rows, :]
        v_bd = []
        for c in range(n_sub):
            v_c = v[c * C:(c + 1) * C]
            v_bd.append(jnp.concatenate(
                [jnp.where(v_head == h, v_c, jnp.zeros_like(v_c)) for h in range(GLA_HEADS)], axis=0))
            vt_scr[i * n_sub + c] = v_c.astype(F32).T.astype(BF16)
        scores = []
        for d in range(2):
            fwd = d == 0
            la_hi, la_lo = _split_bf16(la[:, d * GLA_KP:(d + 1) * GLA_KP])
            bcum = _dot(tri[d], la_hi) + _dot(tri[d], la_lo)
            ends = [bcum[(c + 1) * C - 1:(c + 1) * C] if fwd else bcum[c * C:c * C + 1]
                    for c in range(n_sub)]
            b_end = jnp.concatenate([jnp.broadcast_to(e, (C, GLA_KP)) for e in ends], axis=0)
            q_dec = (q * jnp.exp(bcum)).astype(BF16)
            k_dec = k * jnp.exp(-bcum)
            qd_scr[d, rows, :] = q_dec
            ke_scr[d, rows, :] = (k * jnp.exp(b_end - bcum)).astype(BF16)
            for c in range(n_sub):
                dec_scr[d, pl.ds(i * n_sub + c, 1), :] = jnp.exp(ends[c])
                k_c = k_dec[c * C:(c + 1) * C]
                k_bd = jnp.concatenate(
                    [jnp.where(k_head == h, k_c, 0.0) for h in range(GLA_HEADS)], axis=0).astype(BF16)
                scores.append(_dot_nt(q_dec[c * C:(c + 1) * C], k_bd))
        for d in range(2):
            causal = (si4 <= ti4) if d == 0 else (si4 >= ti4)
            for c in range(n_sub):
                a = jnp.where(causal, scores[d * n_sub + c], 0.0).astype(BF16)
                o_scr[d, pl.ds(r0 + c * C, C), :] = _dot(a, v_bd[c])
        return carry

    lax.fori_loop(0, T_ALL // TR, tile_body, 0)

    st_scr[...] = jnp.zeros_like(st_scr)
    n_all = N_CHUNKS_C + N_CHUNKS_X

    def step(i, carry):
        in_ctx = i < N_CHUNKS_C
        idx_f = jnp.where(in_ctx, N_CHUNKS_X + i, i - N_CHUNKS_C)
        idx_b = jnp.where(in_ctx, N_CHUNKS_X + N_CHUNKS_C - 1 - i, n_all - 1 - i)
        for d, idx in ((0, idx_f), (1, idx_b)):
            off = pl.ds(pl.multiple_of(idx * C, C), C)
            st = st_scr[d]
            o_scr[d, off, :] += _dot_nt(qd_scr[d, off, :], st.astype(BF16))
            half = D_GLA_V // 2
            ke = ke_scr[d, off, :]
            upd = jnp.concatenate([_dot(vt_scr[idx, :half, :], ke), _dot(vt_scr[idx, half:, :], ke)], axis=0)
            st_scr[d] = st * dec_scr[d, pl.ds(idx, 1), :] + jnp.where(st_mask, upd, 0.0)
        return carry

    lax.fori_loop(0, n_all, step, 0, unroll=4)

    head_ones = jnp.where(
        (lax.broadcasted_iota(jnp.int32, (D_GLA_V, 1), 0) // GLA_DV) == v_head, 1.0, 0.0).astype(BF16)
    tile = 256

    def fin(i, carry):
        r0 = pl.multiple_of(i * tile, tile)
        o = o_scr[0, pl.ds(r0, tile), :] + o_scr[1, pl.ds(r0, tile), :]
        sq_hi, sq_lo = _split_bf16(o * o)
        ms = (_dot(sq_hi, head_ones) + _dot(sq_lo, head_ones)) * (1.0 / GLA_DV)
        y = o * lax.rsqrt(ms + EPS) * nw_ref[...]
        gate = g_ref[0, pl.ds(r0, tile), :].astype(F32)
        o_ref[0, pl.ds(r0, tile), :] = (y * _silu(gate)).astype(BF16)
        return carry

    lax.fori_loop(0, T_ALL // tile, fin, 0)


def _gla(gq, gk, gv, gg, gz, wa, ba, gnw):
    nb = gq.shape[0]
    seq = lambda w: pl.BlockSpec((1, T_ALL, w), lambda b: (b, 0, 0))
    full = lambda a: pl.BlockSpec(a.shape, lambda b: (0,) * a.ndim)
    return pl.pallas_call(
        _gla_kernel,
        grid=(nb,),
        in_specs=[seq(GLA_KP), seq(GLA_KP), seq(D_GLA_V), seq(D_GLA_V), seq(Z_PAD),
                  full(wa), full(ba), full(gnw)],
        out_specs=seq(D_GLA_V),
        out_shape=jax.ShapeDtypeStruct((nb, T_ALL, D_GLA_V), BF16),
        scratch_shapes=[pltpu.VMEM((2, D_GLA_V, GLA_KP), F32),
                        pltpu.VMEM((2, T_ALL, D_GLA_V), F32),
                        pltpu.VMEM((2, T_ALL, GLA_KP), BF16),
                        pltpu.VMEM((2, T_ALL, GLA_KP), BF16),
                        pltpu.VMEM((T_ALL // GLA_CHUNK, D_GLA_V, GLA_CHUNK), BF16),
                        pltpu.VMEM((2, T_ALL // GLA_CHUNK, GLA_KP), F32)],

        compiler_params=pltpu.CompilerParams(
            dimension_semantics=("arbitrary",), vmem_limit_bytes=VMEM_LIMIT),
        name="gla_mix",
    )(gq, gk, gv, gg, gz, wa, ba, gnw)


def _outproj_kernel(h_ref, mf_ref, mn_ref, mg_ref, w_ref, mod_ref, nf_ref, o_ref, w_scr, *, tile, final):
    b = pl.program_id(0)
    j = pl.program_id(1)

    @pl.when((b == 0) & (j == 0))
    def _():
        for r in range(0, D_MODEL, 256):
            w_scr[r:r + 256, :] = w_ref[0, r:r + 256, :].astype(BF16)

    acc = (_dot(mf_ref[0], w_scr[:D_FOUR, :])
           + _dot(mn_ref[0], w_scr[D_FOUR:D_FOUR + D_NA, :])
           + _dot(mg_ref[0], w_scr[D_FOUR + D_NA:, :]))
    gate = _mod_rows(mod_ref, b, j * tile, tile, 2 * D_MODEL)
    x = h_ref[0] + gate * acc
    if final:
        ms = jnp.mean(x * x, axis=-1, keepdims=True)
        x = x * lax.rsqrt(ms + EPS) * nf_ref[...]
    o_ref[0] = x


def _outproj(h, mf, mn, mg, w_out, mod, l, nf, final):
    nb = h.shape[0]
    tile = 512 if final else TOK_TILE
    rows = SEQ if final else T_ALL
    tok = lambda width: pl.BlockSpec((1, tile, width), lambda b, j: (b, j, 0))
    layer = lambda a: pl.BlockSpec((1,) + a.shape[1:], lambda b, j: (l,) + (0,) * (a.ndim - 1))
    return pl.pallas_call(
        functools.partial(_outproj_kernel, tile=tile, final=final),
        grid=(nb, rows // tile),
        in_specs=[tok(D_MODEL), tok(D_FOUR), tok(D_NA), tok(D_GLA_V),
                  layer(w_out), layer(mod), pl.BlockSpec((1, D_MODEL), lambda b, j: (0, 0))],
        out_specs=tok(D_MODEL),
        out_shape=jax.ShapeDtypeStruct((nb, rows, D_MODEL), F32),
        scratch_shapes=[pltpu.VMEM((D_MODEL, D_MODEL), BF16)],
        compiler_params=pltpu.CompilerParams(
            dimension_semantics=("arbitrary", "arbitrary"), vmem_limit_bytes=VMEM_LIMIT),
        name="outproj_final" if final else "outproj",
    )(h, mf, mn, mg, w_out, mod, nf)


def _decay_weights(w_f, b_f, w_b, b_b):
    n_freq = GLA_DK // 4
    lane_pad = lambda a: jnp.zeros(a.shape[:-1] + (LANE - GLA_HEADS * 2 * n_freq,), a.dtype)
    lanes = lambda a: jnp.concatenate(
        [blk for offs in _qk_pieces() for blk in [a[..., o:o + n_freq] for o in offs] + [lane_pad(a)]],
        axis=-1)
    zero = jnp.zeros_like(lanes(w_f))
    top = jnp.concatenate([lanes(w_f), zero], axis=-1)
    bot = jnp.concatenate([zero, lanes(w_b)], axis=-1)
    pad = jnp.zeros((w_f.shape[0], Z_PAD - 2 * GLA_RANK, 2 * GLA_KP), w_f.dtype)
    wa = jnp.concatenate([top, bot, pad], axis=1).astype(BF16)
    ba = jnp.concatenate([lanes(b_f), lanes(b_b)], axis=-1)[:, None, :]
    return wa, ba


def kernel(x, c, ctx, c_ctx, w_ada, b_ada, norm_w, w_in, w_four, rpb, w_alpha_fwd, b_alpha_fwd,
           w_alpha_bwd, b_alpha_bwd, gla_norm_w, w_out, norm_f):
    nb = x.shape[0]
    depth = w_in.shape[0]
    assert x.shape == (nb, SEQ, D_MODEL) and ctx.shape == (nb, CTX_LEN, D_MODEL) and nb <= 8

    v0 = sum(IN_SIZES[:4])
    w_nvt = jnp.swapaxes(w_in[:, :, v0:v0 + D_NA], 1, 2).astype(BF16)
    w_four_b = w_four.astype(BF16)
    norm_w3 = norm_w.reshape(depth, 1, D_MODEL)
    wa, ba = _decay_weights(w_alpha_fwd, b_alpha_fwd, w_alpha_bwd, b_alpha_bwd)
    gnw = jnp.tile(gla_norm_w, (1, GLA_HEADS)).reshape(depth, 1, D_GLA_V)
    n_off = 2 * NA_KW - 1
    rpb_rev = rpb[..., ::-1]
    rpb_a = jnp.pad(rpb_rev, ((0, 0), (0, 0), (0, 1), (0, LANE - n_off)))
    rpb_b = jnp.pad(rpb_rev, ((0, 0), (0, 0), (0, 1), (GRID_W, LANE - GRID_W - n_off)))

    cos_t, sin_t = (jnp.asarray(t) for t in _rope_tables())
    c64, s64 = _dft_tables(FOUR_DIM)
    cl, sl = _dft_tables(SEQ)
    ccx, scx = _dft_tables(CTX_LEN)
    tabs = [jnp.asarray(t, dtype=F32).astype(BF16) for t in
            (_block_diag(c64, FOUR_GROUPS), _block_diag(s64, FOUR_GROUPS), cl, sl, ccx, scx)]

    cc = jnp.zeros((16, D_MODEL), F32).at[:nb].set(c).at[8].set(c_ctx)
    mod = _modulation(cc, w_ada, b_ada)

    h = jnp.concatenate([x, ctx], axis=1)
    for l in range(depth):
        final = l == depth - 1
        (p_four, nq, nk, nvt, ng, gq, gk, gv, gg, gz) = _inproj(
            h, mod, l, norm_w3, w_in, w_nvt, cos_t, sin_t)
        mf = _fourier(p_four, tabs, w_four_b[l])
        mn = _na(nq, nk, nvt, ng, rpb_a[l], rpb_b[l])
        mg = _gla(gq, gk, gv, gg, gz, wa[l], ba[l], gnw[l])
        h = _outproj(h, mf, mn, mg, w_out, mod, l, norm_f.reshape(1, D_MODEL), final)
    return h
```

```python
import functools

import numpy as np
import jax
import jax.numpy as jnp
from jax import lax
from jax.experimental import pallas as pl
from jax.experimental.pallas import tpu as pltpu

F32 = jnp.float32
BF16 = jnp.bfloat16

D_MODEL = 1024
SEQ = 2048
CTX_LEN = 256
T_ALL = SEQ + CTX_LEN
GRID_W = 64
GRID_H = SEQ // GRID_W
EPS = 1e-6

FOUR_GROUPS, FOUR_DIM = 4, 64
D_FOUR = FOUR_GROUPS * FOUR_DIM
NA_HEADS, NA_HEAD_DIM = 6, 64
D_NA = NA_HEADS * NA_HEAD_DIM
NA_KH, NA_KW = 8, 16
GLA_HEADS, GLA_DK, GLA_DV = 4, 48, 96
D_GLA_K = GLA_HEADS * GLA_DK
D_GLA_V = GLA_HEADS * GLA_DV
GLA_RANK = 16
GLA_TAU = 16.0
GLA_CHUNK = 64
ROPE_BASE = 10000.0
LOG2E = 1.4426950408889634

IN_SIZES = (D_FOUR, D_FOUR, D_NA, D_NA, D_NA, D_NA, D_GLA_K, D_GLA_K, D_GLA_V, D_GLA_V,
            GLA_RANK, GLA_RANK)
D_IN = sum(IN_SIZES)

LANE = 128
GLA_KP = 2 * LANE
Z_PAD = LANE
N_CHUNKS_X = SEQ // GLA_CHUNK
N_CHUNKS_C = CTX_LEN // GLA_CHUNK
GLA_TILE = 256

NA_QROWS = 4
NA_KROWS = NA_QROWS + NA_KH
NA_QB = NA_QROWS * GRID_W
NA_KB = NA_KROWS * GRID_W
NA_NBLK = GRID_H // NA_QROWS
NA_VT_PAIR = LANE + 16
NA_VT_ROWS = NA_HEADS // 2 * NA_VT_PAIR

TOK_TILE = 768
VMEM_LIMIT = 56 * 1024 * 1024

COL_FOUR = 0
COL_NA = COL_FOUR + 2 * D_FOUR
COL_GQ = COL_NA + 3 * D_NA
COL_GK = COL_GQ + GLA_KP
COL_GV = COL_GK + GLA_KP
COL_GG = COL_GV + D_GLA_V
COL_Z = COL_GG + D_GLA_V
N_COLS = COL_Z + Z_PAD


def _dot(a, b):
    return jnp.dot(a, b, preferred_element_type=F32)


def _dot_nt(a, b):
    return lax.dot_general(a, b, (((1,), (1,)), ((), ())), preferred_element_type=F32)


def _silu(x):
    return x / (1.0 + jnp.exp(-x))


def _reduce_rows(x, op):
    r, n = x.shape
    part = op(x.reshape(r // 8, 8, n), axis=0)
    return op(part, axis=0, keepdims=True)


def _split_bf16(x):
    hi = x.astype(BF16)
    lo = (x - hi.astype(F32)).astype(BF16)
    return hi, lo


def _qk_pieces():
    n_freq = GLA_DK // 4
    return [[h * GLA_DK + half * 2 * n_freq + part * n_freq
             for h in range(GLA_HEADS) for half in range(2)] for part in range(2)]


def _rope_tables():
    n_freq = GLA_DK // 4
    inv = ROPE_BASE ** (-np.arange(n_freq, dtype=np.float64) / n_freq)
    pos = np.arange(SEQ)
    prow, pcol = pos // GRID_W, pos % GRID_W
    cos = np.ones((T_ALL, LANE), np.float64)
    sin = np.zeros((T_ALL, LANE), np.float64)
    for h in range(GLA_HEADS):
        for half, p in enumerate((prow, pcol)):
            ang = p[:, None] * inv[None, :]
            lo = h * 2 * n_freq + half * n_freq
            cos[:SEQ, lo:lo + n_freq] = np.cos(ang)
            sin[:SEQ, lo:lo + n_freq] = np.sin(ang)
    return cos.astype(np.float32), sin.astype(np.float32)


def _dft_tables(n):
    idx = (np.arange(n)[:, None] * np.arange(n)[None, :]) % n
    ang = 2.0 * np.pi * idx / n
    return np.cos(ang) / np.sqrt(n), np.sin(ang) / np.sqrt(n)


def _block_diag(m, reps):
    n = m.shape[0]
    out = np.zeros((n * reps, n * reps), m.dtype)
    for r in range(reps):
        out[r * n:(r + 1) * n, r * n:(r + 1) * n] = m
    return out


def _na_row_offsets():
    out = []
    for r0 in (0, NA_QROWS, GRID_H - NA_QROWS):
        k0 = int(np.clip(r0 - NA_KH // 2, 0, GRID_H - NA_KROWS))
        per_type = []
        for a in range(NA_QROWS):
            r = r0 + a
            rs = int(np.clip(r - NA_KH // 2, 0, GRID_H - NA_KH))
            per_type.append([(k0 + j) - r + NA_KH - 1 if rs <= k0 + j < rs + NA_KH else None
                             for j in range(NA_KROWS)])
        out.append(per_type)
    return out


def _mod_kernel(c_ref, w_ref, b_ref, o_ref):
    c = c_ref[...]
    s_hi, s_lo = _split_bf16(_silu(c))
    w_hi, w_lo = _split_bf16(w_ref[0])
    acc = _dot(s_hi, w_hi) + _dot(s_lo, w_hi) + _dot(s_hi, w_lo)
    o_ref[0] = acc + b_ref[0]


def _modulation(cc, w_ada, b_ada):
    depth = w_ada.shape[0]
    n_mod = w_ada.shape[2]
    tile = D_MODEL
    return pl.pallas_call(
        _mod_kernel,
        grid=(depth, n_mod // tile),
        in_specs=[
            pl.BlockSpec((16, D_MODEL), lambda l, j: (0, 0)),
            pl.BlockSpec((1, D_MODEL, tile), lambda l, j: (l, 0, j)),
            pl.BlockSpec((1, 1, tile), lambda l, j: (l, 0, j)),
        ],
        out_specs=pl.BlockSpec((1, 16, tile), lambda l, j: (l, 0, j)),
        out_shape=jax.ShapeDtypeStruct((depth, 16, n_mod), F32),
        compiler_params=pltpu.CompilerParams(
            dimension_semantics=("arbitrary", "arbitrary"), vmem_limit_bytes=VMEM_LIMIT),
        name="adaln_mod",
    )(cc, w_ada, b_ada.reshape(depth, 1, n_mod))


def _mod_rows(mod_ref, b, row0, n_rows, col0):
    rows = row0 + lax.broadcasted_iota(jnp.int32, (n_rows, 1), 0)
    m_x = mod_ref[0, pl.ds(b, 1), col0:col0 + D_MODEL]
    m_c = mod_ref[0, 8:9, col0:col0 + D_MODEL]
    return jnp.where(rows >= SEQ, m_c, m_x)


def _fill_weight(w_ref, w_scr, dst0, src0, width, scale=None):
    rows = w_scr.shape[0]
    for c in range(0, width, LANE):
        n = min(LANE, width - c)
        blk = w_ref[0, :, src0 + c:src0 + c + n]
        if scale is not None:
            blk = blk * scale
        if n < LANE:
            blk = jnp.concatenate([blk, jnp.zeros((rows, LANE - n), F32)], axis=1)
        w_scr[:, dst0 + c:dst0 + c + LANE] = blk.astype(BF16)


def _inproj_kernel(h_ref, mod_ref, nw_ref, w_ref, wvt_ref, cos_ref, sin_ref,
                   four_ref, nq_ref, nk_ref, nvt_ref, ng_ref,
                   gq_ref, gk_ref, gv_ref, gg_ref, gz_ref, hx_scr, w_scr):
    b = pl.program_id(0)
    j = pl.program_id(1)

    @pl.when((b == 0) & (j == 0))
    def _():
        src = np.concatenate([[0], np.cumsum(IN_SIZES)])
        _fill_weight(w_ref, w_scr, COL_FOUR, src[0], 2 * D_FOUR)
        _fill_weight(w_ref, w_scr, COL_NA, src[2], D_NA, NA_HEAD_DIM ** -0.5 * LOG2E)
        _fill_weight(w_ref, w_scr, COL_NA + D_NA, src[3], D_NA)
        _fill_weight(w_ref, w_scr, COL_NA + 2 * D_NA, src[5], D_NA)
        n_freq = GLA_DK // 4
        pad = jnp.zeros((D_MODEL, LANE - GLA_HEADS * 2 * n_freq), F32)
        for dst0, src0 in ((COL_GQ, src[6]), (COL_GK, src[7])):
            for part, offs in enumerate(_qk_pieces()):
                tile = jnp.concatenate(
                    [w_ref[0, :, src0 + o:src0 + o + n_freq] for o in offs] + [pad], axis=1)
                w_scr[:, dst0 + part * LANE:dst0 + (part + 1) * LANE] = tile.astype(BF16)
        _fill_weight(w_ref, w_scr, COL_GV, src[8], D_GLA_V)
        _fill_weight(w_ref, w_scr, COL_GG, src[9], D_GLA_V)
        _fill_weight(w_ref, w_scr, COL_Z, src[10], 2 * GLA_RANK)

    x = h_ref[0]
    ms = jnp.mean(x * x, axis=-1, keepdims=True)
    y = x * lax.rsqrt(ms + EPS) * nw_ref[0]
    shift = _mod_rows(mod_ref, b, j * TOK_TILE, TOK_TILE, 0)
    scale = _mod_rows(mod_ref, b, j * TOK_TILE, TOK_TILE, D_MODEL)
    hx_scr[...] = (y * (1.0 + scale) + shift).astype(BF16)

    def proj(c0, width):
        return _dot(hx_scr[...], w_scr[:, c0:c0 + width])

    four_ref[0] = proj(COL_FOUR, 2 * D_FOUR).astype(BF16)
    nq_ref[0] = proj(COL_NA, D_NA).astype(BF16)
    nk_ref[0] = proj(COL_NA + D_NA, D_NA).astype(BF16)
    ones_rows = lax.broadcasted_iota(jnp.int32, (NA_VT_ROWS, 1), 0) % NA_VT_PAIR >= LANE
    nvt_ref[0] = (_dot_nt(wvt_ref[0], hx_scr[...]) + jnp.where(ones_rows, 1.0, 0.0)).astype(BF16)
    ng_ref[0] = proj(COL_NA + 2 * D_NA, D_NA).astype(BF16)
    gv_ref[0] = proj(COL_GV, D_GLA_V).astype(BF16)
    gg_ref[0] = proj(COL_GG, D_GLA_V).astype(BF16)
    gz_ref[0] = proj(COL_Z, Z_PAD)

    cos = cos_ref[...]
    sin = sin_ref[...]

    def rope(c0, out_ref, scale_out):
        u = proj(c0, GLA_KP)
        u1, u2 = u[:, :LANE], u[:, LANE:]
        out_ref[0, :, :LANE] = (u1 * cos - u2 * sin) * scale_out
        out_ref[0, :, LANE:] = (u1 * sin + u2 * cos) * scale_out

    rope(COL_GQ, gq_ref, GLA_DK ** -0.5)
    rope(COL_GK, gk_ref, 1.0)


def _inproj(h, mod, l, norm_w, w_in, w_nvt, cos_t, sin_t):
    nb = h.shape[0]
    nt = T_ALL // TOK_TILE
    tok = lambda width: pl.BlockSpec((1, TOK_TILE, width), lambda b, j: (b, j, 0))
    layer = lambda a: pl.BlockSpec((1,) + a.shape[1:], lambda b, j: (l,) + (0,) * (a.ndim - 1))
    out_widths = [(2 * D_FOUR, BF16), (D_NA, BF16), (D_NA, BF16), (None, BF16), (D_NA, BF16),
                  (GLA_KP, F32), (GLA_KP, F32), (D_GLA_V, BF16), (D_GLA_V, BF16), (Z_PAD, F32)]
    vt_spec = pl.BlockSpec((1, NA_VT_ROWS, TOK_TILE), lambda b, j: (b, 0, j))
    vt_shape = jax.ShapeDtypeStruct((nb, NA_VT_ROWS, T_ALL), BF16)
    return pl.pallas_call(
        _inproj_kernel,
        grid=(nb, nt),
        in_specs=[
            tok(D_MODEL),
            layer(mod),
            layer(norm_w),
            layer(w_in),
            layer(w_nvt),
            pl.BlockSpec((TOK_TILE, LANE), lambda b, j: (j, 0)),
            pl.BlockSpec((TOK_TILE, LANE), lambda b, j: (j, 0)),
        ],
        out_specs=[vt_spec if w is None else tok(w) for w, _ in out_widths],
        out_shape=[vt_shape if w is None else jax.ShapeDtypeStruct((nb, T_ALL, w), dt)
                   for w, dt in out_widths],
        scratch_shapes=[pltpu.VMEM((TOK_TILE, D_MODEL), BF16), pltpu.VMEM((D_MODEL, N_COLS), BF16)],
        compiler_params=pltpu.CompilerParams(
            dimension_semantics=("arbitrary", "arbitrary"), vmem_limit_bytes=VMEM_LIMIT),
        name="inproj",
    )(h, mod, norm_w, w_in, w_nvt, cos_t, sin_t)


def _fourier_kernel(p_ref, cbd_ref, sbd_ref, cl_ref, sl_ref, cc_ref, sc_ref, wf_ref, o_ref,
                    uc_scr, us_scr):
    u = p_ref[0, :, :D_FOUR]
    uc_scr[...] = _dot(u, cbd_ref[...]).astype(BF16)
    us_scr[...] = _dot(u, sbd_ref[...]).astype(BF16)

    def finish(f, r0, n):
        four = _dot(f.astype(BF16), wf_ref[...])
        gate = p_ref[0, pl.ds(r0, n), D_FOUR:].astype(F32)
        o_ref[0, pl.ds(r0, n), :] = (four * _silu(gate)).astype(BF16)

    tile = 256

    def body(i, carry):
        r0 = pl.multiple_of(i * tile, tile)
        f = (_dot(cl_ref[pl.ds(r0, tile), :], uc_scr[:SEQ, :])
             - _dot(sl_ref[pl.ds(r0, tile), :], us_scr[:SEQ, :]))
        finish(f, r0, tile)
        return carry

    lax.fori_loop(0, SEQ // tile, body, 0)
    f_c = _dot(cc_ref[...], uc_scr[SEQ:, :]) - _dot(sc_ref[...], us_scr[SEQ:, :])
    finish(f_c, SEQ, CTX_LEN)


def _fourier(p_four, tabs, wf):
    nb = p_four.shape[0]
    full = lambda a: pl.BlockSpec(a.shape, lambda b: (0,) * a.ndim)
    return pl.pallas_call(
        _fourier_kernel,
        grid=(nb,),
        in_specs=[pl.BlockSpec((1, T_ALL, 2 * D_FOUR), lambda b: (b, 0, 0))]
        + [full(t) for t in tabs] + [full(wf)],
        out_specs=pl.BlockSpec((1, T_ALL, D_FOUR), lambda b: (b, 0, 0)),
        out_shape=jax.ShapeDtypeStruct((nb, T_ALL, D_FOUR), BF16),
        scratch_shapes=[pltpu.VMEM((T_ALL, D_FOUR), BF16), pltpu.VMEM((T_ALL, D_FOUR), BF16)],
        compiler_params=pltpu.CompilerParams(
            dimension_semantics=("arbitrary",), vmem_limit_bytes=VMEM_LIMIT),
        name="fourier_mix",
    )(p_four, *tabs, wf)


def _na_build_bias(ra_ref, rb_ref, tbl_ref):
    w = GRID_W
    ck = lax.broadcasted_iota(jnp.int32, (w, 2 * w), 0)
    ln = lax.broadcasted_iota(jnp.int32, (w, 2 * w), 1)
    cq = ln % w
    cs = jnp.clip(cq - NA_KW // 2, 0, w - NA_KW)
    col_ok = (ck >= cs) & (ck < cs + NA_KW)
    left = ln < w
    neg = jnp.full((w, 2 * w), -jnp.inf, F32)
    offs = _na_row_offsets()
    cache = {}
    for typ in range(3):
        for ap in range(NA_QROWS // 2):
            for jr in range(NA_KROWS):
                d0 = offs[typ][2 * ap][jr]
                d1 = offs[typ][2 * ap + 1][jr]
                for h in range(NA_HEADS):
                    key = (h, d0, d1)
                    if key not in cache:
                        if d0 is None and d1 is None:
                            cache[key] = neg
                        else:
                            ok = col_ok
                            if d0 is None:
                                row = rb_ref[h, d1:d1 + 1, :]
                                ok = ok & jnp.logical_not(left)
                            elif d1 is None:
                                row = ra_ref[h, d0:d0 + 1, :]
                                ok = ok & left
                            else:
                                row = ra_ref[h, d0:d0 + 1, :] + rb_ref[h, d1:d1 + 1, :]
                            skew = pltpu.roll(jnp.broadcast_to(row * LOG2E, (w, 2 * w)),
                                              2 * w - (NA_KW - 1), 1, stride=1, stride_axis=0)
                            cache[key] = jnp.where(ok, skew, neg)
                    tbl_ref[h * 3 + typ, jr * w:(jr + 1) * w, ap * 2 * w:(ap + 1) * 2 * w] = cache[key]


def _na_kernel(q_ref, k_ref, vt_ref, g_ref, ra_ref, rb_ref, o_ref, bias_ref):
    j = pl.program_id(1)
    lane = lax.broadcasted_iota(jnp.int32, (1, LANE), 1)
    first_head_lanes = lane < NA_HEAD_DIM
    first_head_rows = lax.broadcasted_iota(jnp.int32, (LANE, 1), 0) < NA_HEAD_DIM

    @pl.when((pl.program_id(0) == 0) & (j == 0))
    def _():
        _na_build_bias(ra_ref, rb_ref, bias_ref)

    def attend(local):
        if local:
            k0 = pl.multiple_of(jnp.clip(j * NA_QROWS - NA_KH // 2, 0, GRID_H - NA_KROWS) * GRID_W,
                                NA_QB)
            typ = jnp.where(j == 0, 0, jnp.where(j == NA_NBLK - 1, 2, 1))
        heads = range(NA_HEADS)
        cols = [slice(h // 2 * LANE, (h // 2 + 1) * LANE) for h in heads]
        s_c, s_l = [], []
        for h in heads:
            qp = q_ref[0, :, cols[h]]
            keep = first_head_lanes if h % 2 == 0 else jnp.logical_not(first_head_lanes)
            qm = jnp.where(keep, qp, jnp.zeros_like(qp))
            s_c.append(_dot_nt(k_ref[0, SEQ:, cols[h]], qm))
            if local:
                s_l.append(_dot_nt(k_ref[0, pl.ds(k0, NA_KB), cols[h]], qm) + bias_ref[h * 3 + typ])
        e_c, e_l = [], []
        for h in heads:
            m = _reduce_rows(s_c[h], jnp.max)
            if local:
                m = jnp.maximum(m, _reduce_rows(s_l[h], jnp.max))
            e_c.append(jnp.exp2((s_c[h] - m).astype(BF16)))
            if local:
                e_l.append(jnp.exp2((s_l[h] - m).astype(BF16)))
        for p in range(NA_HEADS // 2):
            vrows = slice(p * NA_VT_PAIR, (p + 1) * NA_VT_PAIR)
            acc = None
            for h in (2 * p, 2 * p + 1):
                o = _dot(vt_ref[0, vrows, SEQ:], e_c[h])
                if local:
                    o = o + _dot(vt_ref[0, vrows, pl.ds(k0, NA_KB)], e_l[h])
                o = o[:LANE] / o[LANE:LANE + 1]
                acc = o if h % 2 == 0 else jnp.where(first_head_rows, acc, o)
            gate = g_ref[0, :, cols[2 * p]].astype(F32)
            o_ref[0, :, cols[2 * p]] = (acc.T * _silu(gate)).astype(BF16)

    @pl.when(j < NA_NBLK)
    def _():
        attend(True)

    @pl.when(j == NA_NBLK)
    def _():
        attend(False)


def _na(nq, nk, nvt, ng, rpb_a, rpb_b):
    nb = nq.shape[0]
    blk = pl.BlockSpec((1, NA_QB, D_NA), lambda b, j: (b, j, 0))
    seq = pl.BlockSpec((1, T_ALL, D_NA), lambda b, j: (b, 0, 0))
    seq_t = pl.BlockSpec((1, NA_VT_ROWS, T_ALL), lambda b, j: (b, 0, 0))
    tab = pl.BlockSpec(rpb_a.shape, lambda b, j: (0, 0, 0))
    return pl.pallas_call(
        _na_kernel,
        grid=(nb, NA_NBLK + 1),
        in_specs=[blk, seq, seq_t, blk, tab, tab],
        out_specs=blk,
        out_shape=jax.ShapeDtypeStruct((nb, T_ALL, D_NA), BF16),
        scratch_shapes=[pltpu.VMEM((NA_HEADS * 3, NA_KB, NA_QB), F32)],
        compiler_params=pltpu.CompilerParams(
            dimension_semantics=("arbitrary", "arbitrary"), vmem_limit_bytes=VMEM_LIMIT),
        name="nbr_attention",
    )(nq, nk, nvt, ng, rpb_a, rpb_b)


def _gla_kernel(q_ref, k_ref, v_ref, g_ref, z_ref, wa_ref, ba_ref, nw_ref, o_ref,
                st_scr, o_scr, qd_scr, ke_scr, vt_scr, dec_scr):
    C = GLA_CHUNK
    TR = GLA_TILE
    n_sub = TR // C
    kl = lax.broadcasted_iota(jnp.int32, (1, GLA_KP), 1) % LANE
    k_head = jnp.where(kl < GLA_HEADS * (GLA_DK // 2), kl // (GLA_DK // 2), GLA_HEADS)
    v_head = lax.broadcasted_iota(jnp.int32, (1, D_GLA_V), 1) // GLA_DV
    st_mask = (lax.broadcasted_iota(jnp.int32, (D_GLA_V, 1), 0) // GLA_DV) == k_head
    ti4 = lax.broadcasted_iota(jnp.int32, (C, GLA_HEADS * C), 0)
    si4 = lax.broadcasted_iota(jnp.int32, (C, GLA_HEADS * C), 1) % C
    tr = lax.broadcasted_iota(jnp.int32, (TR, TR), 0)
    sr = lax.broadcasted_iota(jnp.int32, (TR, TR), 1)
    same_chunk = (tr // C) == (sr // C)
    tri = [jnp.where(same_chunk & (sr <= tr), 1.0, 0.0).astype(BF16),
           jnp.where(same_chunk & (sr >= tr), 1.0, 0.0).astype(BF16)]

    def tile_body(i, carry):
        r0 = pl.multiple_of(i * TR, TR)
        rows = pl.ds(r0, TR)
        z = z_ref[0, rows, :].astype(BF16)
        g = _dot(z, wa_ref[...]) + ba_ref[...]
        la = (jnp.minimum(g, 0.0) - jnp.log(1.0 + jnp.exp(-jnp.abs(g)))) * (1.0 / GLA_TAU)
        q = q_ref[0, rows, :]
        k = k_ref[0, rows, :]
        v = v_ref[0, rows, :]
        v_bd = []
        for c in range(n_sub):
            v_c = v[c * C:(c + 1) * C]
            v_bd.append(jnp.concatenate(
                [jnp.where(v_head == h, v_c, jnp.zeros_like(v_c)) for h in range(GLA_HEADS)], axis=0))
            vt_scr[i * n_sub + c] = v_c.astype(F32).T.astype(BF16)
        scores = []
        for d in range(2):
            fwd = d == 0
            la_hi, la_lo = _split_bf16(la[:, d * GLA_KP:(d + 1) * GLA_KP])
            bcum = _dot(tri[d], la_hi) + _dot(tri[d], la_lo)
            ends = [bcum[(c + 1) * C - 1:(c + 1) * C] if fwd else bcum[c * C:c * C + 1]
                    for c in range(n_sub)]
            b_end = jnp.concatenate([jnp.broadcast_to(e, (C, GLA_KP)) for e in ends], axis=0)
            q_dec = (q * jnp.exp(bcum)).astype(BF16)
            k_dec = k * jnp.exp(-bcum)
            qd_scr[d, rows, :] = q_dec
            ke_scr[d, rows, :] = (k * jnp.exp(b_end - bcum)).astype(BF16)
            for c in range(n_sub):
                dec_scr[d, pl.ds(i * n_sub + c, 1), :] = jnp.exp(ends[c])
                k_c = k_dec[c * C:(c + 1) * C]
                k_bd = jnp.concatenate(
                    [jnp.where(k_head == h, k_c, 0.0) for h in range(GLA_HEADS)], axis=0).astype(BF16)
                scores.append(_dot_nt(q_dec[c * C:(c + 1) * C], k_bd))
        for d in range(2):
            causal = (si4 <= ti4) if d == 0 else (si4 >= ti4)
            for c in range(n_sub):
                a = jnp.where(causal, scores[d * n_sub + c], 0.0).astype(BF16)
                o_scr[d, pl.ds(r0 + c * C, C), :] = _dot(a, v_bd[c])
        return carry

    lax.fori_loop(0, T_ALL // TR, tile_body, 0, unroll=3)

    st_scr[...] = jnp.zeros_like(st_scr)
    n_all = N_CHUNKS_C + N_CHUNKS_X

    def step(i, carry):
        in_ctx = i < N_CHUNKS_C
        idx_f = jnp.where(in_ctx, N_CHUNKS_X + i, i - N_CHUNKS_C)
        idx_b = jnp.where(in_ctx, N_CHUNKS_X + N_CHUNKS_C - 1 - i, n_all - 1 - i)
        for d, idx in ((0, idx_f), (1, idx_b)):
            off = pl.ds(pl.multiple_of(idx * C, C), C)
            st = st_scr[d]
            o_scr[d, off, :] += _dot_nt(qd_scr[d, off, :], st.astype(BF16))
            half = D_GLA_V // 2
            ke = ke_scr[d, off, :]
            upd = jnp.concatenate([_dot(vt_scr[idx, :half, :], ke), _dot(vt_scr[idx, half:, :], ke)], axis=0)
            st_scr[d] = st * dec_scr[d, pl.ds(idx, 1), :] + jnp.where(st_mask, upd, 0.0)
        return carry

    lax.fori_loop(0, n_all, step, 0, unroll=4)

    head_ones = jnp.where(
        (lax.broadcasted_iota(jnp.int32, (D_GLA_V, 1), 0) // GLA_DV) == v_head, 1.0, 0.0).astype(BF16)
    tile = 256

    def fin(i, carry):
        r0 = pl.multiple_of(i * tile, tile)
        o = o_scr[0, pl.ds(r0, tile), :] + o_scr[1, pl.ds(r0, tile), :]
        sq_hi, sq_lo = _split_bf16(o * o)
        ms = (_dot(sq_hi, head_ones) + _dot(sq_lo, head_ones)) * (1.0 / GLA_DV)
        y = o * lax.rsqrt(ms + EPS) * nw_ref[...]
        gate = g_ref[0, pl.ds(r0, tile), :].astype(F32)
        o_ref[0, pl.ds(r0, tile), :] = (y * _silu(gate)).astype(BF16)
        return carry

    lax.fori_loop(0, T_ALL // tile, fin, 0)


def _gla(gq, gk, gv, gg, gz, wa, ba, gnw):
    nb = gq.shape[0]
    seq = lambda w: pl.BlockSpec((1, T_ALL, w), lambda b: (b, 0, 0))
    full = lambda a: pl.BlockSpec(a.shape, lambda b: (0,) * a.ndim)
    return pl.pallas_call(
        _gla_kernel,
        grid=(nb,),
        in_specs=[seq(GLA_KP), seq(GLA_KP), seq(D_GLA_V), seq(D_GLA_V), seq(Z_PAD),
                  full(wa), full(ba), full(gnw)],
        out_specs=seq(D_GLA_V),
        out_shape=jax.ShapeDtypeStruct((nb, T_ALL, D_GLA_V), BF16),
        scratch_shapes=[pltpu.VMEM((2, D_GLA_V, GLA_KP), F32),
                        pltpu.VMEM((2, T_ALL, D_GLA_V), F32),
                        pltpu.VMEM((2, T_ALL, GLA_KP), BF16),
                        pltpu.VMEM((2, T_ALL, GLA_KP), BF16),
                        pltpu.VMEM((T_ALL // GLA_CHUNK, D_GLA_V, GLA_CHUNK), BF16),
                        pltpu.VMEM((2, T_ALL // GLA_CHUNK, GLA_KP), F32)],

        compiler_params=pltpu.CompilerParams(
            dimension_semantics=("arbitrary",), vmem_limit_bytes=VMEM_LIMIT),
        name="gla_mix",
    )(gq, gk, gv, gg, gz, wa, ba, gnw)


def _outproj_kernel(h_ref, mf_ref, mn_ref, mg_ref, w_ref, mod_ref, nf_ref, o_ref, w_scr, *, tile, final):
    b = pl.program_id(0)
    j = pl.program_id(1)

    @pl.when((b == 0) & (j == 0))
    def _():
        for r in range(0, D_MODEL, 256):
            w_scr[r:r + 256, :] = w_ref[0, r:r + 256, :].astype(BF16)

    acc = (_dot(mf_ref[0], w_scr[:D_FOUR, :])
           + _dot(mn_ref[0], w_scr[D_FOUR:D_FOUR + D_NA, :])
           + _dot(mg_ref[0], w_scr[D_FOUR + D_NA:, :]))
    gate = _mod_rows(mod_ref, b, j * tile, tile, 2 * D_MODEL)
    x = h_ref[0] + gate * acc
    if final:
        ms = jnp.mean(x * x, axis=-1, keepdims=True)
        x = x * lax.rsqrt(ms + EPS) * nf_ref[...]
    o_ref[0] = x


def _outproj(h, mf, mn, mg, w_out, mod, l, nf, final):
    nb = h.shape[0]
    tile = 512 if final else TOK_TILE
    rows = SEQ if final else T_ALL
    tok = lambda width: pl.BlockSpec((1, tile, width), lambda b, j: (b, j, 0))
    layer = lambda a: pl.BlockSpec((1,) + a.shape[1:], lambda b, j: (l,) + (0,) * (a.ndim - 1))
    return pl.pallas_call(
        functools.partial(_outproj_kernel, tile=tile, final=final),
        grid=(nb, rows // tile),
        in_specs=[tok(D_MODEL), tok(D_FOUR), tok(D_NA), tok(D_GLA_V),
                  layer(w_out), layer(mod), pl.BlockSpec((1, D_MODEL), lambda b, j: (0, 0))],
        out_specs=tok(D_MODEL),
        out_shape=jax.ShapeDtypeStruct((nb, rows, D_MODEL), F32),
        scratch_shapes=[pltpu.VMEM((D_MODEL, D_MODEL), BF16)],
        compiler_params=pltpu.CompilerParams(
            dimension_semantics=("arbitrary", "arbitrary"), vmem_limit_bytes=VMEM_LIMIT),
        name="outproj_final" if final else "outproj",
    )(h, mf, mn, mg, w_out, mod, nf)


def _decay_weights(w_f, b_f, w_b, b_b):
    n_freq = GLA_DK // 4
    lane_pad = lambda a: jnp.zeros(a.shape[:-1] + (LANE - GLA_HEADS * 2 * n_freq,), a.dtype)
    lanes = lambda a: jnp.concatenate(
        [blk for offs in _qk_pieces() for blk in [a[..., o:o + n_freq] for o in offs] + [lane_pad(a)]],
        axis=-1)
    zero = jnp.zeros_like(lanes(w_f))
    top = jnp.concatenate([lanes(w_f), zero], axis=-1)
    bot = jnp.concatenate([zero, lanes(w_b)], axis=-1)
    pad = jnp.zeros((w_f.shape[0], Z_PAD - 2 * GLA_RANK, 2 * GLA_KP), w_f.dtype)
    wa = jnp.concatenate([top, bot, pad], axis=1).astype(BF16)
    ba = jnp.concatenate([lanes(b_f), lanes(b_b)], axis=-1)[:, None, :]
    return wa, ba


def kernel(x, c, ctx, c_ctx, w_ada, b_ada, norm_w, w_in, w_four, rpb, w_alpha_fwd, b_alpha_fwd,
           w_alpha_bwd, b_alpha_bwd, gla_norm_w, w_out, norm_f):
    nb = x.shape[0]
    depth = w_in.shape[0]
    assert x.shape == (nb, SEQ, D_MODEL) and ctx.shape == (nb, CTX_LEN, D_MODEL) and nb <= 8

    v0 = sum(IN_SIZES[:4])
    w_nvt = jnp.swapaxes(w_in[:, :, v0:v0 + D_NA], 1, 2).astype(BF16)
    w_nvt = jnp.pad(w_nvt.reshape(depth, NA_HEADS // 2, LANE, D_MODEL),
                    ((0, 0), (0, 0), (0, NA_VT_PAIR - LANE), (0, 0))).reshape(depth, NA_VT_ROWS, D_MODEL)
    w_four_b = w_four.astype(BF16)
    norm_w3 = norm_w.reshape(depth, 1, D_MODEL)
    wa, ba = _decay_weights(w_alpha_fwd, b_alpha_fwd, w_alpha_bwd, b_alpha_bwd)
    gnw = jnp.tile(gla_norm_w, (1, GLA_HEADS)).reshape(depth, 1, D_GLA_V)
    n_off = 2 * NA_KW - 1
    rpb_rev = rpb[..., ::-1]
    rpb_a = jnp.pad(rpb_rev, ((0, 0), (0, 0), (0, 1), (0, LANE - n_off)))
    rpb_b = jnp.pad(rpb_rev, ((0, 0), (0, 0), (0, 1), (GRID_W, LANE - GRID_W - n_off)))

    cos_t, sin_t = (jnp.asarray(t) for t in _rope_tables())
    c64, s64 = _dft_tables(FOUR_DIM)
    cl, sl = _dft_tables(SEQ)
    ccx, scx = _dft_tables(CTX_LEN)
    tabs = [jnp.asarray(t, dtype=F32).astype(BF16) for t in
            (_block_diag(c64, FOUR_GROUPS), _block_diag(s64, FOUR_GROUPS), cl, sl, ccx, scx)]

    cc = jnp.zeros((16, D_MODEL), F32).at[:nb].set(c).at[8].set(c_ctx)
    mod = _modulation(cc, w_ada, b_ada)

    h = jnp.concatenate([x, ctx], axis=1)
    for l in range(depth):
        final = l == depth - 1
        (p_four, nq, nk, nvt, ng, gq, gk, gv, gg, gz) = _inproj(
            h, mod, l, norm_w3, w_in, w_nvt, cos_t, sin_t)
        mf = _fourier(p_four, tabs, w_four_b[l])
        mn = _na(nq, nk, nvt, ng, rpb_a[l], rpb_b[l])
        mg = _gla(gq, gk, gv, gg, gz, wa[l], ba[l], gnw[l])
        h = _outproj(h, mf, mn, mg, w_out, mod, l, norm_f.reshape(1, D_MODEL), final)
    return h
```

```python
import functools

import numpy as np
import jax
import jax.numpy as jnp
from jax import lax
from jax.experimental import pallas as pl
from jax.experimental.pallas import tpu as pltpu

F32 = jnp.float32
BF16 = jnp.bfloat16

D_MODEL = 1024
SEQ = 2048
CTX_LEN = 256
T_ALL = SEQ + CTX_LEN
GRID_W = 64
GRID_H = SEQ // GRID_W
EPS = 1e-6

FOUR_GROUPS, FOUR_DIM = 4, 64
D_FOUR = FOUR_GROUPS * FOUR_DIM
NA_HEADS, NA_HEAD_DIM = 6, 64
D_NA = NA_HEADS * NA_HEAD_DIM
NA_KH, NA_KW = 8, 16
GLA_HEADS, GLA_DK, GLA_DV = 4, 48, 96
D_GLA_K = GLA_HEADS * GLA_DK
D_GLA_V = GLA_HEADS * GLA_DV
GLA_RANK = 16
GLA_TAU = 16.0
GLA_CHUNK = 64
ROPE_BASE = 10000.0
LOG2E = 1.4426950408889634

IN_SIZES = (D_FOUR, D_FOUR, D_NA, D_NA, D_NA, D_NA, D_GLA_K, D_GLA_K, D_GLA_V, D_GLA_V,
            GLA_RANK, GLA_RANK)
D_IN = sum(IN_SIZES)

LANE = 128
GLA_KP = 2 * LANE
Z_PAD = LANE
N_CHUNKS_X = SEQ // GLA_CHUNK
N_CHUNKS_C = CTX_LEN // GLA_CHUNK
GLA_TILE = 256

NA_QROWS = 4
NA_KROWS = NA_QROWS + NA_KH
NA_QB = NA_QROWS * GRID_W
NA_KB = NA_KROWS * GRID_W
NA_NBLK = GRID_H // NA_QROWS
NA_VT_PAIR = LANE + 16
NA_VT_ROWS = NA_HEADS // 2 * NA_VT_PAIR

TOK_TILE = 768
VMEM_LIMIT = 56 * 1024 * 1024

COL_FOUR = 0
COL_NA = COL_FOUR + 2 * D_FOUR
COL_GQ = COL_NA + 3 * D_NA
COL_GK = COL_GQ + GLA_KP
COL_GV = COL_GK + GLA_KP
COL_GG = COL_GV + D_GLA_V
COL_Z = COL_GG + D_GLA_V
N_COLS = COL_Z + Z_PAD


def _dot(a, b):
    return jnp.dot(a, b, preferred_element_type=F32)


def _dot_nt(a, b):
    return lax.dot_general(a, b, (((1,), (1,)), ((), ())), preferred_element_type=F32)


def _silu(x):
    return x / (1.0 + jnp.exp(-x))


def _reduce_rows(x, op):
    r, n = x.shape
    part = op(x.reshape(r // 8, 8, n), axis=0)
    return op(part, axis=0, keepdims=True)


def _split_bf16(x):
    hi = x.astype(BF16)
    lo = (x - hi.astype(F32)).astype(BF16)
    return hi, lo


def _qk_pieces():
    n_freq = GLA_DK // 4
    return [[h * GLA_DK + half * 2 * n_freq + part * n_freq
             for h in range(GLA_HEADS) for half in range(2)] for part in range(2)]


def _rope_tables():
    n_freq = GLA_DK // 4
    inv = ROPE_BASE ** (-np.arange(n_freq, dtype=np.float64) / n_freq)
    pos = np.arange(SEQ)
    prow, pcol = pos // GRID_W, pos % GRID_W
    cos = np.ones((T_ALL, LANE), np.float64)
    sin = np.zeros((T_ALL, LANE), np.float64)
    for h in range(GLA_HEADS):
        for half, p in enumerate((prow, pcol)):
            ang = p[:, None] * inv[None, :]
            lo = h * 2 * n_freq + half * n_freq
            cos[:SEQ, lo:lo + n_freq] = np.cos(ang)
            sin[:SEQ, lo:lo + n_freq] = np.sin(ang)
    return cos.astype(np.float32), sin.astype(np.float32)


def _dft_tables(n):
    idx = (np.arange(n)[:, None] * np.arange(n)[None, :]) % n
    ang = 2.0 * np.pi * idx / n
    return np.cos(ang) / np.sqrt(n), np.sin(ang) / np.sqrt(n)


def _block_diag(m, reps):
    n = m.shape[0]
    out = np.zeros((n * reps, n * reps), m.dtype)
    for r in range(reps):
        out[r * n:(r + 1) * n, r * n:(r + 1) * n] = m
    return out


def _na_row_offsets():
    out = []
    for r0 in (0, NA_QROWS, GRID_H - NA_QROWS):
        k0 = int(np.clip(r0 - NA_KH // 2, 0, GRID_H - NA_KROWS))
        per_type = []
        for a in range(NA_QROWS):
            r = r0 + a
            rs = int(np.clip(r - NA_KH // 2, 0, GRID_H - NA_KH))
            per_type.append([(k0 + j) - r + NA_KH - 1 if rs <= k0 + j < rs + NA_KH else None
                             for j in range(NA_KROWS)])
        out.append(per_type)
    return out


def _mod_kernel(c_ref, w_ref, b_ref, o_ref):
    c = c_ref[...]
    s_hi, s_lo = _split_bf16(_silu(c))
    w_hi, w_lo = _split_bf16(w_ref[0])
    acc = _dot(s_hi, w_hi) + _dot(s_lo, w_hi) + _dot(s_hi, w_lo)
    o_ref[0] = acc + b_ref[0]


def _modulation(cc, w_ada, b_ada):
    depth = w_ada.shape[0]
    n_mod = w_ada.shape[2]
    tile = D_MODEL
    return pl.pallas_call(
        _mod_kernel,
        grid=(depth, n_mod // tile),
        in_specs=[
            pl.BlockSpec((16, D_MODEL), lambda l, j: (0, 0)),
            pl.BlockSpec((1, D_MODEL, tile), lambda l, j: (l, 0, j)),
            pl.BlockSpec((1, 1, tile), lambda l, j: (l, 0, j)),
        ],
        out_specs=pl.BlockSpec((1, 16, tile), lambda l, j: (l, 0, j)),
        out_shape=jax.ShapeDtypeStruct((depth, 16, n_mod), F32),
        compiler_params=pltpu.CompilerParams(
            dimension_semantics=("arbitrary", "arbitrary"), vmem_limit_bytes=VMEM_LIMIT),
        name="adaln_mod",
    )(cc, w_ada, b_ada.reshape(depth, 1, n_mod))


def _mod_rows(mod_ref, b, row0, n_rows, col0):
    rows = row0 + lax.broadcasted_iota(jnp.int32, (n_rows, 1), 0)
    m_x = mod_ref[0, pl.ds(b, 1), col0:col0 + D_MODEL]
    m_c = mod_ref[0, 8:9, col0:col0 + D_MODEL]
    return jnp.where(rows >= SEQ, m_c, m_x)


def _fill_weight(w_ref, w_scr, dst0, src0, width, scale=None):
    rows = w_scr.shape[0]
    for c in range(0, width, LANE):
        n = min(LANE, width - c)
        blk = w_ref[0, :, src0 + c:src0 + c + n]
        if scale is not None:
            blk = blk * scale
        if n < LANE:
            blk = jnp.concatenate([blk, jnp.zeros((rows, LANE - n), F32)], axis=1)
        w_scr[:, dst0 + c:dst0 + c + LANE] = blk.astype(BF16)


def _inproj_kernel(h_ref, mod_ref, nw_ref, w_ref, wvt_ref, cos_ref, sin_ref,
                   four_ref, nq_ref, nk_ref, nvt_ref, ng_ref,
                   gq_ref, gk_ref, gv_ref, gg_ref, gz_ref, hx_scr, w_scr):
    b = pl.program_id(0)
    j = pl.program_id(1)

    @pl.when((b == 0) & (j == 0))
    def _():
        src = np.concatenate([[0], np.cumsum(IN_SIZES)])
        _fill_weight(w_ref, w_scr, COL_FOUR, src[0], 2 * D_FOUR)
        _fill_weight(w_ref, w_scr, COL_NA, src[2], D_NA, NA_HEAD_DIM ** -0.5 * LOG2E)
        _fill_weight(w_ref, w_scr, COL_NA + D_NA, src[3], D_NA)
        _fill_weight(w_ref, w_scr, COL_NA + 2 * D_NA, src[5], D_NA)
        n_freq = GLA_DK // 4
        pad = jnp.zeros((D_MODEL, LANE - GLA_HEADS * 2 * n_freq), F32)
        for dst0, src0 in ((COL_GQ, src[6]), (COL_GK, src[7])):
            for part, offs in enumerate(_qk_pieces()):
                tile = jnp.concatenate(
                    [w_ref[0, :, src0 + o:src0 + o + n_freq] for o in offs] + [pad], axis=1)
                w_scr[:, dst0 + part * LANE:dst0 + (part + 1) * LANE] = tile.astype(BF16)
        _fill_weight(w_ref, w_scr, COL_GV, src[8], D_GLA_V)
        _fill_weight(w_ref, w_scr, COL_GG, src[9], D_GLA_V)
        _fill_weight(w_ref, w_scr, COL_Z, src[10], 2 * GLA_RANK)

    x = h_ref[0]
    ms = jnp.mean(x * x, axis=-1, keepdims=True)
    y = x * lax.rsqrt(ms + EPS) * nw_ref[0]
    shift = _mod_rows(mod_ref, b, j * TOK_TILE, TOK_TILE, 0)
    scale = _mod_rows(mod_ref, b, j * TOK_TILE, TOK_TILE, D_MODEL)
    hx_scr[...] = (y * (1.0 + scale) + shift).astype(BF16)

    def proj(c0, width):
        return _dot(hx_scr[...], w_scr[:, c0:c0 + width])

    four_ref[0] = proj(COL_FOUR, 2 * D_FOUR).astype(BF16)
    nq_ref[0] = proj(COL_NA, D_NA).astype(BF16)
    nk_ref[0] = proj(COL_NA + D_NA, D_NA).astype(BF16)
    ones_rows = lax.broadcasted_iota(jnp.int32, (NA_VT_ROWS, 1), 0) % NA_VT_PAIR >= LANE
    nvt_ref[0] = (_dot_nt(wvt_ref[0], hx_scr[...]) + jnp.where(ones_rows, 1.0, 0.0)).astype(BF16)
    ng_ref[0] = proj(COL_NA + 2 * D_NA, D_NA).astype(BF16)
    gv_ref[0] = proj(COL_GV, D_GLA_V).astype(BF16)
    gg_ref[0] = proj(COL_GG, D_GLA_V).astype(BF16)
    gz_ref[0] = proj(COL_Z, Z_PAD)

    cos = cos_ref[...]
    sin = sin_ref[...]

    def rope(c0, out_ref, scale_out):
        u = proj(c0, GLA_KP)
        u1, u2 = u[:, :LANE], u[:, LANE:]
        out_ref[0, :, :LANE] = (u1 * cos - u2 * sin) * scale_out
        out_ref[0, :, LANE:] = (u1 * sin + u2 * cos) * scale_out

    rope(COL_GQ, gq_ref, GLA_DK ** -0.5)
    rope(COL_GK, gk_ref, 1.0)


def _inproj(h, mod, l, norm_w, w_in, w_nvt, cos_t, sin_t):
    nb = h.shape[0]
    nt = T_ALL // TOK_TILE
    tok = lambda width: pl.BlockSpec((1, TOK_TILE, width), lambda b, j: (b, j, 0))
    layer = lambda a: pl.BlockSpec((1,) + a.shape[1:], lambda b, j: (l,) + (0,) * (a.ndim - 1))
    out_widths = [(2 * D_FOUR, BF16), (D_NA, BF16), (D_NA, BF16), (None, BF16), (D_NA, BF16),
                  (GLA_KP, F32), (GLA_KP, F32), (D_GLA_V, BF16), (D_GLA_V, BF16), (Z_PAD, F32)]
    vt_spec = pl.BlockSpec((1, NA_VT_ROWS, TOK_TILE), lambda b, j: (b, 0, j))
    vt_shape = jax.ShapeDtypeStruct((nb, NA_VT_ROWS, T_ALL), BF16)
    return pl.pallas_call(
        _inproj_kernel,
        grid=(nb, nt),
        in_specs=[
            tok(D_MODEL),
            layer(mod),
            layer(norm_w),
            layer(w_in),
            layer(w_nvt),
            pl.BlockSpec((TOK_TILE, LANE), lambda b, j: (j, 0)),
            pl.BlockSpec((TOK_TILE, LANE), lambda b, j: (j, 0)),
        ],
        out_specs=[vt_spec if w is None else tok(w) for w, _ in out_widths],
        out_shape=[vt_shape if w is None else jax.ShapeDtypeStruct((nb, T_ALL, w), dt)
                   for w, dt in out_widths],
        scratch_shapes=[pltpu.VMEM((TOK_TILE, D_MODEL), BF16), pltpu.VMEM((D_MODEL, N_COLS), BF16)],
        compiler_params=pltpu.CompilerParams(
            dimension_semantics=("arbitrary", "arbitrary"), vmem_limit_bytes=VMEM_LIMIT),
        name="inproj",
    )(h, mod, norm_w, w_in, w_nvt, cos_t, sin_t)


def _fourier_kernel(p_ref, cbd_ref, sbd_ref, cl_ref, sl_ref, cc_ref, sc_ref, wf_ref, o_ref,
                    uc_scr, us_scr):
    u = p_ref[0, :, :D_FOUR]
    uc_scr[...] = _dot(u, cbd_ref[...]).astype(BF16)
    us_scr[...] = _dot(u, sbd_ref[...]).astype(BF16)

    def finish(f, r0, n):
        four = _dot(f.astype(BF16), wf_ref[...])
        gate = p_ref[0, pl.ds(r0, n), D_FOUR:].astype(F32)
        o_ref[0, pl.ds(r0, n), :] = (four * _silu(gate)).astype(BF16)

    tile = 256

    def body(i, carry):
        r0 = pl.multiple_of(i * tile, tile)
        f = (_dot(cl_ref[pl.ds(r0, tile), :], uc_scr[:SEQ, :])
             - _dot(sl_ref[pl.ds(r0, tile), :], us_scr[:SEQ, :]))
        finish(f, r0, tile)
        return carry

    lax.fori_loop(0, SEQ // tile, body, 0)
    f_c = _dot(cc_ref[...], uc_scr[SEQ:, :]) - _dot(sc_ref[...], us_scr[SEQ:, :])
    finish(f_c, SEQ, CTX_LEN)


def _fourier(p_four, tabs, wf):
    nb = p_four.shape[0]
    full = lambda a: pl.BlockSpec(a.shape, lambda b: (0,) * a.ndim)
    return pl.pallas_call(
        _fourier_kernel,
        grid=(nb,),
        in_specs=[pl.BlockSpec((1, T_ALL, 2 * D_FOUR), lambda b: (b, 0, 0))]
        + [full(t) for t in tabs] + [full(wf)],
        out_specs=pl.BlockSpec((1, T_ALL, D_FOUR), lambda b: (b, 0, 0)),
        out_shape=jax.ShapeDtypeStruct((nb, T_ALL, D_FOUR), BF16),
        scratch_shapes=[pltpu.VMEM((T_ALL, D_FOUR), BF16), pltpu.VMEM((T_ALL, D_FOUR), BF16)],
        compiler_params=pltpu.CompilerParams(
            dimension_semantics=("arbitrary",), vmem_limit_bytes=VMEM_LIMIT),
        name="fourier_mix",
    )(p_four, *tabs, wf)


def _na_build_bias(ra_ref, rb_ref, tbl_ref):
    w = GRID_W
    ck = lax.broadcasted_iota(jnp.int32, (w, 2 * w), 0)
    ln = lax.broadcasted_iota(jnp.int32, (w, 2 * w), 1)
    cq = ln % w
    cs = jnp.clip(cq - NA_KW // 2, 0, w - NA_KW)
    col_ok = (ck >= cs) & (ck < cs + NA_KW)
    left = ln < w
    neg = jnp.full((w, 2 * w), -jnp.inf, F32)
    offs = _na_row_offsets()
    cache = {}
    for typ in range(3):
        for ap in range(NA_QROWS // 2):
            for jr in range(NA_KROWS):
                d0 = offs[typ][2 * ap][jr]
                d1 = offs[typ][2 * ap + 1][jr]
                for h in range(NA_HEADS):
                    key = (h, d0, d1)
                    if key not in cache:
                        if d0 is None and d1 is None:
                            cache[key] = neg
                        else:
                            ok = col_ok
                            if d0 is None:
                                row = rb_ref[h, d1:d1 + 1, :]
                                ok = ok & jnp.logical_not(left)
                            elif d1 is None:
                                row = ra_ref[h, d0:d0 + 1, :]
                                ok = ok & left
                            else:
                                row = ra_ref[h, d0:d0 + 1, :] + rb_ref[h, d1:d1 + 1, :]
                            skew = pltpu.roll(jnp.broadcast_to(row * LOG2E, (w, 2 * w)),
                                              2 * w - (NA_KW - 1), 1, stride=1, stride_axis=0)
                            cache[key] = jnp.where(ok, skew, neg)
                    tbl_ref[h * 3 + typ, jr * w:(jr + 1) * w, ap * 2 * w:(ap + 1) * 2 * w] = cache[key]


def _na_kernel(q_ref, k_ref, vt_ref, g_ref, ra_ref, rb_ref, o_ref, bias_ref):
    lane = lax.broadcasted_iota(jnp.int32, (1, LANE), 1)
    first_head_lanes = lane < NA_HEAD_DIM
    first_head_rows = lax.broadcasted_iota(jnp.int32, (LANE, 1), 0) < NA_HEAD_DIM

    @pl.when(pl.program_id(0) == 0)
    def _():
        _na_build_bias(ra_ref, rb_ref, bias_ref)

    def attend(j, local):
        qrows = pl.ds(pl.multiple_of(j * NA_QB, NA_QB) if local else j * NA_QB, NA_QB)
        if local:
            k0 = pl.multiple_of(jnp.clip(j * NA_QROWS - NA_KH // 2, 0, GRID_H - NA_KROWS) * GRID_W,
                                NA_QB)
            typ = jnp.where(j == 0, 0, jnp.where(j == NA_NBLK - 1, 2, 1))
        heads = range(NA_HEADS)
        cols = [slice(h // 2 * LANE, (h // 2 + 1) * LANE) for h in heads]
        s_c, s_l = [], []
        for h in heads:
            qp = q_ref[0, qrows, cols[h]]
            keep = first_head_lanes if h % 2 == 0 else jnp.logical_not(first_head_lanes)
            qm = jnp.where(keep, qp, jnp.zeros_like(qp))
            s_c.append(_dot_nt(k_ref[0, SEQ:, cols[h]], qm))
            if local:
                s_l.append(_dot_nt(k_ref[0, pl.ds(k0, NA_KB), cols[h]], qm) + bias_ref[h * 3 + typ])
        e_c, e_l = [], []
        for h in heads:
            m = _reduce_rows(s_c[h], jnp.max)
            if local:
                m = jnp.maximum(m, _reduce_rows(s_l[h], jnp.max))
            e_c.append(jnp.exp2((s_c[h] - m).astype(BF16)))
            if local:
                e_l.append(jnp.exp2((s_l[h] - m).astype(BF16)))
        for p in range(NA_HEADS // 2):
            vrows = slice(p * NA_VT_PAIR, (p + 1) * NA_VT_PAIR)
            acc = None
            for h in (2 * p, 2 * p + 1):
                o = _dot(vt_ref[0, vrows, SEQ:], e_c[h])
                if local:
                    o = o + _dot(vt_ref[0, vrows, pl.ds(k0, NA_KB)], e_l[h])
                o = o[:LANE] / o[LANE:LANE + 1]
                acc = o if h % 2 == 0 else jnp.where(first_head_rows, acc, o)
            gate = g_ref[0, qrows, cols[2 * p]].astype(F32)
            o_ref[0, qrows, cols[2 * p]] = (acc.T * _silu(gate)).astype(BF16)

    def local_body(j, carry):
        attend(j, True)
        return carry

    lax.fori_loop(0, NA_NBLK, local_body, 0)
    attend(NA_NBLK, False)


def _na(nq, nk, nvt, ng, rpb_a, rpb_b):
    nb = nq.shape[0]
    seq = pl.BlockSpec((1, T_ALL, D_NA), lambda b: (b, 0, 0))
    seq_t = pl.BlockSpec((1, NA_VT_ROWS, T_ALL), lambda b: (b, 0, 0))
    tab = pl.BlockSpec(rpb_a.shape, lambda b: (0, 0, 0))
    return pl.pallas_call(
        _na_kernel,
        grid=(nb,),
        in_specs=[seq, seq, seq_t, seq, tab, tab],
        out_specs=seq,
        out_shape=jax.ShapeDtypeStruct((nb, T_ALL, D_NA), BF16),
        scratch_shapes=[pltpu.VMEM((NA_HEADS * 3, NA_KB, NA_QB), F32)],
        compiler_params=pltpu.CompilerParams(
            dimension_semantics=("arbitrary",), vmem_limit_bytes=VMEM_LIMIT),
        name="nbr_attention",
    )(nq, nk, nvt, ng, rpb_a, rpb_b)


def _gla_kernel(q_ref, k_ref, v_ref, g_ref, z_ref, wa_ref, ba_ref, nw_ref, o_ref,
                st_scr, o_scr, qd_scr, ke_scr, vt_scr, dec_scr):
    C = GLA_CHUNK
    TR = GLA_TILE
    n_sub = TR // C
    kl = lax.broadcasted_iota(jnp.int32, (1, GLA_KP), 1) % LANE
    k_head = jnp.where(kl < GLA_HEADS * (GLA_DK // 2), kl // (GLA_DK // 2), GLA_HEADS)
    v_head = lax.broadcasted_iota(jnp.int32, (1, D_GLA_V), 1) // GLA_DV
    st_mask = (lax.broadcasted_iota(jnp.int32, (D_GLA_V, 1), 0) // GLA_DV) == k_head
    ti4 = lax.broadcasted_iota(jnp.int32, (C, GLA_HEADS * C), 0)
    si4 = lax.broadcasted_iota(jnp.int32, (C, GLA_HEADS * C), 1) % C
    tr = lax.broadcasted_iota(jnp.int32, (TR, TR), 0)
    sr = lax.broadcasted_iota(jnp.int32, (TR, TR), 1)
    same_chunk = (tr // C) == (sr // C)
    tri = [jnp.where(same_chunk & (sr <= tr), 1.0, 0.0).astype(BF16),
           jnp.where(same_chunk & (sr >= tr), 1.0, 0.0).astype(BF16)]

    def tile_body(i, carry):
        r0 = pl.multiple_of(i * TR, TR)
        rows = pl.ds(r0, TR)
        z = z_ref[0, rows, :].astype(BF16)
        g = _dot(z, wa_ref[...]) + ba_ref[...]
        la = (jnp.minimum(g, 0.0) - jnp.log(1.0 + jnp.exp(-jnp.abs(g)))) * (1.0 / GLA_TAU)
        q = q_ref[0, rows, :]
        k = k_ref[0, rows, :]
        v = v_ref[0, rows, :]
        v_bd = []
        for c in range(n_sub):
            v_c = v[c * C:(c + 1) * C]
            v_bd.append(jnp.concatenate(
                [jnp.where(v_head == h, v_c, jnp.zeros_like(v_c)) for h in range(GLA_HEADS)], axis=0))
            vt_scr[i * n_sub + c] = v_c.astype(F32).T.astype(BF16)
        scores = []
        for d in range(2):
            fwd = d == 0
            la_hi, la_lo = _split_bf16(la[:, d * GLA_KP:(d + 1) * GLA_KP])
            bcum = _dot(tri[d], la_hi) + _dot(tri[d], la_lo)
            ends = [bcum[(c + 1) * C - 1:(c + 1) * C] if fwd else bcum[c * C:c * C + 1]
                    for c in range(n_sub)]
            b_end = jnp.concatenate([jnp.broadcast_to(e, (C, GLA_KP)) for e in ends], axis=0)
            q_dec = (q * jnp.exp(bcum)).astype(BF16)
            k_dec = k * jnp.exp(-bcum)
            qd_scr[d, rows, :] = q_dec
            ke_scr[d, rows, :] = (k * jnp.exp(b_end - bcum)).astype(BF16)
            for c in range(n_sub):
                dec_scr[d, pl.ds(i * n_sub + c, 1), :] = jnp.exp(ends[c])
                k_c = k_dec[c * C:(c + 1) * C]
                k_bd = jnp.concatenate(
                    [jnp.where(k_head == h, k_c, 0.0) for h in range(GLA_HEADS)], axis=0).astype(BF16)
                scores.append(_dot_nt(q_dec[c * C:(c + 1) * C], k_bd))
        for d in range(2):
            causal = (si4 <= ti4) if d == 0 else (si4 >= ti4)
            for c in range(n_sub):
                a = jnp.where(causal, scores[d * n_sub + c], 0.0).astype(BF16)
                o_scr[d, pl.ds(r0 + c * C, C), :] = _dot(a, v_bd[c])
        return carry

    lax.fori_loop(0, T_ALL // TR, tile_body, 0, unroll=3)

    st_scr[...] = jnp.zeros_like(st_scr)
    n_all = N_CHUNKS_C + N_CHUNKS_X

    def step(i, carry):
        in_ctx = i < N_CHUNKS_C
        idx_f = jnp.where(in_ctx, N_CHUNKS_X + i, i - N_CHUNKS_C)
        idx_b = jnp.where(in_ctx, N_CHUNKS_X + N_CHUNKS_C - 1 - i, n_all - 1 - i)
        for d, idx in ((0, idx_f), (1, idx_b)):
            off = pl.ds(pl.multiple_of(idx * C, C), C)
            st = st_scr[d]
            o_scr[d, off, :] += _dot_nt(qd_scr[d, off, :], st.astype(BF16))
            half = D_GLA_V // 2
            ke = ke_scr[d, off, :]
            upd = jnp.concatenate([_dot(vt_scr[idx, :half, :], ke), _dot(vt_scr[idx, half:, :], ke)], axis=0)
            st_scr[d] = st * dec_scr[d, pl.ds(idx, 1), :] + jnp.where(st_mask, upd, 0.0)
        return carry

    lax.fori_loop(0, n_all, step, 0, unroll=4)

    head_ones = jnp.where(
        (lax.broadcasted_iota(jnp.int32, (D_GLA_V, 1), 0) // GLA_DV) == v_head, 1.0, 0.0).astype(BF16)
    tile = 256

    def fin(i, carry):
        r0 = pl.multiple_of(i * tile, tile)
        o = o_scr[0, pl.ds(r0, tile), :] + o_scr[1, pl.ds(r0, tile), :]
        sq_hi, sq_lo = _split_bf16(o * o)
        ms = (_dot(sq_hi, head_ones) + _dot(sq_lo, head_ones)) * (1.0 / GLA_DV)
        y = o * lax.rsqrt(ms + EPS) * nw_ref[...]
        gate = g_ref[0, pl.ds(r0, tile), :].astype(F32)
        o_ref[0, pl.ds(r0, tile), :] = (y * _silu(gate)).astype(BF16)
        return carry

    lax.fori_loop(0, T_ALL // tile, fin, 0)


def _gla(gq, gk, gv, gg, gz, wa, ba, gnw):
    nb = gq.shape[0]
    seq = lambda w: pl.BlockSpec((1, T_ALL, w), lambda b: (b, 0, 0))
    full = lambda a: pl.BlockSpec(a.shape, lambda b: (0,) * a.ndim)
    return pl.pallas_call(
        _gla_kernel,
        grid=(nb,),
        in_specs=[seq(GLA_KP), seq(GLA_KP), seq(D_GLA_V), seq(D_GLA_V), seq(Z_PAD),
                  full(wa), full(ba), full(gnw)],
        out_specs=seq(D_GLA_V),
        out_shape=jax.ShapeDtypeStruct((nb, T_ALL, D_GLA_V), BF16),
        scratch_shapes=[pltpu.VMEM((2, D_GLA_V, GLA_KP), F32),
                        pltpu.VMEM((2, T_ALL, D_GLA_V), F32),
                        pltpu.VMEM((2, T_ALL, GLA_KP), BF16),
                        pltpu.VMEM((2, T_ALL, GLA_KP), BF16),
                        pltpu.VMEM((T_ALL // GLA_CHUNK, D_GLA_V, GLA_CHUNK), BF16),
                        pltpu.VMEM((2, T_ALL // GLA_CHUNK, GLA_KP), F32)],

        compiler_params=pltpu.CompilerParams(
            dimension_semantics=("arbitrary",), vmem_limit_bytes=VMEM_LIMIT),
        name="gla_mix",
    )(gq, gk, gv, gg, gz, wa, ba, gnw)


def _outproj_kernel(h_ref, mf_ref, mn_ref, mg_ref, w_ref, mod_ref, nf_ref, o_ref, w_scr, *, tile, final):
    b = pl.program_id(0)
    j = pl.program_id(1)

    @pl.when((b == 0) & (j == 0))
    def _():
        for r in range(0, D_MODEL, 256):
            w_scr[r:r + 256, :] = w_ref[0, r:r + 256, :].astype(BF16)

    acc = (_dot(mf_ref[0], w_scr[:D_FOUR, :])
           + _dot(mn_ref[0], w_scr[D_FOUR:D_FOUR + D_NA, :])
           + _dot(mg_ref[0], w_scr[D_FOUR + D_NA:, :]))
    gate = _mod_rows(mod_ref, b, j * tile, tile, 2 * D_MODEL)
    x = h_ref[0] + gate * acc
    if final:
        ms = jnp.mean(x * x, axis=-1, keepdims=True)
        x = x * lax.rsqrt(ms + EPS) * nf_ref[...]
    o_ref[0] = x


def _outproj(h, mf, mn, mg, w_out, mod, l, nf, final):
    nb = h.shape[0]
    tile = 512 if final else TOK_TILE
    rows = SEQ if final else T_ALL
    tok = lambda width: pl.BlockSpec((1, tile, width), lambda b, j: (b, j, 0))
    layer = lambda a: pl.BlockSpec((1,) + a.shape[1:], lambda b, j: (l,) + (0,) * (a.ndim - 1))
    return pl.pallas_call(
        functools.partial(_outproj_kernel, tile=tile, final=final),
        grid=(nb, rows // tile),
        in_specs=[tok(D_MODEL), tok(D_FOUR), tok(D_NA), tok(D_GLA_V),
                  layer(w_out), layer(mod), pl.BlockSpec((1, D_MODEL), lambda b, j: (0, 0))],
        out_specs=tok(D_MODEL),
        out_shape=jax.ShapeDtypeStruct((nb, rows, D_MODEL), F32),
        scratch_shapes=[pltpu.VMEM((D_MODEL, D_MODEL), BF16)],
        compiler_params=pltpu.CompilerParams(
            dimension_semantics=("arbitrary", "arbitrary"), vmem_limit_bytes=VMEM_LIMIT),
        name="outproj_final" if final else "outproj",
    )(h, mf, mn, mg, w_out, mod, nf)


def _decay_weights(w_f, b_f, w_b, b_b):
    n_freq = GLA_DK // 4
    lane_pad = lambda a: jnp.zeros(a.shape[:-1] + (LANE - GLA_HEADS * 2 * n_freq,), a.dtype)
    lanes = lambda a: jnp.concatenate(
        [blk for offs in _qk_pieces() for blk in [a[..., o:o + n_freq] for o in offs] + [lane_pad(a)]],
        axis=-1)
    zero = jnp.zeros_like(lanes(w_f))
    top = jnp.concatenate([lanes(w_f), zero], axis=-1)
    bot = jnp.concatenate([zero, lanes(w_b)], axis=-1)
    pad = jnp.zeros((w_f.shape[0], Z_PAD - 2 * GLA_RANK, 2 * GLA_KP), w_f.dtype)
    wa = jnp.concatenate([top, bot, pad], axis=1).astype(BF16)
    ba = jnp.concatenate([lanes(b_f), lanes(b_b)], axis=-1)[:, None, :]
    return wa, ba


def kernel(x, c, ctx, c_ctx, w_ada, b_ada, norm_w, w_in, w_four, rpb, w_alpha_fwd, b_alpha_fwd,
           w_alpha_bwd, b_alpha_bwd, gla_norm_w, w_out, norm_f):
    nb = x.shape[0]
    depth = w_in.shape[0]
    assert x.shape == (nb, SEQ, D_MODEL) and ctx.shape == (nb, CTX_LEN, D_MODEL) and nb <= 8

    v0 = sum(IN_SIZES[:4])
    w_nvt = jnp.swapaxes(w_in[:, :, v0:v0 + D_NA], 1, 2).astype(BF16)
    w_nvt = jnp.pad(w_nvt.reshape(depth, NA_HEADS // 2, LANE, D_MODEL),
                    ((0, 0), (0, 0), (0, NA_VT_PAIR - LANE), (0, 0))).reshape(depth, NA_VT_ROWS, D_MODEL)
    w_four_b = w_four.astype(BF16)
    norm_w3 = norm_w.reshape(depth, 1, D_MODEL)
    wa, ba = _decay_weights(w_alpha_fwd, b_alpha_fwd, w_alpha_bwd, b_alpha_bwd)
    gnw = jnp.tile(gla_norm_w, (1, GLA_HEADS)).reshape(depth, 1, D_GLA_V)
    n_off = 2 * NA_KW - 1
    rpb_rev = rpb[..., ::-1]
    rpb_a = jnp.pad(rpb_rev, ((0, 0), (0, 0), (0, 1), (0, LANE - n_off)))
    rpb_b = jnp.pad(rpb_rev, ((0, 0), (0, 0), (0, 1), (GRID_W, LANE - GRID_W - n_off)))

    cos_t, sin_t = (jnp.asarray(t) for t in _rope_tables())
    c64, s64 = _dft_tables(FOUR_DIM)
    cl, sl = _dft_tables(SEQ)
    ccx, scx = _dft_tables(CTX_LEN)
    tabs = [jnp.asarray(t, dtype=F32).astype(BF16) for t in
            (_block_diag(c64, FOUR_GROUPS), _block_diag(s64, FOUR_GROUPS), cl, sl, ccx, scx)]

    cc = jnp.zeros((16, D_MODEL), F32).at[:nb].set(c).at[8].set(c_ctx)
    mod = _modulation(cc, w_ada, b_ada)

    h = jnp.concatenate([x, ctx], axis=1)
    for l in range(depth):
        final = l == depth - 1
        (p_four, nq, nk, nvt, ng, gq, gk, gv, gg, gz) = _inproj(
            h, mod, l, norm_w3, w_in, w_nvt, cos_t, sin_t)
        mf = _fourier(p_four, tabs, w_four_b[l])
        mn = _na(nq, nk, nvt, ng, rpb_a[l], rpb_b[l])
        mg = _gla(gq, gk, gv, gg, gz, wa[l], ba[l], gnw[l])
        h = _outproj(h, mf, mn, mg, w_out, mod, l, norm_f.reshape(1, D_MODEL), final)
    return h
```

```python
import functools

import numpy as np
import jax
import jax.numpy as jnp
from jax import lax
from jax.experimental import pallas as pl
from jax.experimental.pallas import tpu as pltpu

F32 = jnp.float32
BF16 = jnp.bfloat16

D_MODEL = 1024
SEQ = 2048
CTX_LEN = 256
T_ALL = SEQ + CTX_LEN
GRID_W = 64
GRID_H = SEQ // GRID_W
EPS = 1e-6

FOUR_GROUPS, FOUR_DIM = 4, 64
D_FOUR = FOUR_GROUPS * FOUR_DIM
NA_HEADS, NA_HEAD_DIM = 6, 64
D_NA = NA_HEADS * NA_HEAD_DIM
NA_KH, NA_KW = 8, 16
GLA_HEADS, GLA_DK, GLA_DV = 4, 48, 96
D_GLA_K = GLA_HEADS * GLA_DK
D_GLA_V = GLA_HEADS * GLA_DV
GLA_RANK = 16
GLA_TAU = 16.0
GLA_CHUNK = 64
ROPE_BASE = 10000.0
LOG2E = 1.4426950408889634

IN_SIZES = (D_FOUR, D_FOUR, D_NA, D_NA, D_NA, D_NA, D_GLA_K, D_GLA_K, D_GLA_V, D_GLA_V,
            GLA_RANK, GLA_RANK)
D_IN = sum(IN_SIZES)

LANE = 128
GLA_KP = 2 * LANE
Z_PAD = LANE
N_CHUNKS_X = SEQ // GLA_CHUNK
N_CHUNKS_C = CTX_LEN // GLA_CHUNK
GLA_TILE = 256

NA_QROWS = 4
NA_KROWS = NA_QROWS + NA_KH
NA_QB = NA_QROWS * GRID_W
NA_KB = NA_KROWS * GRID_W
NA_NBLK = GRID_H // NA_QROWS
NA_VT_PAIR = LANE + 16
NA_VT_ROWS = NA_HEADS // 2 * NA_VT_PAIR

TOK_TILE = 768
VMEM_LIMIT = 56 * 1024 * 1024

COL_FOUR = 0
COL_NA = COL_FOUR + 2 * D_FOUR
COL_GQ = COL_NA + 3 * D_NA
COL_GK = COL_GQ + GLA_KP
COL_GV = COL_GK + GLA_KP
COL_GG = COL_GV + D_GLA_V
COL_Z = COL_GG + D_GLA_V
N_COLS = COL_Z + Z_PAD


def _dot(a, b):
    return jnp.dot(a, b, preferred_element_type=F32)


def _dot_nt(a, b):
    return lax.dot_general(a, b, (((1,), (1,)), ((), ())), preferred_element_type=F32)


def _silu(x):
    return x / (1.0 + jnp.exp(-x))


def _reduce_rows(x, op):
    r, n = x.shape
    part = op(x.reshape(r // 8, 8, n), axis=0)
    return op(part, axis=0, keepdims=True)


def _split_bf16(x):
    hi = x.astype(BF16)
    lo = (x - hi.astype(F32)).astype(BF16)
    return hi, lo


def _qk_pieces():
    n_freq = GLA_DK // 4
    return [[h * GLA_DK + half * 2 * n_freq + part * n_freq
             for h in range(GLA_HEADS) for half in range(2)] for part in range(2)]


def _rope_tables():
    n_freq = GLA_DK // 4
    inv = ROPE_BASE ** (-np.arange(n_freq, dtype=np.float64) / n_freq)
    pos = np.arange(SEQ)
    prow, pcol = pos // GRID_W, pos % GRID_W
    cos = np.ones((T_ALL, LANE), np.float64)
    sin = np.zeros((T_ALL, LANE), np.float64)
    for h in range(GLA_HEADS):
        for half, p in enumerate((prow, pcol)):
            ang = p[:, None] * inv[None, :]
            lo = h * 2 * n_freq + half * n_freq
            cos[:SEQ, lo:lo + n_freq] = np.cos(ang)
            sin[:SEQ, lo:lo + n_freq] = np.sin(ang)
    return cos.astype(np.float32), sin.astype(np.float32)


def _dft_tables(n):
    idx = (np.arange(n)[:, None] * np.arange(n)[None, :]) % n
    ang = 2.0 * np.pi * idx / n
    return np.cos(ang) / np.sqrt(n), np.sin(ang) / np.sqrt(n)


def _block_diag(m, reps):
    n = m.shape[0]
    out = np.zeros((n * reps, n * reps), m.dtype)
    for r in range(reps):
        out[r * n:(r + 1) * n, r * n:(r + 1) * n] = m
    return out


def _na_row_offsets():
    out = []
    for r0 in (0, NA_QROWS, GRID_H - NA_QROWS):
        k0 = int(np.clip(r0 - NA_KH // 2, 0, GRID_H - NA_KROWS))
        per_type = []
        for a in range(NA_QROWS):
            r = r0 + a
            rs = int(np.clip(r - NA_KH // 2, 0, GRID_H - NA_KH))
            per_type.append([(k0 + j) - r + NA_KH - 1 if rs <= k0 + j < rs + NA_KH else None
                             for j in range(NA_KROWS)])
        out.append(per_type)
    return out


def _mod_kernel(c_ref, w_ref, b_ref, o_ref):
    c = c_ref[...]
    s_hi, s_lo = _split_bf16(_silu(c))
    w_hi, w_lo = _split_bf16(w_ref[0])
    acc = _dot(s_hi, w_hi) + _dot(s_lo, w_hi) + _dot(s_hi, w_lo)
    o_ref[0] = acc + b_ref[0]


def _modulation(cc, w_ada, b_ada):
    depth = w_ada.shape[0]
    n_mod = w_ada.shape[2]
    tile = D_MODEL
    return pl.pallas_call(
        _mod_kernel,
        grid=(depth, n_mod // tile),
        in_specs=[
            pl.BlockSpec((16, D_MODEL), lambda l, j: (0, 0)),
            pl.BlockSpec((1, D_MODEL, tile), lambda l, j: (l, 0, j)),
            pl.BlockSpec((1, 1, tile), lambda l, j: (l, 0, j)),
        ],
        out_specs=pl.BlockSpec((1, 16, tile), lambda l, j: (l, 0, j)),
        out_shape=jax.ShapeDtypeStruct((depth, 16, n_mod), F32),
        compiler_params=pltpu.CompilerParams(
            dimension_semantics=("arbitrary", "arbitrary"), vmem_limit_bytes=VMEM_LIMIT),
        name="adaln_mod",
    )(cc, w_ada, b_ada.reshape(depth, 1, n_mod))


def _mod_rows(mod_ref, b, row0, n_rows, col0):
    rows = row0 + lax.broadcasted_iota(jnp.int32, (n_rows, 1), 0)
    m_x = mod_ref[0, pl.ds(b, 1), col0:col0 + D_MODEL]
    m_c = mod_ref[0, 8:9, col0:col0 + D_MODEL]
    return jnp.where(rows >= SEQ, m_c, m_x)


def _fill_weight(w_ref, w_scr, dst0, src0, width, scale=None):
    rows = w_scr.shape[0]
    for c in range(0, width, LANE):
        n = min(LANE, width - c)
        blk = w_ref[0, :, src0 + c:src0 + c + n]
        if scale is not None:
            blk = blk * scale
        if n < LANE:
            blk = jnp.concatenate([blk, jnp.zeros((rows, LANE - n), F32)], axis=1)
        w_scr[:, dst0 + c:dst0 + c + LANE] = blk.astype(BF16)


def _inproj_kernel(h_ref, mod_ref, nw_ref, w_ref, wvt_ref, cos_ref, sin_ref,
                   four_ref, nq_ref, nk_ref, nvt_ref, ng_ref,
                   gq_ref, gk_ref, gv_ref, gg_ref, gz_ref, hx_scr, w_scr):
    b = pl.program_id(0)
    j = pl.program_id(1)

    @pl.when((b == 0) & (j == 0))
    def _():
        src = np.concatenate([[0], np.cumsum(IN_SIZES)])
        _fill_weight(w_ref, w_scr, COL_FOUR, src[0], 2 * D_FOUR)
        _fill_weight(w_ref, w_scr, COL_NA, src[2], D_NA, NA_HEAD_DIM ** -0.5 * LOG2E)
        _fill_weight(w_ref, w_scr, COL_NA + D_NA, src[3], D_NA)
        _fill_weight(w_ref, w_scr, COL_NA + 2 * D_NA, src[5], D_NA)
        n_freq = GLA_DK // 4
        pad = jnp.zeros((D_MODEL, LANE - GLA_HEADS * 2 * n_freq), F32)
        for dst0, src0 in ((COL_GQ, src[6]), (COL_GK, src[7])):
            for part, offs in enumerate(_qk_pieces()):
                tile = jnp.concatenate(
                    [w_ref[0, :, src0 + o:src0 + o + n_freq] for o in offs] + [pad], axis=1)
                w_scr[:, dst0 + part * LANE:dst0 + (part + 1) * LANE] = tile.astype(BF16)
        _fill_weight(w_ref, w_scr, COL_GV, src[8], D_GLA_V)
        _fill_weight(w_ref, w_scr, COL_GG, src[9], D_GLA_V)
        _fill_weight(w_ref, w_scr, COL_Z, src[10], 2 * GLA_RANK)

    x = h_ref[0]
    ms = jnp.mean(x * x, axis=-1, keepdims=True)
    y = x * lax.rsqrt(ms + EPS) * nw_ref[0]
    shift = _mod_rows(mod_ref, b, j * TOK_TILE, TOK_TILE, 0)
    scale = _mod_rows(mod_ref, b, j * TOK_TILE, TOK_TILE, D_MODEL)
    hx_scr[...] = (y * (1.0 + scale) + shift).astype(BF16)

    def proj(c0, width):
        return _dot(hx_scr[...], w_scr[:, c0:c0 + width])

    four_ref[0] = proj(COL_FOUR, 2 * D_FOUR).astype(BF16)
    nq_ref[0] = proj(COL_NA, D_NA).astype(BF16)
    nk_ref[0] = proj(COL_NA + D_NA, D_NA).astype(BF16)
    ones_rows = lax.broadcasted_iota(jnp.int32, (NA_VT_ROWS, 1), 0) % NA_VT_PAIR >= LANE
    nvt_ref[0] = (_dot_nt(wvt_ref[0], hx_scr[...]) + jnp.where(ones_rows, 1.0, 0.0)).astype(BF16)
    ng_ref[0] = proj(COL_NA + 2 * D_NA, D_NA).astype(BF16)
    gv_ref[0] = proj(COL_GV, D_GLA_V).astype(BF16)
    gg_ref[0] = proj(COL_GG, D_GLA_V).astype(BF16)
    gz_ref[0] = proj(COL_Z, Z_PAD)

    cos = cos_ref[...]
    sin = sin_ref[...]

    def rope(c0, out_ref, scale_out):
        u = proj(c0, GLA_KP)
        u1, u2 = u[:, :LANE], u[:, LANE:]
        out_ref[0, :, :LANE] = (u1 * cos - u2 * sin) * scale_out
        out_ref[0, :, LANE:] = (u1 * sin + u2 * cos) * scale_out

    rope(COL_GQ, gq_ref, GLA_DK ** -0.5)
    rope(COL_GK, gk_ref, 1.0)


def _inproj(h, mod, l, norm_w, w_in, w_nvt, cos_t, sin_t):
    nb = h.shape[0]
    nt = T_ALL // TOK_TILE
    tok = lambda width: pl.BlockSpec((1, TOK_TILE, width), lambda b, j: (b, j, 0))
    layer = lambda a: pl.BlockSpec((1,) + a.shape[1:], lambda b, j: (l,) + (0,) * (a.ndim - 1))
    out_widths = [(2 * D_FOUR, BF16), (D_NA, BF16), (D_NA, BF16), (None, BF16), (D_NA, BF16),
                  (GLA_KP, F32), (GLA_KP, F32), (D_GLA_V, BF16), (D_GLA_V, BF16), (Z_PAD, F32)]
    vt_spec = pl.BlockSpec((1, NA_VT_ROWS, TOK_TILE), lambda b, j: (b, 0, j))
    vt_shape = jax.ShapeDtypeStruct((nb, NA_VT_ROWS, T_ALL), BF16)
    return pl.pallas_call(
        _inproj_kernel,
        grid=(nb, nt),
        in_specs=[
            tok(D_MODEL),
            layer(mod),
            layer(norm_w),
            layer(w_in),
            layer(w_nvt),
            pl.BlockSpec((TOK_TILE, LANE), lambda b, j: (j, 0)),
            pl.BlockSpec((TOK_TILE, LANE), lambda b, j: (j, 0)),
        ],
        out_specs=[vt_spec if w is None else tok(w) for w, _ in out_widths],
        out_shape=[vt_shape if w is None else jax.ShapeDtypeStruct((nb, T_ALL, w), dt)
                   for w, dt in out_widths],
        scratch_shapes=[pltpu.VMEM((TOK_TILE, D_MODEL), BF16), pltpu.VMEM((D_MODEL, N_COLS), BF16)],
        compiler_params=pltpu.CompilerParams(
            dimension_semantics=("arbitrary", "arbitrary"), vmem_limit_bytes=VMEM_LIMIT),
        name="inproj",
    )(h, mod, norm_w, w_in, w_nvt, cos_t, sin_t)


def _fourier_kernel(p_ref, cbd_ref, sbd_ref, cl_ref, sl_ref, cc_ref, sc_ref, wf_ref, o_ref,
                    uc_scr, us_scr):
    u = p_ref[0, :, :D_FOUR]
    uc_scr[...] = _dot(u, cbd_ref[...]).astype(BF16)
    us_scr[...] = _dot(u, sbd_ref[...]).astype(BF16)

    def finish(f, r0, n):
        four = _dot(f.astype(BF16), wf_ref[...])
        gate = p_ref[0, pl.ds(r0, n), D_FOUR:].astype(F32)
        o_ref[0, pl.ds(r0, n), :] = (four * _silu(gate)).astype(BF16)

    tile = 256

    def body(i, carry):
        r0 = pl.multiple_of(i * tile, tile)
        f = (_dot(cl_ref[pl.ds(r0, tile), :], uc_scr[:SEQ, :])
             - _dot(sl_ref[pl.ds(r0, tile), :], us_scr[:SEQ, :]))
        finish(f, r0, tile)
        return carry

    lax.fori_loop(0, SEQ // tile, body, 0)
    f_c = _dot(cc_ref[...], uc_scr[SEQ:, :]) - _dot(sc_ref[...], us_scr[SEQ:, :])
    finish(f_c, SEQ, CTX_LEN)


def _fourier(p_four, tabs, wf):
    nb = p_four.shape[0]
    full = lambda a: pl.BlockSpec(a.shape, lambda b: (0,) * a.ndim)
    return pl.pallas_call(
        _fourier_kernel,
        grid=(nb,),
        in_specs=[pl.BlockSpec((1, T_ALL, 2 * D_FOUR), lambda b: (b, 0, 0))]
        + [full(t) for t in tabs] + [full(wf)],
        out_specs=pl.BlockSpec((1, T_ALL, D_FOUR), lambda b: (b, 0, 0)),
        out_shape=jax.ShapeDtypeStruct((nb, T_ALL, D_FOUR), BF16),
        scratch_shapes=[pltpu.VMEM((T_ALL, D_FOUR), BF16), pltpu.VMEM((T_ALL, D_FOUR), BF16)],
        compiler_params=pltpu.CompilerParams(
            dimension_semantics=("arbitrary",), vmem_limit_bytes=VMEM_LIMIT),
        name="fourier_mix",
    )(p_four, *tabs, wf)


def _na_build_bias(ra_ref, rb_ref, tbl_ref):
    w = GRID_W
    ck = lax.broadcasted_iota(jnp.int32, (w, 2 * w), 0)
    ln = lax.broadcasted_iota(jnp.int32, (w, 2 * w), 1)
    cq = ln % w
    cs = jnp.clip(cq - NA_KW // 2, 0, w - NA_KW)
    col_ok = (ck >= cs) & (ck < cs + NA_KW)
    left = ln < w
    neg = jnp.full((w, 2 * w), -jnp.inf, F32)
    offs = _na_row_offsets()
    cache = {}
    for typ in range(3):
        for ap in range(NA_QROWS // 2):
            for jr in range(NA_KROWS):
                d0 = offs[typ][2 * ap][jr]
                d1 = offs[typ][2 * ap + 1][jr]
                for h in range(NA_HEADS):
                    key = (h, d0, d1)
                    if key not in cache:
                        if d0 is None and d1 is None:
                            cache[key] = neg
                        else:
                            ok = col_ok
                            if d0 is None:
                                row = rb_ref[h, d1:d1 + 1, :]
                                ok = ok & jnp.logical_not(left)
                            elif d1 is None:
                                row = ra_ref[h, d0:d0 + 1, :]
                                ok = ok & left
                            else:
                                row = ra_ref[h, d0:d0 + 1, :] + rb_ref[h, d1:d1 + 1, :]
                            skew = pltpu.roll(jnp.broadcast_to(row * LOG2E, (w, 2 * w)),
                                              2 * w - (NA_KW - 1), 1, stride=1, stride_axis=0)
                            cache[key] = jnp.where(ok, skew, neg)
                    tbl_ref[h * 3 + typ, jr * w:(jr + 1) * w, ap * 2 * w:(ap + 1) * 2 * w] = cache[key]


def _na_kernel(q_ref, k_ref, vt_ref, g_ref, ra_ref, rb_ref, o_ref, bias_ref, s_scr):
    lane = lax.broadcasted_iota(jnp.int32, (1, LANE), 1)
    first_head_lanes = lane < NA_HEAD_DIM
    first_head_rows = lax.broadcasted_iota(jnp.int32, (LANE, 1), 0) < NA_HEAD_DIM
    heads = range(NA_HEADS)
    cols = [slice(h // 2 * LANE, (h // 2 + 1) * LANE) for h in heads]

    @pl.when(pl.program_id(0) == 0)
    def _():
        _na_build_bias(ra_ref, rb_ref, bias_ref)

    def block_geometry(j):
        if isinstance(j, int):
            k0 = min(max(j * NA_QROWS - NA_KH // 2, 0), GRID_H - NA_KROWS) * GRID_W
            return pl.ds(j * NA_QB, NA_QB), k0, (0 if j == 0 else 2 if j == NA_NBLK - 1 else 1)
        k0 = jnp.clip(j * NA_QROWS - NA_KH // 2, 0, GRID_H - NA_KROWS) * GRID_W
        typ = jnp.where(j == 0, 0, jnp.where(j == NA_NBLK - 1, 2, 1))
        return pl.ds(pl.multiple_of(j * NA_QB, NA_QB), NA_QB), pl.multiple_of(k0, NA_QB), typ

    def scores(j, buf, local):
        qrows, k0, typ = block_geometry(j)
        for h in heads:
            qp = q_ref[0, qrows, cols[h]]
            keep = first_head_lanes if h % 2 == 0 else jnp.logical_not(first_head_lanes)
            qm = jnp.where(keep, qp, jnp.zeros_like(qp))
            s_scr[buf, h, :CTX_LEN, :] = _dot_nt(k_ref[0, SEQ:, cols[h]], qm)
            if local:
                s_scr[buf, h, CTX_LEN:, :] = (
                    _dot_nt(k_ref[0, pl.ds(k0, NA_KB), cols[h]], qm) + bias_ref[h * 3 + typ])

    def finish(j, buf, local):
        qrows, k0, _ = block_geometry(j)
        n_keys = CTX_LEN + NA_KB if local else CTX_LEN
        e = []
        for h in heads:
            s = s_scr[buf, h, :n_keys, :]
            e.append(jnp.exp2((s - _reduce_rows(s, jnp.max)).astype(BF16)))
        for p in range(NA_HEADS // 2):
            vrows = slice(p * NA_VT_PAIR, (p + 1) * NA_VT_PAIR)
            acc = None
            for h in (2 * p, 2 * p + 1):
                o = _dot(vt_ref[0, vrows, SEQ:], e[h][:CTX_LEN])
                if local:
                    o = o + _dot(vt_ref[0, vrows, pl.ds(k0, NA_KB)], e[h][CTX_LEN:])
                o = o[:LANE] / o[LANE:LANE + 1]
                acc = o if h % 2 == 0 else jnp.where(first_head_rows, acc, o)
            gate = g_ref[0, qrows, cols[2 * p]].astype(F32)
            o_ref[0, qrows, cols[2 * p]] = (acc.T * _silu(gate)).astype(BF16)

    def pair_body(i, carry):
        scores(2 * i + 1, 1, True)
        finish(2 * i, 0, True)
        scores(2 * i + 2, 0, True)
        finish(2 * i + 1, 1, True)
        return carry

    scores(0, 0, True)
    lax.fori_loop(0, NA_NBLK // 2 - 1, pair_body, 0)
    scores(NA_NBLK - 1, 1, True)
    finish(NA_NBLK - 2, 0, True)
    scores(NA_NBLK, 0, False)
    finish(NA_NBLK - 1, 1, True)
    finish(NA_NBLK, 0, False)


def _na(nq, nk, nvt, ng, rpb_a, rpb_b):
    nb = nq.shape[0]
    seq = pl.BlockSpec((1, T_ALL, D_NA), lambda b: (b, 0, 0))
    seq_t = pl.BlockSpec((1, NA_VT_ROWS, T_ALL), lambda b: (b, 0, 0))
    tab = pl.BlockSpec(rpb_a.shape, lambda b: (0, 0, 0))
    return pl.pallas_call(
        _na_kernel,
        grid=(nb,),
        in_specs=[seq, seq, seq_t, seq, tab, tab],
        out_specs=seq,
        out_shape=jax.ShapeDtypeStruct((nb, T_ALL, D_NA), BF16),
        scratch_shapes=[pltpu.VMEM((NA_HEADS * 3, NA_KB, NA_QB), F32),
                        pltpu.VMEM((2, NA_HEADS, CTX_LEN + NA_KB, NA_QB), F32)],
        compiler_params=pltpu.CompilerParams(
            dimension_semantics=("arbitrary",), vmem_limit_bytes=VMEM_LIMIT),
        name="nbr_attention",
    )(nq, nk, nvt, ng, rpb_a, rpb_b)


def _gla_kernel(q_ref, k_ref, v_ref, g_ref, z_ref, wa_ref, ba_ref, nw_ref, o_ref,
                st_scr, o_scr, qd_scr, ke_scr, vt_scr, dec_scr):
    C = GLA_CHUNK
    TR = GLA_TILE
    n_sub = TR // C
    kl = lax.broadcasted_iota(jnp.int32, (1, GLA_KP), 1) % LANE
    k_head = jnp.where(kl < GLA_HEADS * (GLA_DK // 2), kl // (GLA_DK // 2), GLA_HEADS)
    v_head = lax.broadcasted_iota(jnp.int32, (1, D_GLA_V), 1) // GLA_DV
    st_mask = (lax.broadcasted_iota(jnp.int32, (D_GLA_V, 1), 0) // GLA_DV) == k_head
    ti4 = lax.broadcasted_iota(jnp.int32, (C, GLA_HEADS * C), 0)
    si4 = lax.broadcasted_iota(jnp.int32, (C, GLA_HEADS * C), 1) % C
    tr = lax.broadcasted_iota(jnp.int32, (TR, TR), 0)
    sr = lax.broadcasted_iota(jnp.int32, (TR, TR), 1)
    same_chunk = (tr // C) == (sr // C)
    tri = [jnp.where(same_chunk & (sr <= tr), 1.0, 0.0).astype(BF16),
           jnp.where(same_chunk & (sr >= tr), 1.0, 0.0).astype(BF16)]

    def tile_body(i, carry):
        r0 = pl.multiple_of(i * TR, TR)
        rows = pl.ds(r0, TR)
        z = z_ref[0, rows, :].astype(BF16)
        g = _dot(z, wa_ref[...]) + ba_ref[...]
        la = (jnp.minimum(g, 0.0) - jnp.log(1.0 + jnp.exp(-jnp.abs(g)))) * (1.0 / GLA_TAU)
        q = q_ref[0, rows, :]
        k = k_ref[0, rows, :]
        v = v_ref[0, rows, :]
        v_bd = []
        for c in range(n_sub):
            v_c = v[c * C:(c + 1) * C]
            v_bd.append(jnp.concatenate(
                [jnp.where(v_head == h, v_c, jnp.zeros_like(v_c)) for h in range(GLA_HEADS)], axis=0))
            vt_scr[i * n_sub + c] = v_c.astype(F32).T.astype(BF16)
        scores = []
        for d in range(2):
            fwd = d == 0
            la_hi, la_lo = _split_bf16(la[:, d * GLA_KP:(d + 1) * GLA_KP])
            bcum = _dot(tri[d], la_hi) + _dot(tri[d], la_lo)
            ends = [bcum[(c + 1) * C - 1:(c + 1) * C] if fwd else bcum[c * C:c * C + 1]
                    for c in range(n_sub)]
            b_end = jnp.concatenate([jnp.broadcast_to(e, (C, GLA_KP)) for e in ends], axis=0)
            q_dec = (q * jnp.exp(bcum)).astype(BF16)
            k_dec = k * jnp.exp(-bcum)
            qd_scr[d, rows, :] = q_dec
            ke_scr[d, rows, :] = (k * jnp.exp(b_end - bcum)).astype(BF16)
            for c in range(n_sub):
                dec_scr[d, pl.ds(i * n_sub + c, 1), :] = jnp.exp(ends[c])
                k_c = k_dec[c * C:(c + 1) * C]
                k_bd = jnp.concatenate(
                    [jnp.where(k_head == h, k_c, 0.0) for h in range(GLA_HEADS)], axis=0).astype(BF16)
                scores.append(_dot_nt(q_dec[c * C:(c + 1) * C], k_bd))
        for d in range(2):
            causal = (si4 <= ti4) if d == 0 else (si4 >= ti4)
            for c in range(n_sub):
                a = jnp.where(causal, scores[d * n_sub + c], 0.0).astype(BF16)
                o_scr[d, pl.ds(r0 + c * C, C), :] = _dot(a, v_bd[c])
        return carry

    lax.fori_loop(0, T_ALL // TR, tile_body, 0, unroll=3)

    st_scr[...] = jnp.zeros_like(st_scr)
    n_all = N_CHUNKS_C + N_CHUNKS_X

    def step(i, carry):
        in_ctx = i < N_CHUNKS_C
        idx_f = jnp.where(in_ctx, N_CHUNKS_X + i, i - N_CHUNKS_C)
        idx_b = jnp.where(in_ctx, N_CHUNKS_X + N_CHUNKS_C - 1 - i, n_all - 1 - i)
        for d, idx in ((0, idx_f), (1, idx_b)):
            off = pl.ds(pl.multiple_of(idx * C, C), C)
            st = st_scr[d]
            o_scr[d, off, :] += _dot_nt(qd_scr[d, off, :], st.astype(BF16))
            half = D_GLA_V // 2
            ke = ke_scr[d, off, :]
            upd = jnp.concatenate([_dot(vt_scr[idx, :half, :], ke), _dot(vt_scr[idx, half:, :], ke)], axis=0)
            st_scr[d] = st * dec_scr[d, pl.ds(idx, 1), :] + jnp.where(st_mask, upd, 0.0)
        return carry

    lax.fori_loop(0, n_all, step, 0, unroll=4)

    head_ones = jnp.where(
        (lax.broadcasted_iota(jnp.int32, (D_GLA_V, 1), 0) // GLA_DV) == v_head, 1.0, 0.0).astype(BF16)
    tile = 256

    def fin(i, carry):
        r0 = pl.multiple_of(i * tile, tile)
        o = o_scr[0, pl.ds(r0, tile), :] + o_scr[1, pl.ds(r0, tile), :]
        sq_hi, sq_lo = _split_bf16(o * o)
        ms = (_dot(sq_hi, head_ones) + _dot(sq_lo, head_ones)) * (1.0 / GLA_DV)
        y = o * lax.rsqrt(ms + EPS) * nw_ref[...]
        gate = g_ref[0, pl.ds(r0, tile), :].astype(F32)
        o_ref[0, pl.ds(r0, tile), :] = (y * _silu(gate)).astype(BF16)
        return carry

    lax.fori_loop(0, T_ALL // tile, fin, 0)


def _gla(gq, gk, gv, gg, gz, wa, ba, gnw):
    nb = gq.shape[0]
    seq = lambda w: pl.BlockSpec((1, T_ALL, w), lambda b: (b, 0, 0))
    full = lambda a: pl.BlockSpec(a.shape, lambda b: (0,) * a.ndim)
    return pl.pallas_call(
        _gla_kernel,
        grid=(nb,),
        in_specs=[seq(GLA_KP), seq(GLA_KP), seq(D_GLA_V), seq(D_GLA_V), seq(Z_PAD),
                  full(wa), full(ba), full(gnw)],
        out_specs=seq(D_GLA_V),
        out_shape=jax.ShapeDtypeStruct((nb, T_ALL, D_GLA_V), BF16),
        scratch_shapes=[pltpu.VMEM((2, D_GLA_V, GLA_KP), F32),
                        pltpu.VMEM((2, T_ALL, D_GLA_V), F32),
                        pltpu.VMEM((2, T_ALL, GLA_KP), BF16),
                        pltpu.VMEM((2, T_ALL, GLA_KP), BF16),
                        pltpu.VMEM((T_ALL // GLA_CHUNK, D_GLA_V, GLA_CHUNK), BF16),
                        pltpu.VMEM((2, T_ALL // GLA_CHUNK, GLA_KP), F32)],

        compiler_params=pltpu.CompilerParams(
            dimension_semantics=("arbitrary",), vmem_limit_bytes=VMEM_LIMIT),
        name="gla_mix",
    )(gq, gk, gv, gg, gz, wa, ba, gnw)


def _outproj_kernel(h_ref, mf_ref, mn_ref, mg_ref, w_ref, mod_ref, nf_ref, o_ref, w_scr, *, tile, final):
    b = pl.program_id(0)
    j = pl.program_id(1)

    @pl.when((b == 0) & (j == 0))
    def _():
        for r in range(0, D_MODEL, 256):
            w_scr[r:r + 256, :] = w_ref[0, r:r + 256, :].astype(BF16)

    acc = (_dot(mf_ref[0], w_scr[:D_FOUR, :])
           + _dot(mn_ref[0], w_scr[D_FOUR:D_FOUR + D_NA, :])
           + _dot(mg_ref[0], w_scr[D_FOUR + D_NA:, :]))
    gate = _mod_rows(mod_ref, b, j * tile, tile, 2 * D_MODEL)
    x = h_ref[0] + gate * acc
    if final:
        ms = jnp.mean(x * x, axis=-1, keepdims=True)
        x = x * lax.rsqrt(ms + EPS) * nf_ref[...]
    o_ref[0] = x


def _outproj(h, mf, mn, mg, w_out, mod, l, nf, final):
    nb = h.shape[0]
    tile = 512 if final else TOK_TILE
    rows = SEQ if final else T_ALL
    tok = lambda width: pl.BlockSpec((1, tile, width), lambda b, j: (b, j, 0))
    layer = lambda a: pl.BlockSpec((1,) + a.shape[1:], lambda b, j: (l,) + (0,) * (a.ndim - 1))
    return pl.pallas_call(
        functools.partial(_outproj_kernel, tile=tile, final=final),
        grid=(nb, rows // tile),
        in_specs=[tok(D_MODEL), tok(D_FOUR), tok(D_NA), tok(D_GLA_V),
                  layer(w_out), layer(mod), pl.BlockSpec((1, D_MODEL), lambda b, j: (0, 0))],
        out_specs=tok(D_MODEL),
        out_shape=jax.ShapeDtypeStruct((nb, rows, D_MODEL), F32),
        scratch_shapes=[pltpu.VMEM((D_MODEL, D_MODEL), BF16)],
        compiler_params=pltpu.CompilerParams(
            dimension_semantics=("arbitrary", "arbitrary"), vmem_limit_bytes=VMEM_LIMIT),
        name="outproj_final" if final else "outproj",
    )(h, mf, mn, mg, w_out, mod, nf)


def _decay_weights(w_f, b_f, w_b, b_b):
    n_freq = GLA_DK // 4
    lane_pad = lambda a: jnp.zeros(a.shape[:-1] + (LANE - GLA_HEADS * 2 * n_freq,), a.dtype)
    lanes = lambda a: jnp.concatenate(
        [blk for offs in _qk_pieces() for blk in [a[..., o:o + n_freq] for o in offs] + [lane_pad(a)]],
        axis=-1)
    zero = jnp.zeros_like(lanes(w_f))
    top = jnp.concatenate([lanes(w_f), zero], axis=-1)
    bot = jnp.concatenate([zero, lanes(w_b)], axis=-1)
    pad = jnp.zeros((w_f.shape[0], Z_PAD - 2 * GLA_RANK, 2 * GLA_KP), w_f.dtype)
    wa = jnp.concatenate([top, bot, pad], axis=1).astype(BF16)
    ba = jnp.concatenate([lanes(b_f), lanes(b_b)], axis=-1)[:, None, :]
    return wa, ba


def kernel(x, c, ctx, c_ctx, w_ada, b_ada, norm_w, w_in, w_four, rpb, w_alpha_fwd, b_alpha_fwd,
           w_alpha_bwd, b_alpha_bwd, gla_norm_w, w_out, norm_f):
    nb = x.shape[0]
    depth = w_in.shape[0]
    assert x.shape == (nb, SEQ, D_MODEL) and ctx.shape == (nb, CTX_LEN, D_MODEL) and nb <= 8

    v0 = sum(IN_SIZES[:4])
    w_nvt = jnp.swapaxes(w_in[:, :, v0:v0 + D_NA], 1, 2).astype(BF16)
    w_nvt = jnp.pad(w_nvt.reshape(depth, NA_HEADS // 2, LANE, D_MODEL),
                    ((0, 0), (0, 0), (0, NA_VT_PAIR - LANE), (0, 0))).reshape(depth, NA_VT_ROWS, D_MODEL)
    w_four_b = w_four.astype(BF16)
    norm_w3 = norm_w.reshape(depth, 1, D_MODEL)
    wa, ba = _decay_weights(w_alpha_fwd, b_alpha_fwd, w_alpha_bwd, b_alpha_bwd)
    gnw = jnp.tile(gla_norm_w, (1, GLA_HEADS)).reshape(depth, 1, D_GLA_V)
    n_off = 2 * NA_KW - 1
    rpb_rev = rpb[..., ::-1]
    rpb_a = jnp.pad(rpb_rev, ((0, 0), (0, 0), (0, 1), (0, LANE - n_off)))
    rpb_b = jnp.pad(rpb_rev, ((0, 0), (0, 0), (0, 1), (GRID_W, LANE - GRID_W - n_off)))

    cos_t, sin_t = (jnp.asarray(t) for t in _rope_tables())
    c64, s64 = _dft_tables(FOUR_DIM)
    cl, sl = _dft_tables(SEQ)
    ccx, scx = _dft_tables(CTX_LEN)
    tabs = [jnp.asarray(t, dtype=F32).astype(BF16) for t in
            (_block_diag(c64, FOUR_GROUPS), _block_diag(s64, FOUR_GROUPS), cl, sl, ccx, scx)]

    cc = jnp.zeros((16, D_MODEL), F32).at[:nb].set(c).at[8].set(c_ctx)
    mod = _modulation(cc, w_ada, b_ada)

    h = jnp.concatenate([x, ctx], axis=1)
    for l in range(depth):
        final = l == depth - 1
        (p_four, nq, nk, nvt, ng, gq, gk, gv, gg, gz) = _inproj(
            h, mod, l, norm_w3, w_in, w_nvt, cos_t, sin_t)
        mf = _fourier(p_four, tabs, w_four_b[l])
        mn = _na(nq, nk, nvt, ng, rpb_a[l], rpb_b[l])
        mg = _gla(gq, gk, gv, gg, gz, wa[l], ba[l], gnw[l])
        h = _outproj(h, mf, mn, mg, w_out, mod, l, norm_f.reshape(1, D_MODEL), final)
    return h
```

```python
import functools

import numpy as np
import jax
import jax.numpy as jnp
from jax import lax
from jax.experimental import pallas as pl
from jax.experimental.pallas import tpu as pltpu

F32 = jnp.float32
BF16 = jnp.bfloat16

D_MODEL = 1024
SEQ = 2048
CTX_LEN = 256
T_ALL = SEQ + CTX_LEN
GRID_W = 64
GRID_H = SEQ // GRID_W
EPS = 1e-6

FOUR_GROUPS, FOUR_DIM = 4, 64
D_FOUR = FOUR_GROUPS * FOUR_DIM
NA_HEADS, NA_HEAD_DIM = 6, 64
D_NA = NA_HEADS * NA_HEAD_DIM
NA_KH, NA_KW = 8, 16
GLA_HEADS, GLA_DK, GLA_DV = 4, 48, 96
D_GLA_K = GLA_HEADS * GLA_DK
D_GLA_V = GLA_HEADS * GLA_DV
GLA_RANK = 16
GLA_TAU = 16.0
GLA_CHUNK = 64
ROPE_BASE = 10000.0
LOG2E = 1.4426950408889634

IN_SIZES = (D_FOUR, D_FOUR, D_NA, D_NA, D_NA, D_NA, D_GLA_K, D_GLA_K, D_GLA_V, D_GLA_V,
            GLA_RANK, GLA_RANK)
D_IN = sum(IN_SIZES)

LANE = 128
GLA_KP = 2 * LANE
Z_PAD = LANE
N_CHUNKS_X = SEQ // GLA_CHUNK
N_CHUNKS_C = CTX_LEN // GLA_CHUNK
GLA_TILE = 256

NA_QROWS = 4
NA_KROWS = NA_QROWS + NA_KH
NA_QB = NA_QROWS * GRID_W
NA_KB = NA_KROWS * GRID_W
NA_NBLK = GRID_H // NA_QROWS
NA_VT_PAIR = LANE + 16
NA_VT_ROWS = NA_HEADS // 2 * NA_VT_PAIR

TOK_TILE = 768
VMEM_LIMIT = 56 * 1024 * 1024

COL_FOUR = 0
COL_NA = COL_FOUR + 2 * D_FOUR
COL_GQ = COL_NA + 3 * D_NA
COL_GK = COL_GQ + GLA_KP
COL_GV = COL_GK + GLA_KP
COL_GG = COL_GV + D_GLA_V
COL_Z = COL_GG + D_GLA_V
N_COLS = COL_Z + Z_PAD


def _dot(a, b):
    return jnp.dot(a, b, preferred_element_type=F32)


def _dot_nt(a, b):
    return lax.dot_general(a, b, (((1,), (1,)), ((), ())), preferred_element_type=F32)


def _silu(x):
    return x / (1.0 + jnp.exp(-x))


def _reduce_rows(x, op):
    r, n = x.shape
    part = op(x.reshape(r // 8, 8, n), axis=0)
    return op(part, axis=0, keepdims=True)


def _split_bf16(x):
    hi = x.astype(BF16)
    lo = (x - hi.astype(F32)).astype(BF16)
    return hi, lo


def _qk_pieces():
    n_freq = GLA_DK // 4
    return [[h * GLA_DK + half * 2 * n_freq + part * n_freq
             for h in range(GLA_HEADS) for half in range(2)] for part in range(2)]


def _rope_tables():
    n_freq = GLA_DK // 4
    inv = ROPE_BASE ** (-np.arange(n_freq, dtype=np.float64) / n_freq)
    pos = np.arange(SEQ)
    prow, pcol = pos // GRID_W, pos % GRID_W
    cos = np.ones((T_ALL, LANE), np.float64)
    sin = np.zeros((T_ALL, LANE), np.float64)
    for h in range(GLA_HEADS):
        for half, p in enumerate((prow, pcol)):
            ang = p[:, None] * inv[None, :]
            lo = h * 2 * n_freq + half * n_freq
            cos[:SEQ, lo:lo + n_freq] = np.cos(ang)
            sin[:SEQ, lo:lo + n_freq] = np.sin(ang)
    return cos.astype(np.float32), sin.astype(np.float32)


def _dft_tables(n):
    idx = (np.arange(n)[:, None] * np.arange(n)[None, :]) % n
    ang = 2.0 * np.pi * idx / n
    return np.cos(ang) / np.sqrt(n), np.sin(ang) / np.sqrt(n)


def _block_diag(m, reps):
    n = m.shape[0]
    out = np.zeros((n * reps, n * reps), m.dtype)
    for r in range(reps):
        out[r * n:(r + 1) * n, r * n:(r + 1) * n] = m
    return out


def _na_row_offsets():
    out = []
    for r0 in (0, NA_QROWS, GRID_H - NA_QROWS):
        k0 = int(np.clip(r0 - NA_KH // 2, 0, GRID_H - NA_KROWS))
        per_type = []
        for a in range(NA_QROWS):
            r = r0 + a
            rs = int(np.clip(r - NA_KH // 2, 0, GRID_H - NA_KH))
            per_type.append([(k0 + j) - r + NA_KH - 1 if rs <= k0 + j < rs + NA_KH else None
                             for j in range(NA_KROWS)])
        out.append(per_type)
    return out


def _mod_kernel(c_ref, w_ref, b_ref, o_ref):
    c = c_ref[...]
    s_hi, s_lo = _split_bf16(_silu(c))
    w_hi, w_lo = _split_bf16(w_ref[0])
    acc = _dot(s_hi, w_hi) + _dot(s_lo, w_hi) + _dot(s_hi, w_lo)
    o_ref[0] = acc + b_ref[0]


def _modulation(cc, w_ada, b_ada):
    depth = w_ada.shape[0]
    n_mod = w_ada.shape[2]
    tile = D_MODEL
    return pl.pallas_call(
        _mod_kernel,
        grid=(depth, n_mod // tile),
        in_specs=[
            pl.BlockSpec((16, D_MODEL), lambda l, j: (0, 0)),
            pl.BlockSpec((1, D_MODEL, tile), lambda l, j: (l, 0, j)),
            pl.BlockSpec((1, 1, tile), lambda l, j: (l, 0, j)),
        ],
        out_specs=pl.BlockSpec((1, 16, tile), lambda l, j: (l, 0, j)),
        out_shape=jax.ShapeDtypeStruct((depth, 16, n_mod), F32),
        compiler_params=pltpu.CompilerParams(
            dimension_semantics=("arbitrary", "arbitrary"), vmem_limit_bytes=VMEM_LIMIT),
        name="adaln_mod",
    )(cc, w_ada, b_ada.reshape(depth, 1, n_mod))


def _mod_rows(mod_ref, b, row0, n_rows, col0):
    rows = row0 + lax.broadcasted_iota(jnp.int32, (n_rows, 1), 0)
    m_x = mod_ref[0, pl.ds(b, 1), col0:col0 + D_MODEL]
    m_c = mod_ref[0, 8:9, col0:col0 + D_MODEL]
    return jnp.where(rows >= SEQ, m_c, m_x)


def _fill_weight(w_ref, w_scr, dst0, src0, width, scale=None):
    rows = w_scr.shape[0]
    for c in range(0, width, LANE):
        n = min(LANE, width - c)
        blk = w_ref[0, :, src0 + c:src0 + c + n]
        if scale is not None:
            blk = blk * scale
        if n < LANE:
            blk = jnp.concatenate([blk, jnp.zeros((rows, LANE - n), F32)], axis=1)
        w_scr[:, dst0 + c:dst0 + c + LANE] = blk.astype(BF16)


def _inproj_kernel(h_ref, mod_ref, nw_ref, w_ref, wvt_ref, cos_ref, sin_ref,
                   four_ref, nq_ref, nk_ref, nvt_ref, ng_ref,
                   gq_ref, gk_ref, gv_ref, gg_ref, gz_ref, hx_scr, w_scr):
    b = pl.program_id(0)
    j = pl.program_id(1)

    @pl.when((b == 0) & (j == 0))
    def _():
        src = np.concatenate([[0], np.cumsum(IN_SIZES)])
        _fill_weight(w_ref, w_scr, COL_FOUR, src[0], 2 * D_FOUR)
        _fill_weight(w_ref, w_scr, COL_NA, src[2], D_NA, NA_HEAD_DIM ** -0.5 * LOG2E)
        _fill_weight(w_ref, w_scr, COL_NA + D_NA, src[3], D_NA)
        _fill_weight(w_ref, w_scr, COL_NA + 2 * D_NA, src[5], D_NA)
        n_freq = GLA_DK // 4
        pad = jnp.zeros((D_MODEL, LANE - GLA_HEADS * 2 * n_freq), F32)
        for dst0, src0 in ((COL_GQ, src[6]), (COL_GK, src[7])):
            for part, offs in enumerate(_qk_pieces()):
                tile = jnp.concatenate(
                    [w_ref[0, :, src0 + o:src0 + o + n_freq] for o in offs] + [pad], axis=1)
                w_scr[:, dst0 + part * LANE:dst0 + (part + 1) * LANE] = tile.astype(BF16)
        _fill_weight(w_ref, w_scr, COL_GV, src[8], D_GLA_V)
        _fill_weight(w_ref, w_scr, COL_GG, src[9], D_GLA_V)
        _fill_weight(w_ref, w_scr, COL_Z, src[10], 2 * GLA_RANK)

    x = h_ref[0]
    ms = jnp.mean(x * x, axis=-1, keepdims=True)
    y = x * lax.rsqrt(ms + EPS) * nw_ref[0]
    shift = _mod_rows(mod_ref, b, j * TOK_TILE, TOK_TILE, 0)
    scale = _mod_rows(mod_ref, b, j * TOK_TILE, TOK_TILE, D_MODEL)
    hx_scr[...] = (y * (1.0 + scale) + shift).astype(BF16)

    def proj(c0, width):
        return _dot(hx_scr[...], w_scr[:, c0:c0 + width])

    four_ref[0] = proj(COL_FOUR, 2 * D_FOUR).astype(BF16)
    nq_ref[0] = proj(COL_NA, D_NA).astype(BF16)
    nk_ref[0] = proj(COL_NA + D_NA, D_NA).astype(BF16)
    ones_rows = lax.broadcasted_iota(jnp.int32, (NA_VT_ROWS, 1), 0) % NA_VT_PAIR >= LANE
    nvt_ref[0] = (_dot_nt(wvt_ref[0], hx_scr[...]) + jnp.where(ones_rows, 1.0, 0.0)).astype(BF16)
    ng_ref[0] = proj(COL_NA + 2 * D_NA, D_NA).astype(BF16)
    gv_ref[0] = proj(COL_GV, D_GLA_V).astype(BF16)
    gg_ref[0] = proj(COL_GG, D_GLA_V).astype(BF16)
    gz_ref[0] = proj(COL_Z, Z_PAD)

    cos = cos_ref[...]
    sin = sin_ref[...]

    def rope(c0, out_ref, scale_out):
        u = proj(c0, GLA_KP)
        u1, u2 = u[:, :LANE], u[:, LANE:]
        out_ref[0, :, :LANE] = (u1 * cos - u2 * sin) * scale_out
        out_ref[0, :, LANE:] = (u1 * sin + u2 * cos) * scale_out

    rope(COL_GQ, gq_ref, GLA_DK ** -0.5)
    rope(COL_GK, gk_ref, 1.0)


def _inproj(h, mod, l, norm_w, w_in, w_nvt, cos_t, sin_t):
    nb = h.shape[0]
    nt = T_ALL // TOK_TILE
    tok = lambda width: pl.BlockSpec((1, TOK_TILE, width), lambda b, j: (b, j, 0))
    layer = lambda a: pl.BlockSpec((1,) + a.shape[1:], lambda b, j: (l,) + (0,) * (a.ndim - 1))
    out_widths = [(2 * D_FOUR, BF16), (D_NA, BF16), (D_NA, BF16), (None, BF16), (D_NA, BF16),
                  (GLA_KP, F32), (GLA_KP, F32), (D_GLA_V, BF16), (D_GLA_V, BF16), (Z_PAD, F32)]
    vt_spec = pl.BlockSpec((1, NA_VT_ROWS, TOK_TILE), lambda b, j: (b, 0, j))
    vt_shape = jax.ShapeDtypeStruct((nb, NA_VT_ROWS, T_ALL), BF16)
    return pl.pallas_call(
        _inproj_kernel,
        grid=(nb, nt),
        in_specs=[
            tok(D_MODEL),
            layer(mod),
            layer(norm_w),
            layer(w_in),
            layer(w_nvt),
            pl.BlockSpec((TOK_TILE, LANE), lambda b, j: (j, 0)),
            pl.BlockSpec((TOK_TILE, LANE), lambda b, j: (j, 0)),
        ],
        out_specs=[vt_spec if w is None else tok(w) for w, _ in out_widths],
        out_shape=[vt_shape if w is None else jax.ShapeDtypeStruct((nb, T_ALL, w), dt)
                   for w, dt in out_widths],
        scratch_shapes=[pltpu.VMEM((TOK_TILE, D_MODEL), BF16), pltpu.VMEM((D_MODEL, N_COLS), BF16)],
        compiler_params=pltpu.CompilerParams(
            dimension_semantics=("arbitrary", "arbitrary"), vmem_limit_bytes=VMEM_LIMIT),
        name="inproj",
    )(h, mod, norm_w, w_in, w_nvt, cos_t, sin_t)


def _fourier_kernel(p_ref, cbd_ref, sbd_ref, cl_ref, sl_ref, cc_ref, sc_ref, wf_ref, o_ref,
                    uc_scr, us_scr):
    u = p_ref[0, :, :D_FOUR]
    uc_scr[...] = _dot(u, cbd_ref[...]).astype(BF16)
    us_scr[...] = _dot(u, sbd_ref[...]).astype(BF16)

    def finish(f, r0, n):
        four = _dot(f.astype(BF16), wf_ref[...])
        gate = p_ref[0, pl.ds(r0, n), D_FOUR:].astype(F32)
        o_ref[0, pl.ds(r0, n), :] = (four * _silu(gate)).astype(BF16)

    tile = 256

    def body(i, carry):
        r0 = pl.multiple_of(i * tile, tile)
        f = (_dot(cl_ref[pl.ds(r0, tile), :], uc_scr[:SEQ, :])
             - _dot(sl_ref[pl.ds(r0, tile), :], us_scr[:SEQ, :]))
        finish(f, r0, tile)
        return carry

    lax.fori_loop(0, SEQ // tile, body, 0)
    f_c = _dot(cc_ref[...], uc_scr[SEQ:, :]) - _dot(sc_ref[...], us_scr[SEQ:, :])
    finish(f_c, SEQ, CTX_LEN)


def _fourier(p_four, tabs, wf):
    nb = p_four.shape[0]
    full = lambda a: pl.BlockSpec(a.shape, lambda b: (0,) * a.ndim)
    return pl.pallas_call(
        _fourier_kernel,
        grid=(nb,),
        in_specs=[pl.BlockSpec((1, T_ALL, 2 * D_FOUR), lambda b: (b, 0, 0))]
        + [full(t) for t in tabs] + [full(wf)],
        out_specs=pl.BlockSpec((1, T_ALL, D_FOUR), lambda b: (b, 0, 0)),
        out_shape=jax.ShapeDtypeStruct((nb, T_ALL, D_FOUR), BF16),
        scratch_shapes=[pltpu.VMEM((T_ALL, D_FOUR), BF16), pltpu.VMEM((T_ALL, D_FOUR), BF16)],
        compiler_params=pltpu.CompilerParams(
            dimension_semantics=("arbitrary",), vmem_limit_bytes=VMEM_LIMIT),
        name="fourier_mix",
    )(p_four, *tabs, wf)


def _na_build_bias(ra_ref, rb_ref, tbl_ref):
    w = GRID_W
    ck = lax.broadcasted_iota(jnp.int32, (w, 2 * w), 0)
    ln = lax.broadcasted_iota(jnp.int32, (w, 2 * w), 1)
    cq = ln % w
    cs = jnp.clip(cq - NA_KW // 2, 0, w - NA_KW)
    col_ok = (ck >= cs) & (ck < cs + NA_KW)
    left = ln < w
    neg = jnp.full((w, 2 * w), -jnp.inf, F32)
    offs = _na_row_offsets()
    cache = {}
    for typ in range(3):
        for ap in range(NA_QROWS // 2):
            for jr in range(NA_KROWS):
                d0 = offs[typ][2 * ap][jr]
                d1 = offs[typ][2 * ap + 1][jr]
                for h in range(NA_HEADS):
                    key = (h, d0, d1)
                    if key not in cache:
                        if d0 is None and d1 is None:
                            cache[key] = neg
                        else:
                            ok = col_ok
                            if d0 is None:
                                row = rb_ref[h, d1:d1 + 1, :]
                                ok = ok & jnp.logical_not(left)
                            elif d1 is None:
                                row = ra_ref[h, d0:d0 + 1, :]
                                ok = ok & left
                            else:
                                row = ra_ref[h, d0:d0 + 1, :] + rb_ref[h, d1:d1 + 1, :]
                            skew = pltpu.roll(jnp.broadcast_to(row * LOG2E, (w, 2 * w)),
                                              2 * w - (NA_KW - 1), 1, stride=1, stride_axis=0)
                            cache[key] = jnp.where(ok, skew, neg)
                    tbl_ref[h * 3 + typ, jr * w:(jr + 1) * w, ap * 2 * w:(ap + 1) * 2 * w] = cache[key]


def _na_kernel(q_ref, k_ref, vt_ref, g_ref, ra_ref, rb_ref, o_ref, bias_ref, s_scr):
    lane = lax.broadcasted_iota(jnp.int32, (1, LANE), 1)
    first_head_lanes = lane < NA_HEAD_DIM
    first_head_rows = lax.broadcasted_iota(jnp.int32, (LANE, 1), 0) < NA_HEAD_DIM
    heads = range(NA_HEADS)
    cols = [slice(h // 2 * LANE, (h // 2 + 1) * LANE) for h in heads]

    @pl.when(pl.program_id(0) == 0)
    def _():
        _na_build_bias(ra_ref, rb_ref, bias_ref)

    def block_geometry(j):
        if isinstance(j, int):
            k0 = min(max(j * NA_QROWS - NA_KH // 2, 0), GRID_H - NA_KROWS) * GRID_W
            return pl.ds(j * NA_QB, NA_QB), k0, (0 if j == 0 else 2 if j == NA_NBLK - 1 else 1)
        k0 = jnp.clip(j * NA_QROWS - NA_KH // 2, 0, GRID_H - NA_KROWS) * GRID_W
        typ = jnp.where(j == 0, 0, jnp.where(j == NA_NBLK - 1, 2, 1))
        return pl.ds(pl.multiple_of(j * NA_QB, NA_QB), NA_QB), pl.multiple_of(k0, NA_QB), typ

    def scores(j, buf, local):
        qrows, k0, typ = block_geometry(j)
        for h in heads:
            qp = q_ref[0, qrows, cols[h]]
            keep = first_head_lanes if h % 2 == 0 else jnp.logical_not(first_head_lanes)
            qm = jnp.where(keep, qp, jnp.zeros_like(qp))
            s_scr[buf, h, :CTX_LEN, :] = _dot_nt(k_ref[0, SEQ:, cols[h]], qm)
            if local:
                s_scr[buf, h, CTX_LEN:, :] = (
                    _dot_nt(k_ref[0, pl.ds(k0, NA_KB), cols[h]], qm) + bias_ref[h * 3 + typ])

    def finish(j, buf, local):
        qrows, k0, _ = block_geometry(j)
        n_keys = CTX_LEN + NA_KB if local else CTX_LEN
        e = []
        for h in heads:
            s = s_scr[buf, h, :n_keys, :]
            e.append(jnp.exp2((s - _reduce_rows(s, jnp.max)).astype(BF16)))
        for p in range(NA_HEADS // 2):
            vrows = slice(p * NA_VT_PAIR, (p + 1) * NA_VT_PAIR)
            acc = None
            for h in (2 * p, 2 * p + 1):
                o = _dot(vt_ref[0, vrows, SEQ:], e[h][:CTX_LEN])
                if local:
                    o = o + _dot(vt_ref[0, vrows, pl.ds(k0, NA_KB)], e[h][CTX_LEN:])
                o = o[:LANE] / o[LANE:LANE + 1]
                acc = o if h % 2 == 0 else jnp.where(first_head_rows, acc, o)
            gate = g_ref[0, qrows, cols[2 * p]].astype(F32)
            o_ref[0, qrows, cols[2 * p]] = (acc.T * _silu(gate)).astype(BF16)

    def pair_body(i, carry):
        scores(2 * i + 1, 1, True)
        finish(2 * i, 0, True)
        scores(2 * i + 2, 0, True)
        finish(2 * i + 1, 1, True)
        return carry

    scores(0, 0, True)
    lax.fori_loop(0, NA_NBLK // 2 - 1, pair_body, 0)
    scores(NA_NBLK - 1, 1, True)
    finish(NA_NBLK - 2, 0, True)
    scores(NA_NBLK, 0, False)
    finish(NA_NBLK - 1, 1, True)
    finish(NA_NBLK, 0, False)


def _na(nq, nk, nvt, ng, rpb_a, rpb_b):
    nb = nq.shape[0]
    seq = pl.BlockSpec((1, T_ALL, D_NA), lambda b: (b, 0, 0))
    seq_t = pl.BlockSpec((1, NA_VT_ROWS, T_ALL), lambda b: (b, 0, 0))
    tab = pl.BlockSpec(rpb_a.shape, lambda b: (0, 0, 0))
    return pl.pallas_call(
        _na_kernel,
        grid=(nb,),
        in_specs=[seq, seq, seq_t, seq, tab, tab],
        out_specs=seq,
        out_shape=jax.ShapeDtypeStruct((nb, T_ALL, D_NA), BF16),
        scratch_shapes=[pltpu.VMEM((NA_HEADS * 3, NA_KB, NA_QB), F32),
                        pltpu.VMEM((2, NA_HEADS, CTX_LEN + NA_KB, NA_QB), F32)],
        compiler_params=pltpu.CompilerParams(
            dimension_semantics=("arbitrary",), vmem_limit_bytes=VMEM_LIMIT),
        name="nbr_attention",
    )(nq, nk, nvt, ng, rpb_a, rpb_b)


def _gla_kernel(q_ref, k_ref, v_ref, g_ref, z_ref, wa_ref, ba_ref, nw_ref, o_ref,
                st_scr, o_scr, qd_scr, kd_scr, ke_scr, vt_scr, dec_scr):
    C = GLA_CHUNK
    TR = GLA_TILE
    n_sub = TR // C
    kl = lax.broadcasted_iota(jnp.int32, (1, GLA_KP), 1) % LANE
    k_head = jnp.where(kl < GLA_HEADS * (GLA_DK // 2), kl // (GLA_DK // 2), GLA_HEADS)
    v_head = lax.broadcasted_iota(jnp.int32, (1, D_GLA_V), 1) // GLA_DV
    st_mask = (lax.broadcasted_iota(jnp.int32, (D_GLA_V, 1), 0) // GLA_DV) == k_head
    ti4 = lax.broadcasted_iota(jnp.int32, (C, GLA_HEADS * C), 0)
    si4 = lax.broadcasted_iota(jnp.int32, (C, GLA_HEADS * C), 1) % C
    tr = lax.broadcasted_iota(jnp.int32, (TR, TR), 0)
    sr = lax.broadcasted_iota(jnp.int32, (TR, TR), 1)
    same_chunk = (tr // C) == (sr // C)
    tri = [jnp.where(same_chunk & (sr <= tr), 1.0, 0.0).astype(BF16),
           jnp.where(same_chunk & (sr >= tr), 1.0, 0.0).astype(BF16)]

    def tile_start(i):
        return i * TR if isinstance(i, int) else pl.multiple_of(i * TR, TR)

    def tile_rows(i):
        return pl.ds(tile_start(i), TR)

    def decay_sums(i):
        z = z_ref[0, tile_rows(i), :].astype(BF16)
        g = _dot(z, wa_ref[...]) + ba_ref[...]
        la = (jnp.minimum(g, 0.0) - jnp.log(1.0 + jnp.exp(-jnp.abs(g)))) * (1.0 / GLA_TAU)
        sums = []
        for d in range(2):
            la_hi, la_lo = _split_bf16(la[:, d * GLA_KP:(d + 1) * GLA_KP])
            sums.append(_dot(tri[d], la_hi) + _dot(tri[d], la_lo))
        return sums

    def decayed_operands(i, sums):
        rows = tile_rows(i)
        q = q_ref[0, rows, :]
        k = k_ref[0, rows, :]
        for d in range(2):
            bcum = sums[d]
            ends = [bcum[(c + 1) * C - 1:(c + 1) * C] if d == 0 else bcum[c * C:c * C + 1]
                    for c in range(n_sub)]
            b_end = jnp.concatenate([jnp.broadcast_to(e, (C, GLA_KP)) for e in ends], axis=0)
            qd_scr[d, rows, :] = (q * jnp.exp(bcum)).astype(BF16)
            kd_scr[d, rows, :] = (k * jnp.exp(-bcum)).astype(BF16)
            ke_scr[d, rows, :] = (k * jnp.exp(b_end - bcum)).astype(BF16)
            for c in range(n_sub):
                dec_scr[d, pl.ds(i * n_sub + c, 1), :] = jnp.exp(ends[c])

    def intra_chunk(i):
        r0 = tile_start(i)
        v_bd = []
        for c in range(n_sub):
            v_c = v_ref[0, pl.ds(r0 + c * C, C), :]
            v_bd.append(jnp.concatenate(
                [jnp.where(v_head == h, v_c, jnp.zeros_like(v_c)) for h in range(GLA_HEADS)], axis=0))
            vt_scr[i * n_sub + c] = v_c.astype(F32).T.astype(BF16)
        scores = []
        for d in range(2):
            for c in range(n_sub):
                crow = pl.ds(r0 + c * C, C)
                k_c = kd_scr[d, crow, :]
                k_bd = jnp.concatenate(
                    [jnp.where(k_head == h, k_c, jnp.zeros_like(k_c)) for h in range(GLA_HEADS)], axis=0)
                scores.append(_dot_nt(qd_scr[d, crow, :], k_bd))
        for d in range(2):
            causal = (si4 <= ti4) if d == 0 else (si4 >= ti4)
            for c in range(n_sub):
                a = jnp.where(causal, scores[d * n_sub + c], 0.0).astype(BF16)
                o_scr[d, pl.ds(r0 + c * C, C), :] = _dot(a, v_bd[c])

    def tile_body(i, carry):
        sums = decay_sums(i + 1)
        intra_chunk(i)
        decayed_operands(i + 1, sums)
        return carry

    n_tiles = T_ALL // TR
    decayed_operands(0, decay_sums(0))
    lax.fori_loop(0, n_tiles - 1, tile_body, 0, unroll=2)
    intra_chunk(n_tiles - 1)

    st_scr[...] = jnp.zeros_like(st_scr)
    n_all = N_CHUNKS_C + N_CHUNKS_X

    def step(i, carry):
        in_ctx = i < N_CHUNKS_C
        idx_f = jnp.where(in_ctx, N_CHUNKS_X + i, i - N_CHUNKS_C)
        idx_b = jnp.where(in_ctx, N_CHUNKS_X + N_CHUNKS_C - 1 - i, n_all - 1 - i)
        for d, idx in ((0, idx_f), (1, idx_b)):
            off = pl.ds(pl.multiple_of(idx * C, C), C)
            st = st_scr[d]
            o_scr[d, off, :] += _dot_nt(qd_scr[d, off, :], st.astype(BF16))
            half = D_GLA_V // 2
            ke = ke_scr[d, off, :]
            upd = jnp.concatenate([_dot(vt_scr[idx, :half, :], ke), _dot(vt_scr[idx, half:, :], ke)], axis=0)
            st_scr[d] = st * dec_scr[d, pl.ds(idx, 1), :] + jnp.where(st_mask, upd, 0.0)
        return carry

    lax.fori_loop(0, n_all, step, 0, unroll=4)

    head_ones = jnp.where(
        (lax.broadcasted_iota(jnp.int32, (D_GLA_V, 1), 0) // GLA_DV) == v_head, 1.0, 0.0).astype(BF16)
    tile = 256

    def fin(i, carry):
        r0 = pl.multiple_of(i * tile, tile)
        o = o_scr[0, pl.ds(r0, tile), :] + o_scr[1, pl.ds(r0, tile), :]
        sq_hi, sq_lo = _split_bf16(o * o)
        ms = (_dot(sq_hi, head_ones) + _dot(sq_lo, head_ones)) * (1.0 / GLA_DV)
        y = o * lax.rsqrt(ms + EPS) * nw_ref[...]
        gate = g_ref[0, pl.ds(r0, tile), :].astype(F32)
        o_ref[0, pl.ds(r0, tile), :] = (y * _silu(gate)).astype(BF16)
        return carry

    lax.fori_loop(0, T_ALL // tile, fin, 0, unroll=3)


def _gla(gq, gk, gv, gg, gz, wa, ba, gnw):
    nb = gq.shape[0]
    seq = lambda w: pl.BlockSpec((1, T_ALL, w), lambda b: (b, 0, 0))
    full = lambda a: pl.BlockSpec(a.shape, lambda b: (0,) * a.ndim)
    return pl.pallas_call(
        _gla_kernel,
        grid=(nb,),
        in_specs=[seq(GLA_KP), seq(GLA_KP), seq(D_GLA_V), seq(D_GLA_V), seq(Z_PAD),
                  full(wa), full(ba), full(gnw)],
        out_specs=seq(D_GLA_V),
        out_shape=jax.ShapeDtypeStruct((nb, T_ALL, D_GLA_V), BF16),
        scratch_shapes=[pltpu.VMEM((2, D_GLA_V, GLA_KP), F32),
                        pltpu.VMEM((2, T_ALL, D_GLA_V), F32),
                        pltpu.VMEM((2, T_ALL, GLA_KP), BF16),
                        pltpu.VMEM((2, T_ALL, GLA_KP), BF16),
                        pltpu.VMEM((2, T_ALL, GLA_KP), BF16),
                        pltpu.VMEM((T_ALL // GLA_CHUNK, D_GLA_V, GLA_CHUNK), BF16),
                        pltpu.VMEM((2, T_ALL // GLA_CHUNK, GLA_KP), F32)],

        compiler_params=pltpu.CompilerParams(
            dimension_semantics=("arbitrary",), vmem_limit_bytes=VMEM_LIMIT),
        name="gla_mix",
    )(gq, gk, gv, gg, gz, wa, ba, gnw)


def _outproj_kernel(h_ref, mf_ref, mn_ref, mg_ref, w_ref, mod_ref, nf_ref, o_ref, w_scr, *, tile, final):
    b = pl.program_id(0)
    j = pl.program_id(1)

    @pl.when((b == 0) & (j == 0))
    def _():
        for r in range(0, D_MODEL, 256):
            w_scr[r:r + 256, :] = w_ref[0, r:r + 256, :].astype(BF16)

    acc = (_dot(mf_ref[0], w_scr[:D_FOUR, :])
           + _dot(mn_ref[0], w_scr[D_FOUR:D_FOUR + D_NA, :])
           + _dot(mg_ref[0], w_scr[D_FOUR + D_NA:, :]))
    gate = _mod_rows(mod_ref, b, j * tile, tile, 2 * D_MODEL)
    x = h_ref[0] + gate * acc
    if final:
        ms = jnp.mean(x * x, axis=-1, keepdims=True)
        x = x * lax.rsqrt(ms + EPS) * nf_ref[...]
    o_ref[0] = x


def _outproj(h, mf, mn, mg, w_out, mod, l, nf, final):
    nb = h.shape[0]
    tile = 512 if final else TOK_TILE
    rows = SEQ if final else T_ALL
    tok = lambda width: pl.BlockSpec((1, tile, width), lambda b, j: (b, j, 0))
    layer = lambda a: pl.BlockSpec((1,) + a.shape[1:], lambda b, j: (l,) + (0,) * (a.ndim - 1))
    return pl.pallas_call(
        functools.partial(_outproj_kernel, tile=tile, final=final),
        grid=(nb, rows // tile),
        in_specs=[tok(D_MODEL), tok(D_FOUR), tok(D_NA), tok(D_GLA_V),
                  layer(w_out), layer(mod), pl.BlockSpec((1, D_MODEL), lambda b, j: (0, 0))],
        out_specs=tok(D_MODEL),
        out_shape=jax.ShapeDtypeStruct((nb, rows, D_MODEL), F32),
        scratch_shapes=[pltpu.VMEM((D_MODEL, D_MODEL), BF16)],
        compiler_params=pltpu.CompilerParams(
            dimension_semantics=("arbitrary", "arbitrary"), vmem_limit_bytes=VMEM_LIMIT),
        name="outproj_final" if final else "outproj",
    )(h, mf, mn, mg, w_out, mod, nf)


def _decay_weights(w_f, b_f, w_b, b_b):
    n_freq = GLA_DK // 4
    lane_pad = lambda a: jnp.zeros(a.shape[:-1] + (LANE - GLA_HEADS * 2 * n_freq,), a.dtype)
    lanes = lambda a: jnp.concatenate(
        [blk for offs in _qk_pieces() for blk in [a[..., o:o + n_freq] for o in offs] + [lane_pad(a)]],
        axis=-1)
    zero = jnp.zeros_like(lanes(w_f))
    top = jnp.concatenate([lanes(w_f), zero], axis=-1)
    bot = jnp.concatenate([zero, lanes(w_b)], axis=-1)
    pad = jnp.zeros((w_f.shape[0], Z_PAD - 2 * GLA_RANK, 2 * GLA_KP), w_f.dtype)
    wa = jnp.concatenate([top, bot, pad], axis=1).astype(BF16)
    ba = jnp.concatenate([lanes(b_f), lanes(b_b)], axis=-1)[:, None, :]
    return wa, ba


def kernel(x, c, ctx, c_ctx, w_ada, b_ada, norm_w, w_in, w_four, rpb, w_alpha_fwd, b_alpha_fwd,
           w_alpha_bwd, b_alpha_bwd, gla_norm_w, w_out, norm_f):
    nb = x.shape[0]
    depth = w_in.shape[0]
    assert x.shape == (nb, SEQ, D_MODEL) and ctx.shape == (nb, CTX_LEN, D_MODEL) and nb <= 8

    v0 = sum(IN_SIZES[:4])
    w_nvt = jnp.swapaxes(w_in[:, :, v0:v0 + D_NA], 1, 2).astype(BF16)
    w_nvt = jnp.pad(w_nvt.reshape(depth, NA_HEADS // 2, LANE, D_MODEL),
                    ((0, 0), (0, 0), (0, NA_VT_PAIR - LANE), (0, 0))).reshape(depth, NA_VT_ROWS, D_MODEL)
    w_four_b = w_four.astype(BF16)
    norm_w3 = norm_w.reshape(depth, 1, D_MODEL)
    wa, ba = _decay_weights(w_alpha_fwd, b_alpha_fwd, w_alpha_bwd, b_alpha_bwd)
    gnw = jnp.tile(gla_norm_w, (1, GLA_HEADS)).reshape(depth, 1, D_GLA_V)
    n_off = 2 * NA_KW - 1
    rpb_rev = rpb[..., ::-1]
    rpb_a = jnp.pad(rpb_rev, ((0, 0), (0, 0), (0, 1), (0, LANE - n_off)))
    rpb_b = jnp.pad(rpb_rev, ((0, 0), (0, 0), (0, 1), (GRID_W, LANE - GRID_W - n_off)))

    cos_t, sin_t = (jnp.asarray(t) for t in _rope_tables())
    c64, s64 = _dft_tables(FOUR_DIM)
    cl, sl = _dft_tables(SEQ)
    ccx, scx = _dft_tables(CTX_LEN)
    tabs = [jnp.asarray(t, dtype=F32).astype(BF16) for t in
            (_block_diag(c64, FOUR_GROUPS), _block_diag(s64, FOUR_GROUPS), cl, sl, ccx, scx)]

    cc = jnp.zeros((16, D_MODEL), F32).at[:nb].set(c).at[8].set(c_ctx)
    mod = _modulation(cc, w_ada, b_ada)

    h = jnp.concatenate([x, ctx], axis=1)
    for l in range(depth):
        final = l == depth - 1
        (p_four, nq, nk, nvt, ng, gq, gk, gv, gg, gz) = _inproj(
            h, mod, l, norm_w3, w_in, w_nvt, cos_t, sin_t)
        mf = _fourier(p_four, tabs, w_four_b[l])
        mn = _na(nq, nk, nvt, ng, rpb_a[l], rpb_b[l])
        mg = _gla(gq, gk, gv, gg, gz, wa[l], ba[l], gnw[l])
        h = _outproj(h, mf, mn, mg, w_out, mod, l, norm_f.reshape(1, D_MODEL), final)
    return h
```

```python
import functools

import numpy as np
import jax
import jax.numpy as jnp
from jax import lax
from jax.experimental import pallas as pl
from jax.experimental.pallas import tpu as pltpu

F32 = jnp.float32
BF16 = jnp.bfloat16

D_MODEL = 1024
SEQ = 2048
CTX_LEN = 256
T_ALL = SEQ + CTX_LEN
GRID_W = 64
GRID_H = SEQ // GRID_W
EPS = 1e-6

FOUR_GROUPS, FOUR_DIM = 4, 64
D_FOUR = FOUR_GROUPS * FOUR_DIM
NA_HEADS, NA_HEAD_DIM = 6, 64
D_NA = NA_HEADS * NA_HEAD_DIM
NA_KH, NA_KW = 8, 16
GLA_HEADS, GLA_DK, GLA_DV = 4, 48, 96
D_GLA_K = GLA_HEADS * GLA_DK
D_GLA_V = GLA_HEADS * GLA_DV
GLA_RANK = 16
GLA_TAU = 16.0
GLA_CHUNK = 64
ROPE_BASE = 10000.0
LOG2E = 1.4426950408889634

IN_SIZES = (D_FOUR, D_FOUR, D_NA, D_NA, D_NA, D_NA, D_GLA_K, D_GLA_K, D_GLA_V, D_GLA_V,
            GLA_RANK, GLA_RANK)
D_IN = sum(IN_SIZES)

LANE = 128
GLA_KP = 2 * LANE
Z_PAD = LANE
N_CHUNKS_X = SEQ // GLA_CHUNK
N_CHUNKS_C = CTX_LEN // GLA_CHUNK
GLA_TILE = 256

NA_QROWS = 4
NA_KROWS = NA_QROWS + NA_KH
NA_QB = NA_QROWS * GRID_W
NA_KB = NA_KROWS * GRID_W
NA_NBLK = GRID_H // NA_QROWS
NA_VT_PAIR = LANE + 16
NA_VT_ROWS = NA_HEADS // 2 * NA_VT_PAIR

TOK_TILE = 768
N_XTILES = SEQ // TOK_TILE
X_TAIL = SEQ - N_XTILES * TOK_TILE
VMEM_LIMIT = 56 * 1024 * 1024

COL_FOUR = 0
COL_NA = COL_FOUR + 2 * D_FOUR
COL_GQ = COL_NA + 3 * D_NA
COL_GK = COL_GQ + GLA_KP
COL_GV = COL_GK + GLA_KP
COL_GG = COL_GV + D_GLA_V
COL_Z = COL_GG + D_GLA_V
N_COLS = COL_Z + Z_PAD


def _dot(a, b):
    return jnp.dot(a, b, preferred_element_type=F32)


def _dot_nt(a, b):
    return lax.dot_general(a, b, (((1,), (1,)), ((), ())), preferred_element_type=F32)


def _silu(x):
    return x / (1.0 + jnp.exp(-x))


def _reduce_rows(x, op):
    r, n = x.shape
    part = op(x.reshape(r // 8, 8, n), axis=0)
    return op(part, axis=0, keepdims=True)


def _split_bf16(x):
    hi = x.astype(BF16)
    lo = (x - hi.astype(F32)).astype(BF16)
    return hi, lo


def _qk_pieces():
    n_freq = GLA_DK // 4
    return [[h * GLA_DK + half * 2 * n_freq + part * n_freq
             for h in range(GLA_HEADS) for half in range(2)] for part in range(2)]


def _rope_tables():
    n_freq = GLA_DK // 4
    inv = ROPE_BASE ** (-np.arange(n_freq, dtype=np.float64) / n_freq)
    pos = np.arange(SEQ)
    prow, pcol = pos // GRID_W, pos % GRID_W
    cos = np.ones((T_ALL, LANE), np.float64)
    sin = np.zeros((T_ALL, LANE), np.float64)
    for h in range(GLA_HEADS):
        for half, p in enumerate((prow, pcol)):
            ang = p[:, None] * inv[None, :]
            lo = h * 2 * n_freq + half * n_freq
            cos[:SEQ, lo:lo + n_freq] = np.cos(ang)
            sin[:SEQ, lo:lo + n_freq] = np.sin(ang)
    return cos.astype(np.float32), sin.astype(np.float32)


def _dft_tables(n):
    idx = (np.arange(n)[:, None] * np.arange(n)[None, :]) % n
    ang = 2.0 * np.pi * idx / n
    return np.cos(ang) / np.sqrt(n), np.sin(ang) / np.sqrt(n)


def _block_diag(m, reps):
    n = m.shape[0]
    out = np.zeros((n * reps, n * reps), m.dtype)
    for r in range(reps):
        out[r * n:(r + 1) * n, r * n:(r + 1) * n] = m
    return out


def _na_row_offsets():
    out = []
    for r0 in (0, NA_QROWS, GRID_H - NA_QROWS):
        k0 = int(np.clip(r0 - NA_KH // 2, 0, GRID_H - NA_KROWS))
        per_type = []
        for a in range(NA_QROWS):
            r = r0 + a
            rs = int(np.clip(r - NA_KH // 2, 0, GRID_H - NA_KH))
            per_type.append([(k0 + j) - r + NA_KH - 1 if rs <= k0 + j < rs + NA_KH else None
                             for j in range(NA_KROWS)])
        out.append(per_type)
    return out


def _mod_kernel(c_ref, w_ref, b_ref, o_ref):
    c = c_ref[...]
    s_hi, s_lo = _split_bf16(_silu(c))
    w_hi, w_lo = _split_bf16(w_ref[0])
    acc = _dot(s_hi, w_hi) + _dot(s_lo, w_hi) + _dot(s_hi, w_lo)
    o_ref[0] = acc + b_ref[0]


def _modulation(cc, w_ada, b_ada):
    depth = w_ada.shape[0]
    n_mod = w_ada.shape[2]
    tile = D_MODEL
    return pl.pallas_call(
        _mod_kernel,
        grid=(depth, n_mod // tile),
        in_specs=[
            pl.BlockSpec((16, D_MODEL), lambda l, j: (0, 0)),
            pl.BlockSpec((1, D_MODEL, tile), lambda l, j: (l, 0, j)),
            pl.BlockSpec((1, 1, tile), lambda l, j: (l, 0, j)),
        ],
        out_specs=pl.BlockSpec((1, 16, tile), lambda l, j: (l, 0, j)),
        out_shape=jax.ShapeDtypeStruct((depth, 16, n_mod), F32),
        compiler_params=pltpu.CompilerParams(
            dimension_semantics=("arbitrary", "arbitrary"), vmem_limit_bytes=VMEM_LIMIT),
        name="adaln_mod",
    )(cc, w_ada, b_ada.reshape(depth, 1, n_mod))


def _mod_rows(mod_ref, b, row0, n_rows, col0):
    rows = row0 + lax.broadcasted_iota(jnp.int32, (n_rows, 1), 0)
    m_x = mod_ref[0, pl.ds(b, 1), col0:col0 + D_MODEL]
    m_c = mod_ref[0, 8:9, col0:col0 + D_MODEL]
    return jnp.where(rows >= SEQ, m_c, m_x)


def _fill_weight(w_ref, w_scr, dst0, src0, width, scale=None):
    rows = w_scr.shape[0]
    for c in range(0, width, LANE):
        n = min(LANE, width - c)
        blk = w_ref[0, :, src0 + c:src0 + c + n]
        if scale is not None:
            blk = blk * scale
        if n < LANE:
            blk = jnp.concatenate([blk, jnp.zeros((rows, LANE - n), F32)], axis=1)
        w_scr[:, dst0 + c:dst0 + c + LANE] = blk.astype(BF16)


def _stream_specs(tile):
    return [pl.BlockSpec((1, tile, D_MODEL), lambda b, j: (b, j, 0))]


def _split_stream_specs():
    assert X_TAIL + CTX_LEN == TOK_TILE and SEQ % X_TAIL == 0
    return [pl.BlockSpec((1, TOK_TILE, D_MODEL), lambda b, j: (b, jnp.minimum(j, N_XTILES - 1), 0)),
            pl.BlockSpec((1, X_TAIL, D_MODEL), lambda b, j: (b, SEQ // X_TAIL - 1, 0)),
            pl.BlockSpec((1, CTX_LEN, D_MODEL), lambda b, j: (b, 0, 0))]


def _stream_tile(h_refs, j, tile_scr):
    if len(h_refs) == 1:
        return h_refs[0][0]
    x_ref, tail_ref, ctx_ref = h_refs

    @pl.when(j < N_XTILES)
    def _():
        tile_scr[...] = x_ref[0]

    @pl.when(j == N_XTILES)
    def _():
        tile_scr[:X_TAIL] = tail_ref[0]
        tile_scr[X_TAIL:] = ctx_ref[0]

    return tile_scr[...]


def _inproj_kernel(*refs, n_h):
    h_refs, refs = refs[:n_h], refs[n_h:]
    (mod_ref, nw_ref, w_ref, wvt_ref, cos_ref, sin_ref,
     four_ref, nq_ref, nk_ref, nvt_ref, ng_ref,
     gq_ref, gk_ref, gv_ref, gg_ref, gz_ref, hx_scr, w_scr) = refs[:18]
    b = pl.program_id(0)
    j = pl.program_id(1)

    @pl.when((b == 0) & (j == 0))
    def _():
        src = np.concatenate([[0], np.cumsum(IN_SIZES)])
        _fill_weight(w_ref, w_scr, COL_FOUR, src[0], 2 * D_FOUR)
        _fill_weight(w_ref, w_scr, COL_NA, src[2], D_NA, NA_HEAD_DIM ** -0.5 * LOG2E)
        _fill_weight(w_ref, w_scr, COL_NA + D_NA, src[3], D_NA)
        _fill_weight(w_ref, w_scr, COL_NA + 2 * D_NA, src[5], D_NA)
        n_freq = GLA_DK // 4
        pad = jnp.zeros((D_MODEL, LANE - GLA_HEADS * 2 * n_freq), F32)
        for dst0, src0 in ((COL_GQ, src[6]), (COL_GK, src[7])):
            for part, offs in enumerate(_qk_pieces()):
                tile = jnp.concatenate(
                    [w_ref[0, :, src0 + o:src0 + o + n_freq] for o in offs] + [pad], axis=1)
                w_scr[:, dst0 + part * LANE:dst0 + (part + 1) * LANE] = tile.astype(BF16)
        _fill_weight(w_ref, w_scr, COL_GV, src[8], D_GLA_V)
        _fill_weight(w_ref, w_scr, COL_GG, src[9], D_GLA_V)
        _fill_weight(w_ref, w_scr, COL_Z, src[10], 2 * GLA_RANK)

    x = _stream_tile(h_refs, j, refs[18] if n_h > 1 else None)
    ms = jnp.mean(x * x, axis=-1, keepdims=True)
    y = x * lax.rsqrt(ms + EPS) * nw_ref[0]
    shift = _mod_rows(mod_ref, b, j * TOK_TILE, TOK_TILE, 0)
    scale = _mod_rows(mod_ref, b, j * TOK_TILE, TOK_TILE, D_MODEL)
    hx_scr[...] = (y * (1.0 + scale) + shift).astype(BF16)

    def proj(c0, width):
        return _dot(hx_scr[...], w_scr[:, c0:c0 + width])

    four_ref[0] = proj(COL_FOUR, 2 * D_FOUR).astype(BF16)
    nq_ref[0] = proj(COL_NA, D_NA).astype(BF16)
    nk_ref[0] = proj(COL_NA + D_NA, D_NA).astype(BF16)
    ones_rows = lax.broadcasted_iota(jnp.int32, (NA_VT_ROWS, 1), 0) % NA_VT_PAIR >= LANE
    nvt_ref[0] = (_dot_nt(wvt_ref[0], hx_scr[...]) + jnp.where(ones_rows, 1.0, 0.0)).astype(BF16)
    ng_ref[0] = proj(COL_NA + 2 * D_NA, D_NA).astype(BF16)
    gv_ref[0] = proj(COL_GV, D_GLA_V).astype(BF16)
    gg_ref[0] = proj(COL_GG, D_GLA_V).astype(BF16)
    gz_ref[0] = proj(COL_Z, Z_PAD)

    cos = cos_ref[...]
    sin = sin_ref[...]

    def rope(c0, out_ref, scale_out):
        u = proj(c0, GLA_KP)
        u1, u2 = u[:, :LANE], u[:, LANE:]
        out_ref[0, :, :LANE] = (u1 * cos - u2 * sin) * scale_out
        out_ref[0, :, LANE:] = (u1 * sin + u2 * cos) * scale_out

    rope(COL_GQ, gq_ref, GLA_DK ** -0.5)
    rope(COL_GK, gk_ref, 1.0)


def _inproj(h_in, mod, l, norm_w, w_in, w_nvt, cos_t, sin_t):
    nb = h_in[0].shape[0]
    nt = T_ALL // TOK_TILE
    split = len(h_in) > 1
    tok = lambda width: pl.BlockSpec((1, TOK_TILE, width), lambda b, j: (b, j, 0))
    layer = lambda a: pl.BlockSpec((1,) + a.shape[1:], lambda b, j: (l,) + (0,) * (a.ndim - 1))
    out_widths = [(2 * D_FOUR, BF16), (D_NA, BF16), (D_NA, BF16), (None, BF16), (D_NA, BF16),
                  (GLA_KP, F32), (GLA_KP, F32), (D_GLA_V, BF16), (D_GLA_V, BF16), (Z_PAD, F32)]
    vt_spec = pl.BlockSpec((1, NA_VT_ROWS, TOK_TILE), lambda b, j: (b, 0, j))
    vt_shape = jax.ShapeDtypeStruct((nb, NA_VT_ROWS, T_ALL), BF16)
    return pl.pallas_call(
        functools.partial(_inproj_kernel, n_h=len(h_in)),
        grid=(nb, nt),
        in_specs=(_split_stream_specs() if split else _stream_specs(TOK_TILE)) + [
            layer(mod),
            layer(norm_w),
            layer(w_in),
            layer(w_nvt),
            pl.BlockSpec((TOK_TILE, LANE), lambda b, j: (j, 0)),
            pl.BlockSpec((TOK_TILE, LANE), lambda b, j: (j, 0)),
        ],
        out_specs=[vt_spec if w is None else tok(w) for w, _ in out_widths],
        out_shape=[vt_shape if w is None else jax.ShapeDtypeStruct((nb, T_ALL, w), dt)
                   for w, dt in out_widths],
        scratch_shapes=[pltpu.VMEM((TOK_TILE, D_MODEL), BF16), pltpu.VMEM((D_MODEL, N_COLS), BF16)]
        + ([pltpu.VMEM((TOK_TILE, D_MODEL), F32)] if split else []),
        compiler_params=pltpu.CompilerParams(
            dimension_semantics=("arbitrary", "arbitrary"), vmem_limit_bytes=VMEM_LIMIT),
        name="inproj",
    )(*h_in, mod, norm_w, w_in, w_nvt, cos_t, sin_t)


def _cast_kernel(a_ref, b_ref, oa_ref, ob_ref):
    oa_ref[...] = a_ref[...].astype(BF16)
    ob_ref[...] = b_ref[...].astype(BF16)


def _cast_tables(a, b):
    rows, cols = a.shape
    tile = 256
    spec = pl.BlockSpec((tile, cols), lambda i: (i, 0))
    return pl.pallas_call(
        _cast_kernel,
        grid=(rows // tile,),
        in_specs=[spec, spec],
        out_specs=[spec, spec],
        out_shape=[jax.ShapeDtypeStruct(a.shape, BF16)] * 2,
        compiler_params=pltpu.CompilerParams(dimension_semantics=("arbitrary",)),
        name="cast_tables",
    )(a, b)


def _fourier_kernel(p_ref, cbd_ref, sbd_ref, cl_ref, sl_ref, cc_ref, sc_ref, wf_ref, o_ref,
                    uc_scr, us_scr):
    u = p_ref[0, :, :D_FOUR]
    uc_scr[...] = _dot(u, cbd_ref[...]).astype(BF16)
    us_scr[...] = _dot(u, sbd_ref[...]).astype(BF16)

    def finish(f, r0, n):
        four = _dot(f.astype(BF16), wf_ref[...])
        gate = p_ref[0, pl.ds(r0, n), D_FOUR:].astype(F32)
        o_ref[0, pl.ds(r0, n), :] = (four * _silu(gate)).astype(BF16)

    tile = 256

    def body(i, carry):
        r0 = pl.multiple_of(i * tile, tile)
        f = (_dot(cl_ref[pl.ds(r0, tile), :], uc_scr[:SEQ, :])
             - _dot(sl_ref[pl.ds(r0, tile), :], us_scr[:SEQ, :]))
        finish(f, r0, tile)
        return carry

    lax.fori_loop(0, SEQ // tile, body, 0)
    f_c = _dot(cc_ref[...], uc_scr[SEQ:, :]) - _dot(sc_ref[...], us_scr[SEQ:, :])
    finish(f_c, SEQ, CTX_LEN)


def _fourier(p_four, tabs, wf):
    nb = p_four.shape[0]
    full = lambda a: pl.BlockSpec(a.shape, lambda b: (0,) * a.ndim)
    return pl.pallas_call(
        _fourier_kernel,
        grid=(nb,),
        in_specs=[pl.BlockSpec((1, T_ALL, 2 * D_FOUR), lambda b: (b, 0, 0))]
        + [full(t) for t in tabs] + [full(wf)],
        out_specs=pl.BlockSpec((1, T_ALL, D_FOUR), lambda b: (b, 0, 0)),
        out_shape=jax.ShapeDtypeStruct((nb, T_ALL, D_FOUR), BF16),
        scratch_shapes=[pltpu.VMEM((T_ALL, D_FOUR), BF16), pltpu.VMEM((T_ALL, D_FOUR), BF16)],
        compiler_params=pltpu.CompilerParams(
            dimension_semantics=("arbitrary",), vmem_limit_bytes=VMEM_LIMIT),
        name="fourier_mix",
    )(p_four, *tabs, wf)


def _na_build_bias(ra_ref, rb_ref, tbl_ref):
    w = GRID_W
    ck = lax.broadcasted_iota(jnp.int32, (w, 2 * w), 0)
    ln = lax.broadcasted_iota(jnp.int32, (w, 2 * w), 1)
    cq = ln % w
    cs = jnp.clip(cq - NA_KW // 2, 0, w - NA_KW)
    col_ok = (ck >= cs) & (ck < cs + NA_KW)
    left = ln < w
    neg = jnp.full((w, 2 * w), -jnp.inf, F32)
    offs = _na_row_offsets()
    cache = {}
    for typ in range(3):
        for ap in range(NA_QROWS // 2):
            for jr in range(NA_KROWS):
                d0 = offs[typ][2 * ap][jr]
                d1 = offs[typ][2 * ap + 1][jr]
                for h in range(NA_HEADS):
                    key = (h, d0, d1)
                    if key not in cache:
                        if d0 is None and d1 is None:
                            cache[key] = neg
                        else:
                            ok = col_ok
                            if d0 is None:
                                row = rb_ref[h, d1:d1 + 1, :]
                                ok = ok & jnp.logical_not(left)
                            elif d1 is None:
                                row = ra_ref[h, d0:d0 + 1, :]
                                ok = ok & left
                            else:
                                row = ra_ref[h, d0:d0 + 1, :] + rb_ref[h, d1:d1 + 1, :]
                            skew = pltpu.roll(jnp.broadcast_to(row * LOG2E, (w, 2 * w)),
                                              2 * w - (NA_KW - 1), 1, stride=1, stride_axis=0)
                            cache[key] = jnp.where(ok, skew, neg)
                    tbl_ref[h * 3 + typ, jr * w:(jr + 1) * w, ap * 2 * w:(ap + 1) * 2 * w] = cache[key]


def _na_kernel(q_ref, k_ref, vt_ref, g_ref, ra_ref, rb_ref, o_ref, bias_ref, s_scr):
    lane = lax.broadcasted_iota(jnp.int32, (1, LANE), 1)
    first_head_lanes = lane < NA_HEAD_DIM
    first_head_rows = lax.broadcasted_iota(jnp.int32, (LANE, 1), 0) < NA_HEAD_DIM
    heads = range(NA_HEADS)
    cols = [slice(h // 2 * LANE, (h // 2 + 1) * LANE) for h in heads]

    @pl.when(pl.program_id(0) == 0)
    def _():
        _na_build_bias(ra_ref, rb_ref, bias_ref)

    def block_geometry(j):
        if isinstance(j, int):
            k0 = min(max(j * NA_QROWS - NA_KH // 2, 0), GRID_H - NA_KROWS) * GRID_W
            return pl.ds(j * NA_QB, NA_QB), k0, (0 if j == 0 else 2 if j == NA_NBLK - 1 else 1)
        k0 = jnp.clip(j * NA_QROWS - NA_KH // 2, 0, GRID_H - NA_KROWS) * GRID_W
        typ = jnp.where(j == 0, 0, jnp.where(j == NA_NBLK - 1, 2, 1))
        return pl.ds(pl.multiple_of(j * NA_QB, NA_QB), NA_QB), pl.multiple_of(k0, NA_QB), typ

    def scores(j, buf, local):
        qrows, k0, typ = block_geometry(j)
        for h in heads:
            qp = q_ref[0, qrows, cols[h]]
            keep = first_head_lanes if h % 2 == 0 else jnp.logical_not(first_head_lanes)
            qm = jnp.where(keep, qp, jnp.zeros_like(qp))
            s_scr[buf, h, :CTX_LEN, :] = _dot_nt(k_ref[0, SEQ:, cols[h]], qm)
            if local:
                s_scr[buf, h, CTX_LEN:, :] = (
                    _dot_nt(k_ref[0, pl.ds(k0, NA_KB), cols[h]], qm) + bias_ref[h * 3 + typ])

    def finish(j, buf, local):
        qrows, k0, _ = block_geometry(j)
        n_keys = CTX_LEN + NA_KB if local else CTX_LEN
        e = []
        for h in heads:
            s = s_scr[buf, h, :n_keys, :]
            e.append(jnp.exp2((s - _reduce_rows(s, jnp.max)).astype(BF16)))
        for p in range(NA_HEADS // 2):
            vrows = slice(p * NA_VT_PAIR, (p + 1) * NA_VT_PAIR)
            acc = None
            for h in (2 * p, 2 * p + 1):
                o = _dot(vt_ref[0, vrows, SEQ:], e[h][:CTX_LEN])
                if local:
                    o = o + _dot(vt_ref[0, vrows, pl.ds(k0, NA_KB)], e[h][CTX_LEN:])
                o = o[:LANE] / o[LANE:LANE + 1]
                acc = o if h % 2 == 0 else jnp.where(first_head_rows, acc, o)
            gate = g_ref[0, qrows, cols[2 * p]].astype(F32)
            o_ref[0, qrows, cols[2 * p]] = (acc.T * _silu(gate)).astype(BF16)

    def pair_body(i, carry):
        scores(2 * i + 1, 1, True)
        finish(2 * i, 0, True)
        scores(2 * i + 2, 0, True)
        finish(2 * i + 1, 1, True)
        return carry

    scores(0, 0, True)
    lax.fori_loop(0, NA_NBLK // 2 - 1, pair_body, 0)
    scores(NA_NBLK - 1, 1, True)
    finish(NA_NBLK - 2, 0, True)
    scores(NA_NBLK, 0, False)
    finish(NA_NBLK - 1, 1, True)
    finish(NA_NBLK, 0, False)


def _na(nq, nk, nvt, ng, rpb_a, rpb_b):
    nb = nq.shape[0]
    seq = pl.BlockSpec((1, T_ALL, D_NA), lambda b: (b, 0, 0))
    seq_t = pl.BlockSpec((1, NA_VT_ROWS, T_ALL), lambda b: (b, 0, 0))
    tab = pl.BlockSpec(rpb_a.shape, lambda b: (0, 0, 0))
    return pl.pallas_call(
        _na_kernel,
        grid=(nb,),
        in_specs=[seq, seq, seq_t, seq, tab, tab],
        out_specs=seq,
        out_shape=jax.ShapeDtypeStruct((nb, T_ALL, D_NA), BF16),
        scratch_shapes=[pltpu.VMEM((NA_HEADS * 3, NA_KB, NA_QB), F32),
                        pltpu.VMEM((2, NA_HEADS, CTX_LEN + NA_KB, NA_QB), F32)],
        compiler_params=pltpu.CompilerParams(
            dimension_semantics=("arbitrary",), vmem_limit_bytes=VMEM_LIMIT),
        name="nbr_attention",
    )(nq, nk, nvt, ng, rpb_a, rpb_b)


def _gla_kernel(q_ref, k_ref, v_ref, g_ref, z_ref, wa_ref, ba_ref, nw_ref, o_ref,
                st_scr, o_scr, qd_scr, kd_scr, ke_scr, vt_scr, dec_scr):
    C = GLA_CHUNK
    TR = GLA_TILE
    n_sub = TR // C
    kl = lax.broadcasted_iota(jnp.int32, (1, GLA_KP), 1) % LANE
    k_head = jnp.where(kl < GLA_HEADS * (GLA_DK // 2), kl // (GLA_DK // 2), GLA_HEADS)
    v_head = lax.broadcasted_iota(jnp.int32, (1, D_GLA_V), 1) // GLA_DV
    st_mask = (lax.broadcasted_iota(jnp.int32, (D_GLA_V, 1), 0) // GLA_DV) == k_head
    ti4 = lax.broadcasted_iota(jnp.int32, (C, GLA_HEADS * C), 0)
    si4 = lax.broadcasted_iota(jnp.int32, (C, GLA_HEADS * C), 1) % C
    tr = lax.broadcasted_iota(jnp.int32, (TR, TR), 0)
    sr = lax.broadcasted_iota(jnp.int32, (TR, TR), 1)
    same_chunk = (tr // C) == (sr // C)
    tri = [jnp.where(same_chunk & (sr <= tr), 1.0, 0.0).astype(BF16),
           jnp.where(same_chunk & (sr >= tr), 1.0, 0.0).astype(BF16)]

    def tile_start(i):
        return i * TR if isinstance(i, int) else pl.multiple_of(i * TR, TR)

    def tile_rows(i):
        return pl.ds(tile_start(i), TR)

    def decay_sums(i):
        z = z_ref[0, tile_rows(i), :].astype(BF16)
        g = _dot(z, wa_ref[...]) + ba_ref[...]
        la = (jnp.minimum(g, 0.0) - jnp.log(1.0 + jnp.exp(-jnp.abs(g)))) * (1.0 / GLA_TAU)
        sums = []
        for d in range(2):
            la_hi, la_lo = _split_bf16(la[:, d * GLA_KP:(d + 1) * GLA_KP])
            sums.append(_dot(tri[d], la_hi) + _dot(tri[d], la_lo))
        return sums

    def decayed_operands(i, sums):
        rows = tile_rows(i)
        q = q_ref[0, rows, :]
        k = k_ref[0, rows, :]
        for d in range(2):
            bcum = sums[d]
            ends = [bcum[(c + 1) * C - 1:(c + 1) * C] if d == 0 else bcum[c * C:c * C + 1]
                    for c in range(n_sub)]
            b_end = jnp.concatenate([jnp.broadcast_to(e, (C, GLA_KP)) for e in ends], axis=0)
            qd_scr[d, rows, :] = (q * jnp.exp(bcum)).astype(BF16)
            kd_scr[d, rows, :] = (k * jnp.exp(-bcum)).astype(BF16)
            ke_scr[d, rows, :] = (k * jnp.exp(b_end - bcum)).astype(BF16)
            for c in range(n_sub):
                dec_scr[d, pl.ds(i * n_sub + c, 1), :] = jnp.exp(ends[c])

    def intra_chunk(i):
        r0 = tile_start(i)
        v_bd = []
        for c in range(n_sub):
            v_c = v_ref[0, pl.ds(r0 + c * C, C), :]
            v_bd.append(jnp.concatenate(
                [jnp.where(v_head == h, v_c, jnp.zeros_like(v_c)) for h in range(GLA_HEADS)], axis=0))
            vt_scr[i * n_sub + c] = v_c.astype(F32).T.astype(BF16)
        scores = []
        for d in range(2):
            for c in range(n_sub):
                crow = pl.ds(r0 + c * C, C)
                k_c = kd_scr[d, crow, :]
                k_bd = jnp.concatenate(
                    [jnp.where(k_head == h, k_c, jnp.zeros_like(k_c)) for h in range(GLA_HEADS)], axis=0)
                scores.append(_dot_nt(qd_scr[d, crow, :], k_bd))
        for d in range(2):
            causal = (si4 <= ti4) if d == 0 else (si4 >= ti4)
            for c in range(n_sub):
                a = jnp.where(causal, scores[d * n_sub + c], 0.0).astype(BF16)
                o_scr[d, pl.ds(r0 + c * C, C), :] = _dot(a, v_bd[c])

    def tile_body(i, carry):
        sums = decay_sums(i + 1)
        intra_chunk(i)
        decayed_operands(i + 1, sums)
        return carry

    n_tiles = T_ALL // TR
    decayed_operands(0, decay_sums(0))
    lax.fori_loop(0, n_tiles - 1, tile_body, 0, unroll=2)
    intra_chunk(n_tiles - 1)

    st_scr[...] = jnp.zeros_like(st_scr)
    n_all = N_CHUNKS_C + N_CHUNKS_X

    def step(i, carry):
        in_ctx = i < N_CHUNKS_C
        idx_f = jnp.where(in_ctx, N_CHUNKS_X + i, i - N_CHUNKS_C)
        idx_b = jnp.where(in_ctx, N_CHUNKS_X + N_CHUNKS_C - 1 - i, n_all - 1 - i)
        half = D_GLA_V // 2
        dirs = ((0, idx_f), (1, idx_b))
        offs = [pl.ds(pl.multiple_of(idx * C, C), C) for _, idx in dirs]
        upd = []
        for (d, idx), off in zip(dirs, offs):
            ke = ke_scr[d, off, :]
            upd.append(jnp.concatenate(
                [_dot(vt_scr[idx, :half, :], ke), _dot(vt_scr[idx, half:, :], ke)], axis=0))
        for (d, idx), off in zip(dirs, offs):
            st = st_scr[d]
            o_scr[d, off, :] += _dot_nt(qd_scr[d, off, :], st.astype(BF16))
            st_scr[d] = st * dec_scr[d, pl.ds(idx, 1), :] + jnp.where(st_mask, upd[d], 0.0)
        return carry

    lax.fori_loop(0, n_all, step, 0, unroll=4)

    head_ones = jnp.where(
        (lax.broadcasted_iota(jnp.int32, (D_GLA_V, 1), 0) // GLA_DV) == v_head, 1.0, 0.0).astype(BF16)
    tile = 256

    def fin(i, carry):
        r0 = pl.multiple_of(i * tile, tile)
        o = o_scr[0, pl.ds(r0, tile), :] + o_scr[1, pl.ds(r0, tile), :]
        sq_hi, sq_lo = _split_bf16(o * o)
        ms = (_dot(sq_hi, head_ones) + _dot(sq_lo, head_ones)) * (1.0 / GLA_DV)
        y = o * lax.rsqrt(ms + EPS) * nw_ref[...]
        gate = g_ref[0, pl.ds(r0, tile), :].astype(F32)
        o_ref[0, pl.ds(r0, tile), :] = (y * _silu(gate)).astype(BF16)
        return carry

    lax.fori_loop(0, T_ALL // tile, fin, 0, unroll=3)


def _gla(gq, gk, gv, gg, gz, wa, ba, gnw):
    nb = gq.shape[0]
    seq = lambda w: pl.BlockSpec((1, T_ALL, w), lambda b: (b, 0, 0))
    full = lambda a: pl.BlockSpec(a.shape, lambda b: (0,) * a.ndim)
    return pl.pallas_call(
        _gla_kernel,
        grid=(nb,),
        in_specs=[seq(GLA_KP), seq(GLA_KP), seq(D_GLA_V), seq(D_GLA_V), seq(Z_PAD),
                  full(wa), full(ba), full(gnw)],
        out_specs=seq(D_GLA_V),
        out_shape=jax.ShapeDtypeStruct((nb, T_ALL, D_GLA_V), BF16),
        scratch_shapes=[pltpu.VMEM((2, D_GLA_V, GLA_KP), F32),
                        pltpu.VMEM((2, T_ALL, D_GLA_V), F32),
                        pltpu.VMEM((2, T_ALL, GLA_KP), BF16),
                        pltpu.VMEM((2, T_ALL, GLA_KP), BF16),
                        pltpu.VMEM((2, T_ALL, GLA_KP), BF16),
                        pltpu.VMEM((T_ALL // GLA_CHUNK, D_GLA_V, GLA_CHUNK), BF16),
                        pltpu.VMEM((2, T_ALL // GLA_CHUNK, GLA_KP), F32)],

        compiler_params=pltpu.CompilerParams(
            dimension_semantics=("arbitrary",), vmem_limit_bytes=VMEM_LIMIT),
        name="gla_mix",
    )(gq, gk, gv, gg, gz, wa, ba, gnw)


def _outproj_kernel(*refs, n_h, tile, final):
    h_refs, refs = refs[:n_h], refs[n_h:]
    mf_ref, mn_ref, mg_ref, w_ref, mod_ref, nf_ref, o_ref, w_scr = refs[:8]
    b = pl.program_id(0)
    j = pl.program_id(1)

    @pl.when((b == 0) & (j == 0))
    def _():
        for r in range(0, D_MODEL, 256):
            w_scr[r:r + 256, :] = w_ref[0, r:r + 256, :].astype(BF16)

    acc = (_dot(mf_ref[0], w_scr[:D_FOUR, :])
           + _dot(mn_ref[0], w_scr[D_FOUR:D_FOUR + D_NA, :])
           + _dot(mg_ref[0], w_scr[D_FOUR + D_NA:, :]))
    gate = _mod_rows(mod_ref, b, j * tile, tile, 2 * D_MODEL)
    x = _stream_tile(h_refs, j, refs[8] if n_h > 1 else None) + gate * acc
    if final:
        ms = jnp.mean(x * x, axis=-1, keepdims=True)
        x = x * lax.rsqrt(ms + EPS) * nf_ref[...]
    o_ref[0] = x


def _outproj(h_in, mf, mn, mg, w_out, mod, l, nf, final):
    nb = h_in[0].shape[0]
    tile = 512 if final else TOK_TILE
    rows = SEQ if final else T_ALL
    if final:
        h_in = h_in[:1]
    split = len(h_in) > 1
    tok = lambda width: pl.BlockSpec((1, tile, width), lambda b, j: (b, j, 0))
    layer = lambda a: pl.BlockSpec((1,) + a.shape[1:], lambda b, j: (l,) + (0,) * (a.ndim - 1))
    return pl.pallas_call(
        functools.partial(_outproj_kernel, n_h=len(h_in), tile=tile, final=final),
        grid=(nb, rows // tile),
        in_specs=(_split_stream_specs() if split else _stream_specs(tile))
        + [tok(D_FOUR), tok(D_NA), tok(D_GLA_V),
           layer(w_out), layer(mod), pl.BlockSpec((1, D_MODEL), lambda b, j: (0, 0))],
        out_specs=tok(D_MODEL),
        out_shape=jax.ShapeDtypeStruct((nb, rows, D_MODEL), F32),
        scratch_shapes=[pltpu.VMEM((D_MODEL, D_MODEL), BF16)]
        + ([pltpu.VMEM((TOK_TILE, D_MODEL), F32)] if split else []),
        compiler_params=pltpu.CompilerParams(
            dimension_semantics=("arbitrary", "arbitrary"), vmem_limit_bytes=VMEM_LIMIT),
        name="outproj_final" if final else "outproj",
    )(*h_in, mf, mn, mg, w_out, mod, nf)


def _decay_weights(w_f, b_f, w_b, b_b):
    n_freq = GLA_DK // 4
    lane_pad = lambda a: jnp.zeros(a.shape[:-1] + (LANE - GLA_HEADS * 2 * n_freq,), a.dtype)
    lanes = lambda a: jnp.concatenate(
        [blk for offs in _qk_pieces() for blk in [a[..., o:o + n_freq] for o in offs] + [lane_pad(a)]],
        axis=-1)
    zero = jnp.zeros_like(lanes(w_f))
    top = jnp.concatenate([lanes(w_f), zero], axis=-1)
    bot = jnp.concatenate([zero, lanes(w_b)], axis=-1)
    pad = jnp.zeros((w_f.shape[0], Z_PAD - 2 * GLA_RANK, 2 * GLA_KP), w_f.dtype)
    wa = jnp.concatenate([top, bot, pad], axis=1).astype(BF16)
    ba = jnp.concatenate([lanes(b_f), lanes(b_b)], axis=-1)[:, None, :]
    return wa, ba


def kernel(x, c, ctx, c_ctx, w_ada, b_ada, norm_w, w_in, w_four, rpb, w_alpha_fwd, b_alpha_fwd,
           w_alpha_bwd, b_alpha_bwd, gla_norm_w, w_out, norm_f):
    nb = x.shape[0]
    depth = w_in.shape[0]
    assert x.shape == (nb, SEQ, D_MODEL) and ctx.shape == (nb, CTX_LEN, D_MODEL) and nb <= 8

    v0 = sum(IN_SIZES[:4])
    w_nvt = jnp.swapaxes(w_in[:, :, v0:v0 + D_NA], 1, 2).astype(BF16)
    w_nvt = jnp.pad(w_nvt.reshape(depth, NA_HEADS // 2, LANE, D_MODEL),
                    ((0, 0), (0, 0), (0, NA_VT_PAIR - LANE), (0, 0))).reshape(depth, NA_VT_ROWS, D_MODEL)
    w_four_b = w_four.astype(BF16)
    norm_w3 = norm_w.reshape(depth, 1, D_MODEL)
    wa, ba = _decay_weights(w_alpha_fwd, b_alpha_fwd, w_alpha_bwd, b_alpha_bwd)
    gnw = jnp.tile(gla_norm_w, (1, GLA_HEADS)).reshape(depth, 1, D_GLA_V)
    n_off = 2 * NA_KW - 1
    rpb_rev = rpb[..., ::-1]
    rpb_a = jnp.pad(rpb_rev, ((0, 0), (0, 0), (0, 1), (0, LANE - n_off)))
    rpb_b = jnp.pad(rpb_rev, ((0, 0), (0, 0), (0, 1), (GRID_W, LANE - GRID_W - n_off)))

    cos_t, sin_t = (jnp.asarray(t) for t in _rope_tables())
    c64, s64 = _dft_tables(FOUR_DIM)
    cl, sl = _dft_tables(SEQ)
    ccx, scx = _dft_tables(CTX_LEN)
    small = [jnp.asarray(t, dtype=F32).astype(BF16) for t in
             (_block_diag(c64, FOUR_GROUPS), _block_diag(s64, FOUR_GROUPS), ccx, scx)]
    cl_b, sl_b = _cast_tables(jnp.asarray(cl, dtype=F32), jnp.asarray(sl, dtype=F32))
    tabs = small[:2] + [cl_b, sl_b] + small[2:]

    cc = jnp.zeros((16, D_MODEL), F32).at[:nb].set(c).at[8].set(c_ctx)
    mod = _modulation(cc, w_ada, b_ada)

    h_in = (x, x, ctx)
    for l in range(depth):
        final = l == depth - 1
        (p_four, nq, nk, nvt, ng, gq, gk, gv, gg, gz) = _inproj(
            h_in, mod, l, norm_w3, w_in, w_nvt, cos_t, sin_t)
        mf = _fourier(p_four, tabs, w_four_b[l])
        mn = _na(nq, nk, nvt, ng, rpb_a[l], rpb_b[l])
        mg = _gla(gq, gk, gv, gg, gz, wa[l], ba[l], gnw[l])
        h = _outproj(h_in, mf, mn, mg, w_out, mod, l, norm_f.reshape(1, D_MODEL), final)
        h_in = (h,)
    return h
```

```python
import functools

import numpy as np
import jax
import jax.numpy as jnp
from jax import lax
from jax.experimental import pallas as pl
from jax.experimental.pallas import tpu as pltpu

F32 = jnp.float32
BF16 = jnp.bfloat16

D_MODEL = 1024
SEQ = 2048
CTX_LEN = 256
T_ALL = SEQ + CTX_LEN
GRID_W = 64
GRID_H = SEQ // GRID_W
EPS = 1e-6

FOUR_GROUPS, FOUR_DIM = 4, 64
D_FOUR = FOUR_GROUPS * FOUR_DIM
NA_HEADS, NA_HEAD_DIM = 6, 64
D_NA = NA_HEADS * NA_HEAD_DIM
NA_KH, NA_KW = 8, 16
GLA_HEADS, GLA_DK, GLA_DV = 4, 48, 96
D_GLA_K = GLA_HEADS * GLA_DK
D_GLA_V = GLA_HEADS * GLA_DV
GLA_RANK = 16
GLA_TAU = 16.0
GLA_CHUNK = 64
ROPE_BASE = 10000.0
LOG2E = 1.4426950408889634

IN_SIZES = (D_FOUR, D_FOUR, D_NA, D_NA, D_NA, D_NA, D_GLA_K, D_GLA_K, D_GLA_V, D_GLA_V,
            GLA_RANK, GLA_RANK)
D_IN = sum(IN_SIZES)

LANE = 128
GLA_KP = 2 * LANE
Z_PAD = LANE
N_CHUNKS_X = SEQ // GLA_CHUNK
N_CHUNKS_C = CTX_LEN // GLA_CHUNK
GLA_TILE = 256

NA_QROWS = 4
NA_KROWS = NA_QROWS + NA_KH
NA_QB = NA_QROWS * GRID_W
NA_KB = NA_KROWS * GRID_W
NA_NBLK = GRID_H // NA_QROWS
NA_VT_PAIR = LANE + 16
NA_VT_ROWS = NA_HEADS // 2 * NA_VT_PAIR

TOK_TILE = 768
N_XTILES = SEQ // TOK_TILE
X_TAIL = SEQ - N_XTILES * TOK_TILE
VMEM_LIMIT = 56 * 1024 * 1024

COL_FOUR = 0
COL_NA = COL_FOUR + 2 * D_FOUR
COL_GQ = COL_NA + 3 * D_NA
COL_GK = COL_GQ + GLA_KP
COL_GV = COL_GK + GLA_KP
COL_GG = COL_GV + D_GLA_V
COL_Z = COL_GG + D_GLA_V
N_COLS = COL_Z + Z_PAD


def _dot(a, b):
    return jnp.dot(a, b, preferred_element_type=F32)


def _dot_nt(a, b):
    return lax.dot_general(a, b, (((1,), (1,)), ((), ())), preferred_element_type=F32)


def _silu(x):
    return x / (1.0 + jnp.exp(-x))


def _reduce_rows(x, op):
    r, n = x.shape
    part = op(x.reshape(r // 8, 8, n), axis=0)
    return op(part, axis=0, keepdims=True)


def _split_bf16(x):
    hi = x.astype(BF16)
    lo = (x - hi.astype(F32)).astype(BF16)
    return hi, lo


def _qk_pieces():
    n_freq = GLA_DK // 4
    return [[h * GLA_DK + half * 2 * n_freq + part * n_freq
             for h in range(GLA_HEADS) for half in range(2)] for part in range(2)]


def _rope_tables():
    n_freq = GLA_DK // 4
    inv = ROPE_BASE ** (-np.arange(n_freq, dtype=np.float64) / n_freq)
    pos = np.arange(SEQ)
    prow, pcol = pos // GRID_W, pos % GRID_W
    cos = np.ones((T_ALL, LANE), np.float64)
    sin = np.zeros((T_ALL, LANE), np.float64)
    for h in range(GLA_HEADS):
        for half, p in enumerate((prow, pcol)):
            ang = p[:, None] * inv[None, :]
            lo = h * 2 * n_freq + half * n_freq
            cos[:SEQ, lo:lo + n_freq] = np.cos(ang)
            sin[:SEQ, lo:lo + n_freq] = np.sin(ang)
    return cos.astype(np.float32), sin.astype(np.float32)


def _dft_tables(n):
    idx = (np.arange(n)[:, None] * np.arange(n)[None, :]) % n
    ang = 2.0 * np.pi * idx / n
    return np.cos(ang) / np.sqrt(n), np.sin(ang) / np.sqrt(n)


def _block_diag(m, reps):
    n = m.shape[0]
    out = np.zeros((n * reps, n * reps), m.dtype)
    for r in range(reps):
        out[r * n:(r + 1) * n, r * n:(r + 1) * n] = m
    return out


def _na_row_offsets():
    out = []
    for r0 in (0, NA_QROWS, GRID_H - NA_QROWS):
        k0 = int(np.clip(r0 - NA_KH // 2, 0, GRID_H - NA_KROWS))
        per_type = []
        for a in range(NA_QROWS):
            r = r0 + a
            rs = int(np.clip(r - NA_KH // 2, 0, GRID_H - NA_KH))
            per_type.append([(k0 + j) - r + NA_KH - 1 if rs <= k0 + j < rs + NA_KH else None
                             for j in range(NA_KROWS)])
        out.append(per_type)
    return out


def _mod_kernel(c_ref, w_ref, b_ref, o_ref):
    c = c_ref[...]
    s_hi, s_lo = _split_bf16(_silu(c))
    w_hi, w_lo = _split_bf16(w_ref[0])
    acc = _dot(s_hi, w_hi) + _dot(s_lo, w_hi) + _dot(s_hi, w_lo)
    o_ref[0] = acc + b_ref[0]


def _modulation(cc, w_ada, b_ada):
    depth = w_ada.shape[0]
    n_mod = w_ada.shape[2]
    tile = D_MODEL
    return pl.pallas_call(
        _mod_kernel,
        grid=(depth, n_mod // tile),
        in_specs=[
            pl.BlockSpec((16, D_MODEL), lambda l, j: (0, 0)),
            pl.BlockSpec((1, D_MODEL, tile), lambda l, j: (l, 0, j)),
            pl.BlockSpec((1, 1, tile), lambda l, j: (l, 0, j)),
        ],
        out_specs=pl.BlockSpec((1, 16, tile), lambda l, j: (l, 0, j)),
        out_shape=jax.ShapeDtypeStruct((depth, 16, n_mod), F32),
        compiler_params=pltpu.CompilerParams(
            dimension_semantics=("arbitrary", "arbitrary"), vmem_limit_bytes=VMEM_LIMIT),
        name="adaln_mod",
    )(cc, w_ada, b_ada.reshape(depth, 1, n_mod))


def _mod_rows(mod_ref, b, row0, n_rows, col0):
    rows = row0 + lax.broadcasted_iota(jnp.int32, (n_rows, 1), 0)
    m_x = mod_ref[0, pl.ds(b, 1), col0:col0 + D_MODEL]
    m_c = mod_ref[0, 8:9, col0:col0 + D_MODEL]
    return jnp.where(rows >= SEQ, m_c, m_x)


def _fill_weight(wt_ref, w_scr, dst0, src0, width, scale=None):
    k = w_scr.shape[0]
    for c in range(0, width, LANE):
        n = min(LANE, width - c)
        blk = wt_ref[0, src0 + c:src0 + c + n, :]
        if scale is not None:
            blk = blk * scale
        if n < LANE:
            blk = jnp.concatenate([blk, jnp.zeros((LANE - n, k), F32)], axis=0)
        w_scr[:, dst0 + c:dst0 + c + LANE] = blk.T.astype(BF16)


def _stream_specs(tile):
    return [pl.BlockSpec((1, tile, D_MODEL), lambda b, j: (b, j, 0))]


def _split_stream_specs():
    assert X_TAIL + CTX_LEN == TOK_TILE and SEQ % X_TAIL == 0
    return [pl.BlockSpec((1, TOK_TILE, D_MODEL), lambda b, j: (b, jnp.minimum(j, N_XTILES - 1), 0)),
            pl.BlockSpec((1, X_TAIL, D_MODEL), lambda b, j: (b, SEQ // X_TAIL - 1, 0)),
            pl.BlockSpec((1, CTX_LEN, D_MODEL), lambda b, j: (b, 0, 0))]


def _stream_tile(h_refs, j, tile_scr):
    if len(h_refs) == 1:
        return h_refs[0][0]
    x_ref, tail_ref, ctx_ref = h_refs

    @pl.when(j < N_XTILES)
    def _():
        tile_scr[...] = x_ref[0]

    @pl.when(j == N_XTILES)
    def _():
        tile_scr[:X_TAIL] = tail_ref[0]
        tile_scr[X_TAIL:] = ctx_ref[0]

    return tile_scr[...]


def _inproj_kernel(*refs, n_h):
    h_refs, refs = refs[:n_h], refs[n_h:]
    (mod_ref, nw_ref, w_ref, cos_ref, sin_ref,
     four_ref, nq_ref, nk_ref, nvt_ref, ng_ref,
     gq_ref, gk_ref, gv_ref, gg_ref, gz_ref, hx_scr, w_scr, wvt_scr) = refs[:18]
    b = pl.program_id(0)
    j = pl.program_id(1)

    @pl.when((b == 0) & (j == 0))
    def _():
        src = np.concatenate([[0], np.cumsum(IN_SIZES)])
        _fill_weight(w_ref, w_scr, COL_FOUR, src[0], 2 * D_FOUR)
        _fill_weight(w_ref, w_scr, COL_NA, src[2], D_NA, NA_HEAD_DIM ** -0.5 * LOG2E)
        _fill_weight(w_ref, w_scr, COL_NA + D_NA, src[3], D_NA)
        _fill_weight(w_ref, w_scr, COL_NA + 2 * D_NA, src[5], D_NA)
        n_freq = GLA_DK // 4
        pad = jnp.zeros((LANE - GLA_HEADS * 2 * n_freq, D_MODEL), F32)
        for dst0, src0 in ((COL_GQ, src[6]), (COL_GK, src[7])):
            for part, offs in enumerate(_qk_pieces()):
                tile = jnp.concatenate(
                    [w_ref[0, src0 + o:src0 + o + n_freq, :] for o in offs] + [pad], axis=0)
                w_scr[:, dst0 + part * LANE:dst0 + (part + 1) * LANE] = tile.T.astype(BF16)
        _fill_weight(w_ref, w_scr, COL_GV, src[8], D_GLA_V)
        _fill_weight(w_ref, w_scr, COL_GG, src[9], D_GLA_V)
        _fill_weight(w_ref, w_scr, COL_Z, src[10], 2 * GLA_RANK)
        for p in range(NA_HEADS // 2):
            blk = w_ref[0, src[4] + p * LANE:src[4] + (p + 1) * LANE, :]
            wvt_scr[p * NA_VT_PAIR:p * NA_VT_PAIR + LANE, :] = blk.astype(BF16)
            wvt_scr[p * NA_VT_PAIR + LANE:(p + 1) * NA_VT_PAIR, :] = jnp.zeros(
                (NA_VT_PAIR - LANE, D_MODEL), BF16)

    x = _stream_tile(h_refs, j, refs[18] if n_h > 1 else None)
    ms = jnp.mean(x * x, axis=-1, keepdims=True)
    y = x * lax.rsqrt(ms + EPS) * nw_ref[0]
    shift = _mod_rows(mod_ref, b, j * TOK_TILE, TOK_TILE, 0)
    scale = _mod_rows(mod_ref, b, j * TOK_TILE, TOK_TILE, D_MODEL)
    hx_scr[...] = (y * (1.0 + scale) + shift).astype(BF16)

    def proj(c0, width):
        return _dot(hx_scr[...], w_scr[:, c0:c0 + width])

    four_ref[0] = proj(COL_FOUR, 2 * D_FOUR).astype(BF16)
    nq_ref[0] = proj(COL_NA, D_NA).astype(BF16)
    nk_ref[0] = proj(COL_NA + D_NA, D_NA).astype(BF16)
    ones_rows = lax.broadcasted_iota(jnp.int32, (NA_VT_ROWS, 1), 0) % NA_VT_PAIR >= LANE
    nvt_ref[0] = (_dot_nt(wvt_scr[...], hx_scr[...]) + jnp.where(ones_rows, 1.0, 0.0)).astype(BF16)
    ng_ref[0] = proj(COL_NA + 2 * D_NA, D_NA).astype(BF16)
    gv_ref[0] = proj(COL_GV, D_GLA_V).astype(BF16)
    gg_ref[0] = proj(COL_GG, D_GLA_V).astype(BF16)
    gz_ref[0] = proj(COL_Z, Z_PAD)

    cos = cos_ref[...]
    sin = sin_ref[...]

    def rope(c0, out_ref, scale_out):
        u = proj(c0, GLA_KP)
        u1, u2 = u[:, :LANE], u[:, LANE:]
        out_ref[0, :, :LANE] = (u1 * cos - u2 * sin) * scale_out
        out_ref[0, :, LANE:] = (u1 * sin + u2 * cos) * scale_out

    rope(COL_GQ, gq_ref, GLA_DK ** -0.5)
    rope(COL_GK, gk_ref, 1.0)


def _inproj(h_in, mod, l, norm_w, w_in, cos_t, sin_t):
    nb = h_in[0].shape[0]
    nt = T_ALL // TOK_TILE
    split = len(h_in) > 1
    tok = lambda width: pl.BlockSpec((1, TOK_TILE, width), lambda b, j: (b, j, 0))
    layer = lambda a: pl.BlockSpec((1,) + a.shape[1:], lambda b, j: (l,) + (0,) * (a.ndim - 1))
    out_widths = [(2 * D_FOUR, BF16), (D_NA, BF16), (D_NA, BF16), (None, BF16), (D_NA, BF16),
                  (GLA_KP, F32), (GLA_KP, F32), (D_GLA_V, BF16), (D_GLA_V, BF16), (Z_PAD, F32)]
    vt_spec = pl.BlockSpec((1, NA_VT_ROWS, TOK_TILE), lambda b, j: (b, 0, j))
    vt_shape = jax.ShapeDtypeStruct((nb, NA_VT_ROWS, T_ALL), BF16)
    return pl.pallas_call(
        functools.partial(_inproj_kernel, n_h=len(h_in)),
        grid=(nb, nt),
        in_specs=(_split_stream_specs() if split else _stream_specs(TOK_TILE)) + [
            layer(mod),
            layer(norm_w),
            layer(w_in),
            pl.BlockSpec((TOK_TILE, LANE), lambda b, j: (j, 0)),
            pl.BlockSpec((TOK_TILE, LANE), lambda b, j: (j, 0)),
        ],
        out_specs=[vt_spec if w is None else tok(w) for w, _ in out_widths],
        out_shape=[vt_shape if w is None else jax.ShapeDtypeStruct((nb, T_ALL, w), dt)
                   for w, dt in out_widths],
        scratch_shapes=[pltpu.VMEM((TOK_TILE, D_MODEL), BF16), pltpu.VMEM((D_MODEL, N_COLS), BF16),
                        pltpu.VMEM((NA_VT_ROWS, D_MODEL), BF16)]
        + ([pltpu.VMEM((TOK_TILE, D_MODEL), F32)] if split else []),
        compiler_params=pltpu.CompilerParams(
            dimension_semantics=("arbitrary", "arbitrary"), vmem_limit_bytes=VMEM_LIMIT),
        name="inproj",
    )(*h_in, mod, norm_w, w_in, cos_t, sin_t)


def _fourier_kernel(p_ref, cbd_ref, sbd_ref, cl_ref, sl_ref, cc_ref, sc_ref, wf_ref, o_ref,
                    uc_scr, us_scr):
    u = p_ref[0, :, :D_FOUR]
    uc_scr[...] = _dot(u, cbd_ref[...]).astype(BF16)
    us_scr[...] = _dot(u, sbd_ref[...]).astype(BF16)

    def finish(f, r0, n):
        four = _dot(f.astype(BF16), wf_ref[...])
        gate = p_ref[0, pl.ds(r0, n), D_FOUR:].astype(F32)
        o_ref[0, pl.ds(r0, n), :] = (four * _silu(gate)).astype(BF16)

    tile = 256

    def body(i, carry):
        r0 = pl.multiple_of(i * tile, tile)
        f = (_dot(cl_ref[pl.ds(r0, tile), :], uc_scr[:SEQ, :])
             - _dot(sl_ref[pl.ds(r0, tile), :], us_scr[:SEQ, :]))
        finish(f, r0, tile)
        return carry

    lax.fori_loop(0, SEQ // tile, body, 0)
    f_c = _dot(cc_ref[...], uc_scr[SEQ:, :]) - _dot(sc_ref[...], us_scr[SEQ:, :])
    finish(f_c, SEQ, CTX_LEN)


def _fourier(p_four, tabs, wf):
    nb = p_four.shape[0]
    full = lambda a: pl.BlockSpec(a.shape, lambda b: (0,) * a.ndim)
    return pl.pallas_call(
        _fourier_kernel,
        grid=(nb,),
        in_specs=[pl.BlockSpec((1, T_ALL, 2 * D_FOUR), lambda b: (b, 0, 0))]
        + [full(t) for t in tabs] + [full(wf)],
        out_specs=pl.BlockSpec((1, T_ALL, D_FOUR), lambda b: (b, 0, 0)),
        out_shape=jax.ShapeDtypeStruct((nb, T_ALL, D_FOUR), BF16),
        scratch_shapes=[pltpu.VMEM((T_ALL, D_FOUR), BF16), pltpu.VMEM((T_ALL, D_FOUR), BF16)],
        compiler_params=pltpu.CompilerParams(
            dimension_semantics=("arbitrary",), vmem_limit_bytes=VMEM_LIMIT),
        name="fourier_mix",
    )(p_four, *tabs, wf)


def _na_build_bias(ra_ref, rb_ref, tbl_ref):
    w = GRID_W
    ck = lax.broadcasted_iota(jnp.int32, (w, 2 * w), 0)
    ln = lax.broadcasted_iota(jnp.int32, (w, 2 * w), 1)
    cq = ln % w
    cs = jnp.clip(cq - NA_KW // 2, 0, w - NA_KW)
    col_ok = (ck >= cs) & (ck < cs + NA_KW)
    left = ln < w
    neg = jnp.full((w, 2 * w), -jnp.inf, F32)
    offs = _na_row_offsets()
    cache = {}
    for typ in range(3):
        for ap in range(NA_QROWS // 2):
            for jr in range(NA_KROWS):
                d0 = offs[typ][2 * ap][jr]
                d1 = offs[typ][2 * ap + 1][jr]
                for h in range(NA_HEADS):
                    key = (h, d0, d1)
                    if key not in cache:
                        if d0 is None and d1 is None:
                            cache[key] = neg
                        else:
                            ok = col_ok
                            if d0 is None:
                                row = rb_ref[h, d1:d1 + 1, :]
                                ok = ok & jnp.logical_not(left)
                            elif d1 is None:
                                row = ra_ref[h, d0:d0 + 1, :]
                                ok = ok & left
                            else:
                                row = ra_ref[h, d0:d0 + 1, :] + rb_ref[h, d1:d1 + 1, :]
                            skew = pltpu.roll(jnp.broadcast_to(row * LOG2E, (w, 2 * w)),
                                              2 * w - (NA_KW - 1), 1, stride=1, stride_axis=0)
                            cache[key] = jnp.where(ok, skew, neg)
                    tbl_ref[h * 3 + typ, jr * w:(jr + 1) * w, ap * 2 * w:(ap + 1) * 2 * w] = cache[key]


def _na_kernel(q_ref, k_ref, vt_ref, g_ref, ra_ref, rb_ref, o_ref, bias_ref, s_scr):
    lane = lax.broadcasted_iota(jnp.int32, (1, LANE), 1)
    first_head_lanes = lane < NA_HEAD_DIM
    first_head_rows = lax.broadcasted_iota(jnp.int32, (LANE, 1), 0) < NA_HEAD_DIM
    heads = range(NA_HEADS)
    cols = [slice(h // 2 * LANE, (h // 2 + 1) * LANE) for h in heads]

    @pl.when(pl.program_id(0) == 0)
    def _():
        _na_build_bias(ra_ref, rb_ref, bias_ref)

    def block_geometry(j):
        if isinstance(j, int):
            k0 = min(max(j * NA_QROWS - NA_KH // 2, 0), GRID_H - NA_KROWS) * GRID_W
            return pl.ds(j * NA_QB, NA_QB), k0, (0 if j == 0 else 2 if j == NA_NBLK - 1 else 1)
        k0 = jnp.clip(j * NA_QROWS - NA_KH // 2, 0, GRID_H - NA_KROWS) * GRID_W
        typ = jnp.where(j == 0, 0, jnp.where(j == NA_NBLK - 1, 2, 1))
        return pl.ds(pl.multiple_of(j * NA_QB, NA_QB), NA_QB), pl.multiple_of(k0, NA_QB), typ

    def scores(j, buf, local):
        qrows, k0, typ = block_geometry(j)
        for h in heads:
            qp = q_ref[0, qrows, cols[h]]
            keep = first_head_lanes if h % 2 == 0 else jnp.logical_not(first_head_lanes)
            qm = jnp.where(keep, qp, jnp.zeros_like(qp))
            s_scr[buf, h, :CTX_LEN, :] = _dot_nt(k_ref[0, SEQ:, cols[h]], qm)
            if local:
                s_scr[buf, h, CTX_LEN:, :] = (
                    _dot_nt(k_ref[0, pl.ds(k0, NA_KB), cols[h]], qm) + bias_ref[h * 3 + typ])

    def finish(j, buf, local):
        qrows, k0, _ = block_geometry(j)
        n_keys = CTX_LEN + NA_KB if local else CTX_LEN
        e = []
        for h in heads:
            s = s_scr[buf, h, :n_keys, :]
            e.append(jnp.exp2((s - _reduce_rows(s, jnp.max)).astype(BF16)))
        for p in range(NA_HEADS // 2):
            vrows = slice(p * NA_VT_PAIR, (p + 1) * NA_VT_PAIR)
            acc = None
            for h in (2 * p, 2 * p + 1):
                o = _dot(vt_ref[0, vrows, SEQ:], e[h][:CTX_LEN])
                if local:
                    o = o + _dot(vt_ref[0, vrows, pl.ds(k0, NA_KB)], e[h][CTX_LEN:])
                o = o[:LANE] / o[LANE:LANE + 1]
                acc = o if h % 2 == 0 else jnp.where(first_head_rows, acc, o)
            gate = g_ref[0, qrows, cols[2 * p]].astype(F32)
            o_ref[0, qrows, cols[2 * p]] = (acc.T * _silu(gate)).astype(BF16)

    def pair_body(i, carry):
        scores(2 * i + 1, 1, True)
        finish(2 * i, 0, True)
        scores(2 * i + 2, 0, True)
        finish(2 * i + 1, 1, True)
        return carry

    scores(0, 0, True)
    lax.fori_loop(0, NA_NBLK // 2 - 1, pair_body, 0)
    scores(NA_NBLK - 1, 1, True)
    finish(NA_NBLK - 2, 0, True)
    scores(NA_NBLK, 0, False)
    finish(NA_NBLK - 1, 1, True)
    finish(NA_NBLK, 0, False)


def _na(nq, nk, nvt, ng, rpb_a, rpb_b):
    nb = nq.shape[0]
    seq = pl.BlockSpec((1, T_ALL, D_NA), lambda b: (b, 0, 0))
    seq_t = pl.BlockSpec((1, NA_VT_ROWS, T_ALL), lambda b: (b, 0, 0))
    tab = pl.BlockSpec(rpb_a.shape, lambda b: (0, 0, 0))
    return pl.pallas_call(
        _na_kernel,
        grid=(nb,),
        in_specs=[seq, seq, seq_t, seq, tab, tab],
        out_specs=seq,
        out_shape=jax.ShapeDtypeStruct((nb, T_ALL, D_NA), BF16),
        scratch_shapes=[pltpu.VMEM((NA_HEADS * 3, NA_KB, NA_QB), F32),
                        pltpu.VMEM((2, NA_HEADS, CTX_LEN + NA_KB, NA_QB), F32)],
        compiler_params=pltpu.CompilerParams(
            dimension_semantics=("arbitrary",), vmem_limit_bytes=VMEM_LIMIT),
        name="nbr_attention",
    )(nq, nk, nvt, ng, rpb_a, rpb_b)


def _gla_kernel(q_ref, k_ref, v_ref, g_ref, z_ref, wa_ref, ba_ref, nw_ref, o_ref,
                st_scr, o_scr, qd_scr, kd_scr, ke_scr, vt_scr, dec_scr):
    C = GLA_CHUNK
    TR = GLA_TILE
    n_sub = TR // C
    kl = lax.broadcasted_iota(jnp.int32, (1, GLA_KP), 1) % LANE
    k_head = jnp.where(kl < GLA_HEADS * (GLA_DK // 2), kl // (GLA_DK // 2), GLA_HEADS)
    v_head = lax.broadcasted_iota(jnp.int32, (1, D_GLA_V), 1) // GLA_DV
    st_mask = (lax.broadcasted_iota(jnp.int32, (D_GLA_V, 1), 0) // GLA_DV) == k_head
    ti4 = lax.broadcasted_iota(jnp.int32, (C, GLA_HEADS * C), 0)
    si4 = lax.broadcasted_iota(jnp.int32, (C, GLA_HEADS * C), 1) % C
    tr = lax.broadcasted_iota(jnp.int32, (TR, TR), 0)
    sr = lax.broadcasted_iota(jnp.int32, (TR, TR), 1)
    same_chunk = (tr // C) == (sr // C)
    tri = [jnp.where(same_chunk & (sr <= tr), 1.0, 0.0).astype(BF16),
           jnp.where(same_chunk & (sr >= tr), 1.0, 0.0).astype(BF16)]

    def tile_start(i):
        return i * TR if isinstance(i, int) else pl.multiple_of(i * TR, TR)

    def tile_rows(i):
        return pl.ds(tile_start(i), TR)

    def decay_sums(i):
        z = z_ref[0, tile_rows(i), :].astype(BF16)
        g = _dot(z, wa_ref[...]) + ba_ref[...]
        la = (jnp.minimum(g, 0.0) - jnp.log(1.0 + jnp.exp(-jnp.abs(g)))) * (1.0 / GLA_TAU)
        sums = []
        for d in range(2):
            la_hi, la_lo = _split_bf16(la[:, d * GLA_KP:(d + 1) * GLA_KP])
            sums.append(_dot(tri[d], la_hi) + _dot(tri[d], la_lo))
        return sums

    def decayed_operands(i, sums):
        rows = tile_rows(i)
        q = q_ref[0, rows, :]
        k = k_ref[0, rows, :]
        for d in range(2):
            bcum = sums[d]
            ends = [bcum[(c + 1) * C - 1:(c + 1) * C] if d == 0 else bcum[c * C:c * C + 1]
                    for c in range(n_sub)]
            b_end = jnp.concatenate([jnp.broadcast_to(e, (C, GLA_KP)) for e in ends], axis=0)
            qd_scr[d, rows, :] = (q * jnp.exp(bcum)).astype(BF16)
            kd_scr[d, rows, :] = (k * jnp.exp(-bcum)).astype(BF16)
            ke_scr[d, rows, :] = (k * jnp.exp(b_end - bcum)).astype(BF16)
            for c in range(n_sub):
                dec_scr[d, pl.ds(i * n_sub + c, 1), :] = jnp.exp(ends[c])

    def intra_chunk(i):
        r0 = tile_start(i)
        v_bd = []
        for c in range(n_sub):
            v_c = v_ref[0, pl.ds(r0 + c * C, C), :]
            v_bd.append(jnp.concatenate(
                [jnp.where(v_head == h, v_c, jnp.zeros_like(v_c)) for h in range(GLA_HEADS)], axis=0))
            vt_scr[i * n_sub + c] = v_c.astype(F32).T.astype(BF16)
        scores = []
        for d in range(2):
            for c in range(n_sub):
                crow = pl.ds(r0 + c * C, C)
                k_c = kd_scr[d, crow, :]
                k_bd = jnp.concatenate(
                    [jnp.where(k_head == h, k_c, jnp.zeros_like(k_c)) for h in range(GLA_HEADS)], axis=0)
                scores.append(_dot_nt(qd_scr[d, crow, :], k_bd))
        for d in range(2):
            causal = (si4 <= ti4) if d == 0 else (si4 >= ti4)
            for c in range(n_sub):
                a = jnp.where(causal, scores[d * n_sub + c], 0.0).astype(BF16)
                o_scr[d, pl.ds(r0 + c * C, C), :] = _dot(a, v_bd[c])

    def tile_body(i, carry):
        sums = decay_sums(i + 1)
        intra_chunk(i)
        decayed_operands(i + 1, sums)
        return carry

    n_tiles = T_ALL // TR
    decayed_operands(0, decay_sums(0))
    lax.fori_loop(0, n_tiles - 1, tile_body, 0, unroll=2)
    intra_chunk(n_tiles - 1)

    st_scr[...] = jnp.zeros_like(st_scr)
    n_all = N_CHUNKS_C + N_CHUNKS_X

    def step(i, carry):
        in_ctx = i < N_CHUNKS_C
        idx_f = jnp.where(in_ctx, N_CHUNKS_X + i, i - N_CHUNKS_C)
        idx_b = jnp.where(in_ctx, N_CHUNKS_X + N_CHUNKS_C - 1 - i, n_all - 1 - i)
        half = D_GLA_V // 2
        dirs = ((0, idx_f), (1, idx_b))
        offs = [pl.ds(pl.multiple_of(idx * C, C), C) for _, idx in dirs]
        upd = []
        for (d, idx), off in zip(dirs, offs):
            ke = ke_scr[d, off, :]
            upd.append(jnp.concatenate(
                [_dot(vt_scr[idx, :half, :], ke), _dot(vt_scr[idx, half:, :], ke)], axis=0))
        for (d, idx), off in zip(dirs, offs):
            st = st_scr[d]
            o_scr[d, off, :] += _dot_nt(qd_scr[d, off, :], st.astype(BF16))
            st_scr[d] = st * dec_scr[d, pl.ds(idx, 1), :] + jnp.where(st_mask, upd[d], 0.0)
        return carry

    lax.fori_loop(0, n_all, step, 0, unroll=4)

    head_ones = jnp.where(
        (lax.broadcasted_iota(jnp.int32, (D_GLA_V, 1), 0) // GLA_DV) == v_head, 1.0, 0.0).astype(BF16)
    tile = 256

    def fin(i, carry):
        r0 = pl.multiple_of(i * tile, tile)
        o = o_scr[0, pl.ds(r0, tile), :] + o_scr[1, pl.ds(r0, tile), :]
        sq_hi, sq_lo = _split_bf16(o * o)
        ms = (_dot(sq_hi, head_ones) + _dot(sq_lo, head_ones)) * (1.0 / GLA_DV)
        y = o * lax.rsqrt(ms + EPS) * nw_ref[...]
        gate = g_ref[0, pl.ds(r0, tile), :].astype(F32)
        o_ref[0, pl.ds(r0, tile), :] = (y * _silu(gate)).astype(BF16)
        return carry

    lax.fori_loop(0, T_ALL // tile, fin, 0, unroll=3)


def _gla(gq, gk, gv, gg, gz, wa, ba, gnw):
    nb = gq.shape[0]
    seq = lambda w: pl.BlockSpec((1, T_ALL, w), lambda b: (b, 0, 0))
    full = lambda a: pl.BlockSpec(a.shape, lambda b: (0,) * a.ndim)
    return pl.pallas_call(
        _gla_kernel,
        grid=(nb,),
        in_specs=[seq(GLA_KP), seq(GLA_KP), seq(D_GLA_V), seq(D_GLA_V), seq(Z_PAD),
                  full(wa), full(ba), full(gnw)],
        out_specs=seq(D_GLA_V),
        out_shape=jax.ShapeDtypeStruct((nb, T_ALL, D_GLA_V), BF16),
        scratch_shapes=[pltpu.VMEM((2, D_GLA_V, GLA_KP), F32),
                        pltpu.VMEM((2, T_ALL, D_GLA_V), F32),
                        pltpu.VMEM((2, T_ALL, GLA_KP), BF16),
                        pltpu.VMEM((2, T_ALL, GLA_KP), BF16),
                        pltpu.VMEM((2, T_ALL, GLA_KP), BF16),
                        pltpu.VMEM((T_ALL // GLA_CHUNK, D_GLA_V, GLA_CHUNK), BF16),
                        pltpu.VMEM((2, T_ALL // GLA_CHUNK, GLA_KP), F32)],

        compiler_params=pltpu.CompilerParams(
            dimension_semantics=("arbitrary",), vmem_limit_bytes=VMEM_LIMIT),
        name="gla_mix",
    )(gq, gk, gv, gg, gz, wa, ba, gnw)


def _outproj_kernel(*refs, n_h, tile, final):
    h_refs, refs = refs[:n_h], refs[n_h:]
    mf_ref, mn_ref, mg_ref, w_ref, mod_ref, nf_ref, o_ref, w_scr = refs[:8]
    b = pl.program_id(0)
    j = pl.program_id(1)

    @pl.when((b == 0) & (j == 0))
    def _():
        for r in range(0, D_MODEL, 256):
            w_scr[r:r + 256, :] = w_ref[0, r:r + 256, :].astype(BF16)

    acc = (_dot(mf_ref[0], w_scr[:D_FOUR, :])
           + _dot(mn_ref[0], w_scr[D_FOUR:D_FOUR + D_NA, :])
           + _dot(mg_ref[0], w_scr[D_FOUR + D_NA:, :]))
    gate = _mod_rows(mod_ref, b, j * tile, tile, 2 * D_MODEL)
    x = _stream_tile(h_refs, j, refs[8] if n_h > 1 else None) + gate * acc
    if final:
        ms = jnp.mean(x * x, axis=-1, keepdims=True)
        x = x * lax.rsqrt(ms + EPS) * nf_ref[...]
    o_ref[0] = x


def _outproj(h_in, mf, mn, mg, w_out, mod, l, nf, final):
    nb = h_in[0].shape[0]
    tile = 512 if final else TOK_TILE
    rows = SEQ if final else T_ALL
    if final:
        h_in = h_in[:1]
    split = len(h_in) > 1
    tok = lambda width: pl.BlockSpec((1, tile, width), lambda b, j: (b, j, 0))
    layer = lambda a: pl.BlockSpec((1,) + a.shape[1:], lambda b, j: (l,) + (0,) * (a.ndim - 1))
    return pl.pallas_call(
        functools.partial(_outproj_kernel, n_h=len(h_in), tile=tile, final=final),
        grid=(nb, rows // tile),
        in_specs=(_split_stream_specs() if split else _stream_specs(tile))
        + [tok(D_FOUR), tok(D_NA), tok(D_GLA_V),
           layer(w_out), layer(mod), pl.BlockSpec((1, D_MODEL), lambda b, j: (0, 0))],
        out_specs=tok(D_MODEL),
        out_shape=jax.ShapeDtypeStruct((nb, rows, D_MODEL), F32),
        scratch_shapes=[pltpu.VMEM((D_MODEL, D_MODEL), BF16)]
        + ([pltpu.VMEM((TOK_TILE, D_MODEL), F32)] if split else []),
        compiler_params=pltpu.CompilerParams(
            dimension_semantics=("arbitrary", "arbitrary"), vmem_limit_bytes=VMEM_LIMIT),
        name="outproj_final" if final else "outproj",
    )(*h_in, mf, mn, mg, w_out, mod, nf)


def _decay_weights(w_f, b_f, w_b, b_b):
    n_freq = GLA_DK // 4
    lane_pad = lambda a: jnp.zeros(a.shape[:-1] + (LANE - GLA_HEADS * 2 * n_freq,), a.dtype)
    lanes = lambda a: jnp.concatenate(
        [blk for offs in _qk_pieces() for blk in [a[..., o:o + n_freq] for o in offs] + [lane_pad(a)]],
        axis=-1)
    zero = jnp.zeros_like(lanes(w_f))
    top = jnp.concatenate([lanes(w_f), zero], axis=-1)
    bot = jnp.concatenate([zero, lanes(w_b)], axis=-1)
    pad = jnp.zeros((w_f.shape[0], Z_PAD - 2 * GLA_RANK, 2 * GLA_KP), w_f.dtype)
    wa = jnp.concatenate([top, bot, pad], axis=1).astype(BF16)
    ba = jnp.concatenate([lanes(b_f), lanes(b_b)], axis=-1)[:, None, :]
    return wa, ba


def kernel(x, c, ctx, c_ctx, w_ada, b_ada, norm_w, w_in, w_four, rpb, w_alpha_fwd, b_alpha_fwd,
           w_alpha_bwd, b_alpha_bwd, gla_norm_w, w_out, norm_f):
    nb = x.shape[0]
    depth = w_in.shape[0]
    assert x.shape == (nb, SEQ, D_MODEL) and ctx.shape == (nb, CTX_LEN, D_MODEL) and nb <= 8

    w_in_t = jnp.swapaxes(w_in, 1, 2)
    w_four_b = w_four.astype(BF16)
    norm_w3 = norm_w.reshape(depth, 1, D_MODEL)
    wa, ba = _decay_weights(w_alpha_fwd, b_alpha_fwd, w_alpha_bwd, b_alpha_bwd)
    gnw = jnp.tile(gla_norm_w, (1, GLA_HEADS)).reshape(depth, 1, D_GLA_V)
    n_off = 2 * NA_KW - 1
    rpb_rev = rpb[..., ::-1]
    rpb_a = jnp.pad(rpb_rev, ((0, 0), (0, 0), (0, 1), (0, LANE - n_off)))
    rpb_b = jnp.pad(rpb_rev, ((0, 0), (0, 0), (0, 1), (GRID_W, LANE - GRID_W - n_off)))

    cos_t, sin_t = (jnp.asarray(t) for t in _rope_tables())
    c64, s64 = _dft_tables(FOUR_DIM)
    cl, sl = _dft_tables(SEQ)
    ccx, scx = _dft_tables(CTX_LEN)
    tabs = [jnp.asarray(t, dtype=F32).astype(BF16) for t in
            (_block_diag(c64, FOUR_GROUPS), _block_diag(s64, FOUR_GROUPS), cl, sl, ccx, scx)]

    cc = jnp.zeros((16, D_MODEL), F32).at[:nb].set(c).at[8].set(c_ctx)
    mod = _modulation(cc, w_ada, b_ada)

    h_in = (x, x, ctx)
    for l in range(depth):
        final = l == depth - 1
        (p_four, nq, nk, nvt, ng, gq, gk, gv, gg, gz) = _inproj(
            h_in, mod, l, norm_w3, w_in_t, cos_t, sin_t)
        mf = _fourier(p_four, tabs, w_four_b[l])
        mn = _na(nq, nk, nvt, ng, rpb_a[l], rpb_b[l])
        mg = _gla(gq, gk, gv, gg, gz, wa[l], ba[l], gnw[l])
        h = _outproj(h_in, mf, mn, mg, w_out, mod, l, norm_f.reshape(1, D_MODEL), final)
        h_in = (h,)
    return h
```

```python
import functools

import numpy as np
import jax
import jax.numpy as jnp
from jax import lax
from jax.experimental import pallas as pl
from jax.experimental.pallas import tpu as pltpu

F32 = jnp.float32
BF16 = jnp.bfloat16

D_MODEL = 1024
SEQ = 2048
CTX_LEN = 256
T_ALL = SEQ + CTX_LEN
GRID_W = 64
GRID_H = SEQ // GRID_W
EPS = 1e-6

FOUR_GROUPS, FOUR_DIM = 4, 64
D_FOUR = FOUR_GROUPS * FOUR_DIM
NA_HEADS, NA_HEAD_DIM = 6, 64
D_NA = NA_HEADS * NA_HEAD_DIM
NA_KH, NA_KW = 8, 16
GLA_HEADS, GLA_DK, GLA_DV = 4, 48, 96
D_GLA_K = GLA_HEADS * GLA_DK
D_GLA_V = GLA_HEADS * GLA_DV
GLA_RANK = 16
GLA_TAU = 16.0
GLA_CHUNK = 64
ROPE_BASE = 10000.0
LOG2E = 1.4426950408889634

IN_SIZES = (D_FOUR, D_FOUR, D_NA, D_NA, D_NA, D_NA, D_GLA_K, D_GLA_K, D_GLA_V, D_GLA_V,
            GLA_RANK, GLA_RANK)
D_IN = sum(IN_SIZES)

LANE = 128
GLA_KP = 2 * LANE
Z_PAD = LANE
N_CHUNKS_X = SEQ // GLA_CHUNK
N_CHUNKS_C = CTX_LEN // GLA_CHUNK
GLA_TILE = 256

NA_QROWS = 4
NA_KROWS = NA_QROWS + NA_KH
NA_QB = NA_QROWS * GRID_W
NA_KB = NA_KROWS * GRID_W
NA_NBLK = GRID_H // NA_QROWS
NA_VT_PAIR = LANE + 16
NA_VT_ROWS = NA_HEADS // 2 * NA_VT_PAIR

TOK_TILE = 768
N_XTILES = SEQ // TOK_TILE
X_TAIL = SEQ - N_XTILES * TOK_TILE
VMEM_LIMIT = 56 * 1024 * 1024

COL_FOUR = 0
COL_NA = COL_FOUR + 2 * D_FOUR
COL_GQ = COL_NA + 3 * D_NA
COL_GK = COL_GQ + GLA_KP
COL_GV = COL_GK + GLA_KP
COL_GG = COL_GV + D_GLA_V
COL_Z = COL_GG + D_GLA_V
N_COLS = COL_Z + Z_PAD


def _dot(a, b):
    return jnp.dot(a, b, preferred_element_type=F32)


def _dot_nt(a, b):
    return lax.dot_general(a, b, (((1,), (1,)), ((), ())), preferred_element_type=F32)


def _silu(x):
    return x / (1.0 + jnp.exp(-x))


def _reduce_rows(x, op):
    r, n = x.shape
    part = op(x.reshape(r // 8, 8, n), axis=0)
    return op(part, axis=0, keepdims=True)


def _split_bf16(x):
    hi = x.astype(BF16)
    lo = (x - hi.astype(F32)).astype(BF16)
    return hi, lo


def _qk_pieces():
    n_freq = GLA_DK // 4
    return [[h * GLA_DK + half * 2 * n_freq + part * n_freq
             for h in range(GLA_HEADS) for half in range(2)] for part in range(2)]


def _rope_tables():
    n_freq = GLA_DK // 4
    inv = ROPE_BASE ** (-np.arange(n_freq, dtype=np.float64) / n_freq)
    pos = np.arange(SEQ)
    prow, pcol = pos // GRID_W, pos % GRID_W
    cos = np.ones((T_ALL, LANE), np.float64)
    sin = np.zeros((T_ALL, LANE), np.float64)
    for h in range(GLA_HEADS):
        for half, p in enumerate((prow, pcol)):
            ang = p[:, None] * inv[None, :]
            lo = h * 2 * n_freq + half * n_freq
            cos[:SEQ, lo:lo + n_freq] = np.cos(ang)
            sin[:SEQ, lo:lo + n_freq] = np.sin(ang)
    return cos.astype(np.float32), sin.astype(np.float32)


def _dft_tables(n):
    idx = (np.arange(n)[:, None] * np.arange(n)[None, :]) % n
    ang = 2.0 * np.pi * idx / n
    return np.cos(ang) / np.sqrt(n), np.sin(ang) / np.sqrt(n)


def _block_diag(m, reps):
    n = m.shape[0]
    out = np.zeros((n * reps, n * reps), m.dtype)
    for r in range(reps):
        out[r * n:(r + 1) * n, r * n:(r + 1) * n] = m
    return out


def _na_row_offsets():
    out = []
    for r0 in (0, NA_QROWS, GRID_H - NA_QROWS):
        k0 = int(np.clip(r0 - NA_KH // 2, 0, GRID_H - NA_KROWS))
        per_type = []
        for a in range(NA_QROWS):
            r = r0 + a
            rs = int(np.clip(r - NA_KH // 2, 0, GRID_H - NA_KH))
            per_type.append([(k0 + j) - r + NA_KH - 1 if rs <= k0 + j < rs + NA_KH else None
                             for j in range(NA_KROWS)])
        out.append(per_type)
    return out


def _mod_kernel(c_ref, w_ref, b_ref, o_ref):
    c = c_ref[...]
    s_hi, s_lo = _split_bf16(_silu(c))
    w_hi, w_lo = _split_bf16(w_ref[0])
    acc = _dot(s_hi, w_hi) + _dot(s_lo, w_hi) + _dot(s_hi, w_lo)
    o_ref[0] = acc + b_ref[0]


def _modulation(cc, w_ada, b_ada):
    depth = w_ada.shape[0]
    n_mod = w_ada.shape[2]
    tile = D_MODEL
    return pl.pallas_call(
        _mod_kernel,
        grid=(depth, n_mod // tile),
        in_specs=[
            pl.BlockSpec((16, D_MODEL), lambda l, j: (0, 0)),
            pl.BlockSpec((1, D_MODEL, tile), lambda l, j: (l, 0, j)),
            pl.BlockSpec((1, 1, tile), lambda l, j: (l, 0, j)),
        ],
        out_specs=pl.BlockSpec((1, 16, tile), lambda l, j: (l, 0, j)),
        out_shape=jax.ShapeDtypeStruct((depth, 16, n_mod), F32),
        compiler_params=pltpu.CompilerParams(
            dimension_semantics=("arbitrary", "arbitrary"), vmem_limit_bytes=VMEM_LIMIT),
        name="adaln_mod",
    )(cc, w_ada, b_ada.reshape(depth, 1, n_mod))


def _mod_rows(mod_ref, b, row0, n_rows, col0):
    rows = row0 + lax.broadcasted_iota(jnp.int32, (n_rows, 1), 0)
    m_x = mod_ref[0, pl.ds(b, 1), col0:col0 + D_MODEL]
    m_c = mod_ref[0, 8:9, col0:col0 + D_MODEL]
    return jnp.where(rows >= SEQ, m_c, m_x)


def _fill_weight(wt_ref, w_scr, dst0, src0, width, scale=None):
    k = w_scr.shape[0]
    for c in range(0, width, LANE):
        n = min(LANE, width - c)
        blk = wt_ref[0, src0 + c:src0 + c + n, :]
        if scale is not None:
            blk = blk * scale
        if n < LANE:
            blk = jnp.concatenate([blk, jnp.zeros((LANE - n, k), F32)], axis=0)
        w_scr[:, dst0 + c:dst0 + c + LANE] = blk.T.astype(BF16)


def _stream_specs(tile):
    return [pl.BlockSpec((1, tile, D_MODEL), lambda b, j: (b, j, 0))]


def _split_stream_specs():
    assert X_TAIL + CTX_LEN == TOK_TILE and SEQ % X_TAIL == 0
    return [pl.BlockSpec((1, TOK_TILE, D_MODEL), lambda b, j: (b, jnp.minimum(j, N_XTILES - 1), 0)),
            pl.BlockSpec((1, X_TAIL, D_MODEL), lambda b, j: (b, SEQ // X_TAIL - 1, 0)),
            pl.BlockSpec((1, CTX_LEN, D_MODEL), lambda b, j: (b, 0, 0))]


def _stream_tile(h_refs, j, tile_scr):
    if len(h_refs) == 1:
        return h_refs[0][0]
    x_ref, tail_ref, ctx_ref = h_refs

    @pl.when(j < N_XTILES)
    def _():
        tile_scr[...] = x_ref[0]

    @pl.when(j == N_XTILES)
    def _():
        tile_scr[:X_TAIL] = tail_ref[0]
        tile_scr[X_TAIL:] = ctx_ref[0]

    return tile_scr[...]


def _inproj_kernel(*refs, n_h):
    h_refs, refs = refs[:n_h], refs[n_h:]
    (mod_ref, nw_ref, w_ref, cos_ref, sin_ref,
     four_ref, nq_ref, nk_ref, nvt_ref, ng_ref,
     gq_ref, gk_ref, gv_ref, gg_ref, gz_ref, hx_scr, w_scr, wvt_scr) = refs[:18]
    b = pl.program_id(0)
    j = pl.program_id(1)

    @pl.when((b == 0) & (j == 0))
    def _():
        src = np.concatenate([[0], np.cumsum(IN_SIZES)])
        _fill_weight(w_ref, w_scr, COL_FOUR, src[0], 2 * D_FOUR)
        _fill_weight(w_ref, w_scr, COL_NA, src[2], D_NA, NA_HEAD_DIM ** -0.5 * LOG2E)
        _fill_weight(w_ref, w_scr, COL_NA + D_NA, src[3], D_NA)
        _fill_weight(w_ref, w_scr, COL_NA + 2 * D_NA, src[5], D_NA)
        n_freq = GLA_DK // 4
        pad = jnp.zeros((LANE - GLA_HEADS * 2 * n_freq, D_MODEL), F32)
        for dst0, src0 in ((COL_GQ, src[6]), (COL_GK, src[7])):
            for part, offs in enumerate(_qk_pieces()):
                tile = jnp.concatenate(
                    [w_ref[0, src0 + o:src0 + o + n_freq, :] for o in offs] + [pad], axis=0)
                w_scr[:, dst0 + part * LANE:dst0 + (part + 1) * LANE] = tile.T.astype(BF16)
        _fill_weight(w_ref, w_scr, COL_GV, src[8], D_GLA_V)
        _fill_weight(w_ref, w_scr, COL_GG, src[9], D_GLA_V)
        _fill_weight(w_ref, w_scr, COL_Z, src[10], 2 * GLA_RANK)
        for p in range(NA_HEADS // 2):
            blk = w_ref[0, src[4] + p * LANE:src[4] + (p + 1) * LANE, :]
            wvt_scr[p * NA_VT_PAIR:p * NA_VT_PAIR + LANE, :] = blk.astype(BF16)
            wvt_scr[p * NA_VT_PAIR + LANE:(p + 1) * NA_VT_PAIR, :] = jnp.zeros(
                (NA_VT_PAIR - LANE, D_MODEL), BF16)

    x = _stream_tile(h_refs, j, refs[18] if n_h > 1 else None)
    ms = jnp.mean(x * x, axis=-1, keepdims=True)
    y = x * lax.rsqrt(ms + EPS) * nw_ref[0]
    shift = _mod_rows(mod_ref, b, j * TOK_TILE, TOK_TILE, 0)
    scale = _mod_rows(mod_ref, b, j * TOK_TILE, TOK_TILE, D_MODEL)
    hx_scr[...] = (y * (1.0 + scale) + shift).astype(BF16)

    def proj(c0, width):
        return _dot(hx_scr[...], w_scr[:, c0:c0 + width])

    four_ref[0] = proj(COL_FOUR, 2 * D_FOUR).astype(BF16)
    nq_ref[0] = proj(COL_NA, D_NA).astype(BF16)
    nk_ref[0] = proj(COL_NA + D_NA, D_NA).astype(BF16)
    ones_rows = lax.broadcasted_iota(jnp.int32, (NA_VT_ROWS, 1), 0) % NA_VT_PAIR >= LANE
    nvt_ref[0] = (_dot_nt(wvt_scr[...], hx_scr[...]) + jnp.where(ones_rows, 1.0, 0.0)).astype(BF16)
    ng_ref[0] = proj(COL_NA + 2 * D_NA, D_NA).astype(BF16)
    gv_ref[0] = proj(COL_GV, D_GLA_V).astype(BF16)
    gg_ref[0] = proj(COL_GG, D_GLA_V).astype(BF16)
    gz_ref[0] = proj(COL_Z, Z_PAD)

    cos = cos_ref[...]
    sin = sin_ref[...]

    def rope(c0, out_ref, scale_out):
        u = proj(c0, GLA_KP)
        u1, u2 = u[:, :LANE], u[:, LANE:]
        out_ref[0, :, :LANE] = (u1 * cos - u2 * sin) * scale_out
        out_ref[0, :, LANE:] = (u1 * sin + u2 * cos) * scale_out

    rope(COL_GQ, gq_ref, GLA_DK ** -0.5)
    rope(COL_GK, gk_ref, 1.0)


def _inproj(h_in, mod, l, norm_w, w_in, cos_t, sin_t):
    nb = h_in[0].shape[0]
    nt = T_ALL // TOK_TILE
    split = len(h_in) > 1
    tok = lambda width: pl.BlockSpec((1, TOK_TILE, width), lambda b, j: (b, j, 0))
    layer = lambda a: pl.BlockSpec((1,) + a.shape[1:], lambda b, j: (l,) + (0,) * (a.ndim - 1))
    out_widths = [(2 * D_FOUR, BF16), (D_NA, BF16), (D_NA, BF16), (None, BF16), (D_NA, BF16),
                  (GLA_KP, F32), (GLA_KP, F32), (D_GLA_V, BF16), (D_GLA_V, BF16), (Z_PAD, F32)]
    vt_spec = pl.BlockSpec((1, NA_VT_ROWS, TOK_TILE), lambda b, j: (b, 0, j))
    vt_shape = jax.ShapeDtypeStruct((nb, NA_VT_ROWS, T_ALL), BF16)
    return pl.pallas_call(
        functools.partial(_inproj_kernel, n_h=len(h_in)),
        grid=(nb, nt),
        in_specs=(_split_stream_specs() if split else _stream_specs(TOK_TILE)) + [
            layer(mod),
            layer(norm_w),
            layer(w_in),
            pl.BlockSpec((TOK_TILE, LANE), lambda b, j: (j, 0)),
            pl.BlockSpec((TOK_TILE, LANE), lambda b, j: (j, 0)),
        ],
        out_specs=[vt_spec if w is None else tok(w) for w, _ in out_widths],
        out_shape=[vt_shape if w is None else jax.ShapeDtypeStruct((nb, T_ALL, w), dt)
                   for w, dt in out_widths],
        scratch_shapes=[pltpu.VMEM((TOK_TILE, D_MODEL), BF16), pltpu.VMEM((D_MODEL, N_COLS), BF16),
                        pltpu.VMEM((NA_VT_ROWS, D_MODEL), BF16)]
        + ([pltpu.VMEM((TOK_TILE, D_MODEL), F32)] if split else []),
        compiler_params=pltpu.CompilerParams(
            dimension_semantics=("arbitrary", "arbitrary"), vmem_limit_bytes=VMEM_LIMIT),
        name="inproj",
    )(*h_in, mod, norm_w, w_in, cos_t, sin_t)


def _fourier_kernel(p_ref, cbd_ref, sbd_ref, cl_ref, sl_ref, cc_ref, sc_ref, wf_ref, o_ref,
                    uc_scr, us_scr):
    u = p_ref[0, :, :D_FOUR]
    uc_scr[...] = _dot(u, cbd_ref[...]).astype(BF16)
    us_scr[...] = _dot(u, sbd_ref[...]).astype(BF16)

    def finish(f, r0, n):
        four = _dot(f.astype(BF16), wf_ref[...])
        gate = p_ref[0, pl.ds(r0, n), D_FOUR:].astype(F32)
        o_ref[0, pl.ds(r0, n), :] = (four * _silu(gate)).astype(BF16)

    tile = 2048

    def body(i, carry):
        r0 = pl.multiple_of(i * tile, tile)
        f = (_dot(cl_ref[pl.ds(r0, tile), :], uc_scr[:SEQ, :])
             - _dot(sl_ref[pl.ds(r0, tile), :], us_scr[:SEQ, :]))
        finish(f, r0, tile)
        return carry

    lax.fori_loop(0, SEQ // tile, body, 0)
    f_c = _dot(cc_ref[...], uc_scr[SEQ:, :]) - _dot(sc_ref[...], us_scr[SEQ:, :])
    finish(f_c, SEQ, CTX_LEN)


def _fourier(p_four, tabs, wf):
    nb = p_four.shape[0]
    full = lambda a: pl.BlockSpec(a.shape, lambda b: (0,) * a.ndim)
    return pl.pallas_call(
        _fourier_kernel,
        grid=(nb,),
        in_specs=[pl.BlockSpec((1, T_ALL, 2 * D_FOUR), lambda b: (b, 0, 0))]
        + [full(t) for t in tabs] + [full(wf)],
        out_specs=pl.BlockSpec((1, T_ALL, D_FOUR), lambda b: (b, 0, 0)),
        out_shape=jax.ShapeDtypeStruct((nb, T_ALL, D_FOUR), BF16),
        scratch_shapes=[pltpu.VMEM((T_ALL, D_FOUR), BF16), pltpu.VMEM((T_ALL, D_FOUR), BF16)],
        compiler_params=pltpu.CompilerParams(
            dimension_semantics=("arbitrary",), vmem_limit_bytes=VMEM_LIMIT),
        name="fourier_mix",
    )(p_four, *tabs, wf)


def _na_build_bias(ra_ref, rb_ref, tbl_ref):
    w = GRID_W
    ck = lax.broadcasted_iota(jnp.int32, (w, 2 * w), 0)
    ln = lax.broadcasted_iota(jnp.int32, (w, 2 * w), 1)
    cq = ln % w
    cs = jnp.clip(cq - NA_KW // 2, 0, w - NA_KW)
    col_ok = (ck >= cs) & (ck < cs + NA_KW)
    left = ln < w
    neg = jnp.full((w, 2 * w), -jnp.inf, F32)
    offs = _na_row_offsets()
    cache = {}
    for typ in range(3):
        for ap in range(NA_QROWS // 2):
            for jr in range(NA_KROWS):
                d0 = offs[typ][2 * ap][jr]
                d1 = offs[typ][2 * ap + 1][jr]
                for h in range(NA_HEADS):
                    key = (h, d0, d1)
                    if key not in cache:
                        if d0 is None and d1 is None:
                            cache[key] = neg
                        else:
                            ok = col_ok
                            if d0 is None:
                                row = rb_ref[h, d1:d1 + 1, :]
                                ok = ok & jnp.logical_not(left)
                            elif d1 is None:
                                row = ra_ref[h, d0:d0 + 1, :]
                                ok = ok & left
                            else:
                                row = ra_ref[h, d0:d0 + 1, :] + rb_ref[h, d1:d1 + 1, :]
                            skew = pltpu.roll(jnp.broadcast_to(row * LOG2E, (w, 2 * w)),
                                              2 * w - (NA_KW - 1), 1, stride=1, stride_axis=0)
                            cache[key] = jnp.where(ok, skew, neg)
                    tbl_ref[h * 3 + typ, jr * w:(jr + 1) * w, ap * 2 * w:(ap + 1) * 2 * w] = cache[key]


def _na_kernel(q_ref, k_ref, vt_ref, g_ref, ra_ref, rb_ref, o_ref, bias_ref, s_scr):
    lane = lax.broadcasted_iota(jnp.int32, (1, LANE), 1)
    first_head_lanes = lane < NA_HEAD_DIM
    first_head_rows = lax.broadcasted_iota(jnp.int32, (LANE, 1), 0) < NA_HEAD_DIM
    heads = range(NA_HEADS)
    cols = [slice(h // 2 * LANE, (h // 2 + 1) * LANE) for h in heads]

    @pl.when(pl.program_id(0) == 0)
    def _():
        _na_build_bias(ra_ref, rb_ref, bias_ref)

    def block_geometry(j):
        if isinstance(j, int):
            k0 = min(max(j * NA_QROWS - NA_KH // 2, 0), GRID_H - NA_KROWS) * GRID_W
            return pl.ds(j * NA_QB, NA_QB), k0, (0 if j == 0 else 2 if j == NA_NBLK - 1 else 1)
        k0 = jnp.clip(j * NA_QROWS - NA_KH // 2, 0, GRID_H - NA_KROWS) * GRID_W
        typ = jnp.where(j == 0, 0, jnp.where(j == NA_NBLK - 1, 2, 1))
        return pl.ds(pl.multiple_of(j * NA_QB, NA_QB), NA_QB), pl.multiple_of(k0, NA_QB), typ

    def scores(j, buf, local):
        qrows, k0, typ = block_geometry(j)
        for h in heads:
            qp = q_ref[0, qrows, cols[h]]
            keep = first_head_lanes if h % 2 == 0 else jnp.logical_not(first_head_lanes)
            qm = jnp.where(keep, qp, jnp.zeros_like(qp))
            s_scr[buf, h, :CTX_LEN, :] = _dot_nt(k_ref[0, SEQ:, cols[h]], qm)
            if local:
                s_scr[buf, h, CTX_LEN:, :] = (
                    _dot_nt(k_ref[0, pl.ds(k0, NA_KB), cols[h]], qm) + bias_ref[h * 3 + typ])

    def finish(j, buf, local):
        qrows, k0, _ = block_geometry(j)
        n_keys = CTX_LEN + NA_KB if local else CTX_LEN
        e = []
        for h in heads:
            s = s_scr[buf, h, :n_keys, :]
            e.append(jnp.exp2((s - _reduce_rows(s, jnp.max)).astype(BF16)))
        for p in range(NA_HEADS // 2):
            vrows = slice(p * NA_VT_PAIR, (p + 1) * NA_VT_PAIR)
            acc = None
            for h in (2 * p, 2 * p + 1):
                o = _dot(vt_ref[0, vrows, SEQ:], e[h][:CTX_LEN])
                if local:
                    o = o + _dot(vt_ref[0, vrows, pl.ds(k0, NA_KB)], e[h][CTX_LEN:])
                o = o[:LANE] / o[LANE:LANE + 1]
                acc = o if h % 2 == 0 else jnp.where(first_head_rows, acc, o)
            gate = g_ref[0, qrows, cols[2 * p]].astype(F32)
            o_ref[0, qrows, cols[2 * p]] = (acc.T * _silu(gate)).astype(BF16)

    def pair_body(i, carry):
        scores(2 * i + 1, 1, True)
        finish(2 * i, 0, True)
        scores(2 * i + 2, 0, True)
        finish(2 * i + 1, 1, True)
        return carry

    scores(0, 0, True)
    lax.fori_loop(0, NA_NBLK // 2 - 1, pair_body, 0)
    scores(NA_NBLK - 1, 1, True)
    finish(NA_NBLK - 2, 0, True)
    scores(NA_NBLK, 0, False)
    finish(NA_NBLK - 1, 1, True)
    finish(NA_NBLK, 0, False)


def _na(nq, nk, nvt, ng, rpb_a, rpb_b):
    nb = nq.shape[0]
    seq = pl.BlockSpec((1, T_ALL, D_NA), lambda b: (b, 0, 0))
    seq_t = pl.BlockSpec((1, NA_VT_ROWS, T_ALL), lambda b: (b, 0, 0))
    tab = pl.BlockSpec(rpb_a.shape, lambda b: (0, 0, 0))
    return pl.pallas_call(
        _na_kernel,
        grid=(nb,),
        in_specs=[seq, seq, seq_t, seq, tab, tab],
        out_specs=seq,
        out_shape=jax.ShapeDtypeStruct((nb, T_ALL, D_NA), BF16),
        scratch_shapes=[pltpu.VMEM((NA_HEADS * 3, NA_KB, NA_QB), F32),
                        pltpu.VMEM((2, NA_HEADS, CTX_LEN + NA_KB, NA_QB), F32)],
        compiler_params=pltpu.CompilerParams(
            dimension_semantics=("arbitrary",), vmem_limit_bytes=VMEM_LIMIT),
        name="nbr_attention",
    )(nq, nk, nvt, ng, rpb_a, rpb_b)


def _gla_kernel(q_ref, k_ref, v_ref, g_ref, z_ref, wa_ref, ba_ref, nw_ref, o_ref,
                st_scr, o_scr, qd_scr, kd_scr, ke_scr, vt_scr, dec_scr):
    C = GLA_CHUNK
    TR = GLA_TILE
    n_sub = TR // C
    kl = lax.broadcasted_iota(jnp.int32, (1, GLA_KP), 1) % LANE
    k_head = jnp.where(kl < GLA_HEADS * (GLA_DK // 2), kl // (GLA_DK // 2), GLA_HEADS)
    v_head = lax.broadcasted_iota(jnp.int32, (1, D_GLA_V), 1) // GLA_DV
    st_mask = (lax.broadcasted_iota(jnp.int32, (D_GLA_V, 1), 0) // GLA_DV) == k_head
    ti4 = lax.broadcasted_iota(jnp.int32, (C, GLA_HEADS * C), 0)
    si4 = lax.broadcasted_iota(jnp.int32, (C, GLA_HEADS * C), 1) % C
    tr = lax.broadcasted_iota(jnp.int32, (TR, TR), 0)
    sr = lax.broadcasted_iota(jnp.int32, (TR, TR), 1)
    same_chunk = (tr // C) == (sr // C)
    tri = [jnp.where(same_chunk & (sr <= tr), 1.0, 0.0).astype(BF16),
           jnp.where(same_chunk & (sr >= tr), 1.0, 0.0).astype(BF16)]

    def tile_start(i):
        return i * TR if isinstance(i, int) else pl.multiple_of(i * TR, TR)

    def tile_rows(i):
        return pl.ds(tile_start(i), TR)

    def decay_sums(i):
        z = z_ref[0, tile_rows(i), :].astype(BF16)
        g = _dot(z, wa_ref[...]) + ba_ref[...]
        la = (jnp.minimum(g, 0.0) - jnp.log(1.0 + jnp.exp(-jnp.abs(g)))) * (1.0 / GLA_TAU)
        sums = []
        for d in range(2):
            la_hi, la_lo = _split_bf16(la[:, d * GLA_KP:(d + 1) * GLA_KP])
            sums.append(_dot(tri[d], la_hi) + _dot(tri[d], la_lo))
        return sums

    def decayed_operands(i, sums):
        rows = tile_rows(i)
        q = q_ref[0, rows, :]
        k = k_ref[0, rows, :]
        for d in range(2):
            bcum = sums[d]
            ends = [bcum[(c + 1) * C - 1:(c + 1) * C] if d == 0 else bcum[c * C:c * C + 1]
                    for c in range(n_sub)]
            b_end = jnp.concatenate([jnp.broadcast_to(e, (C, GLA_KP)) for e in ends], axis=0)
            qd_scr[d, rows, :] = (q * jnp.exp(bcum)).astype(BF16)
            kd_scr[d, rows, :] = (k * jnp.exp(-bcum)).astype(BF16)
            ke_scr[d, rows, :] = (k * jnp.exp(b_end - bcum)).astype(BF16)
            for c in range(n_sub):
                dec_scr[d, pl.ds(i * n_sub + c, 1), :] = jnp.exp(ends[c])

    def intra_chunk(i):
        r0 = tile_start(i)
        v_bd = []
        for c in range(n_sub):
            v_c = v_ref[0, pl.ds(r0 + c * C, C), :]
            v_bd.append(jnp.concatenate(
                [jnp.where(v_head == h, v_c, jnp.zeros_like(v_c)) for h in range(GLA_HEADS)], axis=0))
            vt_scr[i * n_sub + c] = v_c.astype(F32).T.astype(BF16)
        scores = []
        for d in range(2):
            for c in range(n_sub):
                crow = pl.ds(r0 + c * C, C)
                k_c = kd_scr[d, crow, :]
                k_bd = jnp.concatenate(
                    [jnp.where(k_head == h, k_c, jnp.zeros_like(k_c)) for h in range(GLA_HEADS)], axis=0)
                scores.append(_dot_nt(qd_scr[d, crow, :], k_bd))
        for d in range(2):
            causal = (si4 <= ti4) if d == 0 else (si4 >= ti4)
            for c in range(n_sub):
                a = jnp.where(causal, scores[d * n_sub + c], 0.0).astype(BF16)
                o_scr[d, pl.ds(r0 + c * C, C), :] = _dot(a, v_bd[c])

    def tile_body(i, carry):
        sums = decay_sums(i + 1)
        intra_chunk(i)
        decayed_operands(i + 1, sums)
        return carry

    n_tiles = T_ALL // TR
    decayed_operands(0, decay_sums(0))
    lax.fori_loop(0, n_tiles - 1, tile_body, 0, unroll=2)
    intra_chunk(n_tiles - 1)

    st_scr[...] = jnp.zeros_like(st_scr)
    n_all = N_CHUNKS_C + N_CHUNKS_X

    def step(i, carry):
        in_ctx = i < N_CHUNKS_C
        idx_f = jnp.where(in_ctx, N_CHUNKS_X + i, i - N_CHUNKS_C)
        idx_b = jnp.where(in_ctx, N_CHUNKS_X + N_CHUNKS_C - 1 - i, n_all - 1 - i)
        half = D_GLA_V // 2
        dirs = ((0, idx_f), (1, idx_b))
        offs = [pl.ds(pl.multiple_of(idx * C, C), C) for _, idx in dirs]
        upd = []
        for (d, idx), off in zip(dirs, offs):
            ke = ke_scr[d, off, :]
            upd.append(jnp.concatenate(
                [_dot(vt_scr[idx, :half, :], ke), _dot(vt_scr[idx, half:, :], ke)], axis=0))
        for (d, idx), off in zip(dirs, offs):
            st = st_scr[d]
            o_scr[d, off, :] += _dot_nt(qd_scr[d, off, :], st.astype(BF16))
            st_scr[d] = st * dec_scr[d, pl.ds(idx, 1), :] + jnp.where(st_mask, upd[d], 0.0)
        return carry

    lax.fori_loop(0, n_all, step, 0, unroll=4)

    head_ones = jnp.where(
        (lax.broadcasted_iota(jnp.int32, (D_GLA_V, 1), 0) // GLA_DV) == v_head, 1.0, 0.0).astype(BF16)
    tile = 256

    def fin(i, carry):
        r0 = pl.multiple_of(i * tile, tile)
        o = o_scr[0, pl.ds(r0, tile), :] + o_scr[1, pl.ds(r0, tile), :]
        sq_hi, sq_lo = _split_bf16(o * o)
        ms = (_dot(sq_hi, head_ones) + _dot(sq_lo, head_ones)) * (1.0 / GLA_DV)
        y = o * lax.rsqrt(ms + EPS) * nw_ref[...]
        gate = g_ref[0, pl.ds(r0, tile), :].astype(F32)
        o_ref[0, pl.ds(r0, tile), :] = (y * _silu(gate)).astype(BF16)
        return carry

    lax.fori_loop(0, T_ALL // tile, fin, 0, unroll=3)


def _gla(gq, gk, gv, gg, gz, wa, ba, gnw):
    nb = gq.shape[0]
    seq = lambda w: pl.BlockSpec((1, T_ALL, w), lambda b: (b, 0, 0))
    full = lambda a: pl.BlockSpec(a.shape, lambda b: (0,) * a.ndim)
    return pl.pallas_call(
        _gla_kernel,
        grid=(nb,),
        in_specs=[seq(GLA_KP), seq(GLA_KP), seq(D_GLA_V), seq(D_GLA_V), seq(Z_PAD),
                  full(wa), full(ba), full(gnw)],
        out_specs=seq(D_GLA_V),
        out_shape=jax.ShapeDtypeStruct((nb, T_ALL, D_GLA_V), BF16),
        scratch_shapes=[pltpu.VMEM((2, D_GLA_V, GLA_KP), F32),
                        pltpu.VMEM((2, T_ALL, D_GLA_V), F32),
                        pltpu.VMEM((2, T_ALL, GLA_KP), BF16),
                        pltpu.VMEM((2, T_ALL, GLA_KP), BF16),
                        pltpu.VMEM((2, T_ALL, GLA_KP), BF16),
                        pltpu.VMEM((T_ALL // GLA_CHUNK, D_GLA_V, GLA_CHUNK), BF16),
                        pltpu.VMEM((2, T_ALL // GLA_CHUNK, GLA_KP), F32)],

        compiler_params=pltpu.CompilerParams(
            dimension_semantics=("arbitrary",), vmem_limit_bytes=VMEM_LIMIT),
        name="gla_mix",
    )(gq, gk, gv, gg, gz, wa, ba, gnw)


def _outproj_kernel(*refs, n_h, tile, final):
    h_refs, refs = refs[:n_h], refs[n_h:]
    mf_ref, mn_ref, mg_ref, w_ref, mod_ref, nf_ref, o_ref, w_scr = refs[:8]
    b = pl.program_id(0)
    j = pl.program_id(1)

    @pl.when((b == 0) & (j == 0))
    def _():
        for r in range(0, D_MODEL, 256):
            w_scr[r:r + 256, :] = w_ref[0, r:r + 256, :].astype(BF16)

    acc = (_dot(mf_ref[0], w_scr[:D_FOUR, :])
           + _dot(mn_ref[0], w_scr[D_FOUR:D_FOUR + D_NA, :])
           + _dot(mg_ref[0], w_scr[D_FOUR + D_NA:, :]))
    gate = _mod_rows(mod_ref, b, j * tile, tile, 2 * D_MODEL)
    x = _stream_tile(h_refs, j, refs[8] if n_h > 1 else None) + gate * acc
    if final:
        ms = jnp.mean(x * x, axis=-1, keepdims=True)
        x = x * lax.rsqrt(ms + EPS) * nf_ref[...]
    o_ref[0] = x


def _outproj(h_in, mf, mn, mg, w_out, mod, l, nf, final):
    nb = h_in[0].shape[0]
    tile = 512 if final else TOK_TILE
    rows = SEQ if final else T_ALL
    if final:
        h_in = h_in[:1]
    split = len(h_in) > 1
    tok = lambda width: pl.BlockSpec((1, tile, width), lambda b, j: (b, j, 0))
    layer = lambda a: pl.BlockSpec((1,) + a.shape[1:], lambda b, j: (l,) + (0,) * (a.ndim - 1))
    return pl.pallas_call(
        functools.partial(_outproj_kernel, n_h=len(h_in), tile=tile, final=final),
        grid=(nb, rows // tile),
        in_specs=(_split_stream_specs() if split else _stream_specs(tile))
        + [tok(D_FOUR), tok(D_NA), tok(D_GLA_V),
           layer(w_out), layer(mod), pl.BlockSpec((1, D_MODEL), lambda b, j: (0, 0))],
        out_specs=tok(D_MODEL),
        out_shape=jax.ShapeDtypeStruct((nb, rows, D_MODEL), F32),
        scratch_shapes=[pltpu.VMEM((D_MODEL, D_MODEL), BF16)]
        + ([pltpu.VMEM((TOK_TILE, D_MODEL), F32)] if split else []),
        compiler_params=pltpu.CompilerParams(
            dimension_semantics=("arbitrary", "arbitrary"), vmem_limit_bytes=VMEM_LIMIT),
        name="outproj_final" if final else "outproj",
    )(*h_in, mf, mn, mg, w_out, mod, nf)


def _decay_weights(w_f, b_f, w_b, b_b):
    n_freq = GLA_DK // 4
    lane_pad = lambda a: jnp.zeros(a.shape[:-1] + (LANE - GLA_HEADS * 2 * n_freq,), a.dtype)
    lanes = lambda a: jnp.concatenate(
        [blk for offs in _qk_pieces() for blk in [a[..., o:o + n_freq] for o in offs] + [lane_pad(a)]],
        axis=-1)
    zero = jnp.zeros_like(lanes(w_f))
    top = jnp.concatenate([lanes(w_f), zero], axis=-1)
    bot = jnp.concatenate([zero, lanes(w_b)], axis=-1)
    pad = jnp.zeros((w_f.shape[0], Z_PAD - 2 * GLA_RANK, 2 * GLA_KP), w_f.dtype)
    wa = jnp.concatenate([top, bot, pad], axis=1).astype(BF16)
    ba = jnp.concatenate([lanes(b_f), lanes(b_b)], axis=-1)[:, None, :]
    return wa, ba


def kernel(x, c, ctx, c_ctx, w_ada, b_ada, norm_w, w_in, w_four, rpb, w_alpha_fwd, b_alpha_fwd,
           w_alpha_bwd, b_alpha_bwd, gla_norm_w, w_out, norm_f):
    nb = x.shape[0]
    depth = w_in.shape[0]
    assert x.shape == (nb, SEQ, D_MODEL) and ctx.shape == (nb, CTX_LEN, D_MODEL) and nb <= 8

    w_in_t = jnp.swapaxes(w_in, 1, 2)
    w_four_b = w_four.astype(BF16)
    norm_w3 = norm_w.reshape(depth, 1, D_MODEL)
    wa, ba = _decay_weights(w_alpha_fwd, b_alpha_fwd, w_alpha_bwd, b_alpha_bwd)
    gnw = jnp.tile(gla_norm_w, (1, GLA_HEADS)).reshape(depth, 1, D_GLA_V)
    n_off = 2 * NA_KW - 1
    rpb_rev = rpb[..., ::-1]
    rpb_a = jnp.pad(rpb_rev, ((0, 0), (0, 0), (0, 1), (0, LANE - n_off)))
    rpb_b = jnp.pad(rpb_rev, ((0, 0), (0, 0), (0, 1), (GRID_W, LANE - GRID_W - n_off)))

    cos_t, sin_t = (jnp.asarray(t) for t in _rope_tables())
    c64, s64 = _dft_tables(FOUR_DIM)
    cl, sl = _dft_tables(SEQ)
    ccx, scx = _dft_tables(CTX_LEN)
    tabs = [jnp.asarray(t, dtype=F32).astype(BF16) for t in
            (_block_diag(c64, FOUR_GROUPS), _block_diag(s64, FOUR_GROUPS), cl, sl, ccx, scx)]

    cc = jnp.zeros((16, D_MODEL), F32).at[:nb].set(c).at[8].set(c_ctx)
    mod = _modulation(cc, w_ada, b_ada)

    h_in = (x, x, ctx)
    for l in range(depth):
        final = l == depth - 1
        (p_four, nq, nk, nvt, ng, gq, gk, gv, gg, gz) = _inproj(
            h_in, mod, l, norm_w3, w_in_t, cos_t, sin_t)
        mf = _fourier(p_four, tabs, w_four_b[l])
        mn = _na(nq, nk, nvt, ng, rpb_a[l], rpb_b[l])
        mg = _gla(gq, gk, gv, gg, gz, wa[l], ba[l], gnw[l])
        h = _outproj(h_in, mf, mn, mg, w_out, mod, l, norm_f.reshape(1, D_MODEL), final)
        h_in = (h,)
    return h
```

```python
import functools

import numpy as np
import jax
import jax.numpy as jnp
from jax import lax
from jax.experimental import pallas as pl
from jax.experimental.pallas import tpu as pltpu

F32 = jnp.float32
BF16 = jnp.bfloat16

D_MODEL = 1024
SEQ = 2048
CTX_LEN = 256
T_ALL = SEQ + CTX_LEN
GRID_W = 64
GRID_H = SEQ // GRID_W
EPS = 1e-6

FOUR_GROUPS, FOUR_DIM = 4, 64
D_FOUR = FOUR_GROUPS * FOUR_DIM
NA_HEADS, NA_HEAD_DIM = 6, 64
D_NA = NA_HEADS * NA_HEAD_DIM
NA_KH, NA_KW = 8, 16
GLA_HEADS, GLA_DK, GLA_DV = 4, 48, 96
D_GLA_K = GLA_HEADS * GLA_DK
D_GLA_V = GLA_HEADS * GLA_DV
GLA_RANK = 16
GLA_TAU = 16.0
GLA_CHUNK = 64
ROPE_BASE = 10000.0
LOG2E = 1.4426950408889634

IN_SIZES = (D_FOUR, D_FOUR, D_NA, D_NA, D_NA, D_NA, D_GLA_K, D_GLA_K, D_GLA_V, D_GLA_V,
            GLA_RANK, GLA_RANK)
D_IN = sum(IN_SIZES)

LANE = 128
BF16_ROWS = 16
CTX_MOD_ROW = 8
MOD_ROWS = 16
GLA_KP = 2 * LANE
Z_PAD = LANE
N_CHUNKS_X = SEQ // GLA_CHUNK
N_CHUNKS_C = CTX_LEN // GLA_CHUNK
GLA_TILE = 256

NA_QROWS = 4
NA_KROWS = NA_QROWS + NA_KH
NA_QB = NA_QROWS * GRID_W
NA_KB = NA_KROWS * GRID_W
NA_NBLK = GRID_H // NA_QROWS
NA_VT_PAIR = LANE + BF16_ROWS
NA_VT_ROWS = NA_HEADS // 2 * NA_VT_PAIR

TOK_TILE = 768
N_XTILES = SEQ // TOK_TILE
X_TAIL = SEQ - N_XTILES * TOK_TILE
VMEM_LIMIT = 56 * 1024 * 1024

COL_FOUR = 0
COL_NA = COL_FOUR + 2 * D_FOUR
COL_GQ = COL_NA + 3 * D_NA
COL_GK = COL_GQ + GLA_KP
COL_GV = COL_GK + GLA_KP
COL_GG = COL_GV + D_GLA_V
COL_Z = COL_GG + D_GLA_V
N_COLS = COL_Z + Z_PAD


def _dot(a, b):
    return jnp.dot(a, b, preferred_element_type=F32)


def _dot_nt(a, b):
    return lax.dot_general(a, b, (((1,), (1,)), ((), ())), preferred_element_type=F32)


def _silu(x):
    return x / (1.0 + jnp.exp(-x))


def _reduce_rows(x, op):
    r, n = x.shape
    part = op(x.reshape(r // 8, 8, n), axis=0)
    return op(part, axis=0, keepdims=True)


def _split_bf16(x):
    hi = x.astype(BF16)
    lo = (x - hi.astype(F32)).astype(BF16)
    return hi, lo


def _qk_pieces():
    n_freq = GLA_DK // 4
    return [[h * GLA_DK + half * 2 * n_freq + part * n_freq
             for h in range(GLA_HEADS) for half in range(2)] for part in range(2)]


def _rope_tables():
    n_freq = GLA_DK // 4
    inv = ROPE_BASE ** (-np.arange(n_freq, dtype=np.float64) / n_freq)
    pos = np.arange(SEQ)
    prow, pcol = pos // GRID_W, pos % GRID_W
    cos = np.ones((T_ALL, LANE), np.float64)
    sin = np.zeros((T_ALL, LANE), np.float64)
    for h in range(GLA_HEADS):
        for half, p in enumerate((prow, pcol)):
            ang = p[:, None] * inv[None, :]
            lo = h * 2 * n_freq + half * n_freq
            cos[:SEQ, lo:lo + n_freq] = np.cos(ang)
            sin[:SEQ, lo:lo + n_freq] = np.sin(ang)
    return cos.astype(np.float32), sin.astype(np.float32)


def _dft_tables(n):
    idx = (np.arange(n)[:, None] * np.arange(n)[None, :]) % n
    ang = 2.0 * np.pi * idx / n
    return np.cos(ang) / np.sqrt(n), np.sin(ang) / np.sqrt(n)


def _block_diag(m, reps):
    n = m.shape[0]
    out = np.zeros((n * reps, n * reps), m.dtype)
    for r in range(reps):
        out[r * n:(r + 1) * n, r * n:(r + 1) * n] = m
    return out


def _na_row_offsets():
    out = []
    for r0 in (0, NA_QROWS, GRID_H - NA_QROWS):
        k0 = int(np.clip(r0 - NA_KH // 2, 0, GRID_H - NA_KROWS))
        per_type = []
        for a in range(NA_QROWS):
            r = r0 + a
            rs = int(np.clip(r - NA_KH // 2, 0, GRID_H - NA_KH))
            per_type.append([(k0 + j) - r + NA_KH - 1 if rs <= k0 + j < rs + NA_KH else None
                             for j in range(NA_KROWS)])
        out.append(per_type)
    return out


def _mod_kernel(c_ref, w_ref, b_ref, o_ref):
    c = c_ref[...]
    s_hi, s_lo = _split_bf16(_silu(c))
    w_hi, w_lo = _split_bf16(w_ref[0])
    acc = _dot(s_hi, w_hi) + _dot(s_lo, w_hi) + _dot(s_hi, w_lo)
    o_ref[0] = acc + b_ref[0]


def _modulation(cc, w_ada, b_ada):
    depth = w_ada.shape[0]
    n_mod = w_ada.shape[2]
    tile = D_MODEL
    return pl.pallas_call(
        _mod_kernel,
        grid=(depth, n_mod // tile),
        in_specs=[
            pl.BlockSpec((MOD_ROWS, D_MODEL), lambda l, j: (0, 0)),
            pl.BlockSpec((1, D_MODEL, tile), lambda l, j: (l, 0, j)),
            pl.BlockSpec((1, 1, tile), lambda l, j: (l, 0, j)),
        ],
        out_specs=pl.BlockSpec((1, MOD_ROWS, tile), lambda l, j: (l, 0, j)),
        out_shape=jax.ShapeDtypeStruct((depth, MOD_ROWS, n_mod), F32),
        compiler_params=pltpu.CompilerParams(
            dimension_semantics=("arbitrary", "arbitrary"), vmem_limit_bytes=VMEM_LIMIT),
        name="adaln_mod",
    )(cc, w_ada, b_ada.reshape(depth, 1, n_mod))


def _mod_rows(mod_ref, b, row0, n_rows, col0):
    rows = row0 + lax.broadcasted_iota(jnp.int32, (n_rows, 1), 0)
    m_x = mod_ref[0, pl.ds(b, 1), col0:col0 + D_MODEL]
    m_c = mod_ref[0, CTX_MOD_ROW:CTX_MOD_ROW + 1, col0:col0 + D_MODEL]
    return jnp.where(rows >= SEQ, m_c, m_x)


def _fill_weight(wt_ref, w_scr, dst0, src0, width, scale=None):
    k = w_scr.shape[0]
    for c in range(0, width, LANE):
        n = min(LANE, width - c)
        blk = wt_ref[0, src0 + c:src0 + c + n, :]
        if scale is not None:
            blk = blk * scale
        if n < LANE:
            blk = jnp.concatenate([blk, jnp.zeros((LANE - n, k), F32)], axis=0)
        w_scr[:, dst0 + c:dst0 + c + LANE] = blk.T.astype(BF16)


def _stream_specs(tile):
    return [pl.BlockSpec((1, tile, D_MODEL), lambda b, j: (b, j, 0))]


def _split_stream_specs():
    assert X_TAIL + CTX_LEN == TOK_TILE and SEQ % X_TAIL == 0
    return [pl.BlockSpec((1, TOK_TILE, D_MODEL), lambda b, j: (b, jnp.minimum(j, N_XTILES - 1), 0)),
            pl.BlockSpec((1, X_TAIL, D_MODEL), lambda b, j: (b, SEQ // X_TAIL - 1, 0)),
            pl.BlockSpec((1, CTX_LEN, D_MODEL), lambda b, j: (b, 0, 0))]


def _stream_tile(h_refs, j, tile_scr):
    if len(h_refs) == 1:
        return h_refs[0][0]
    x_ref, tail_ref, ctx_ref = h_refs

    @pl.when(j < N_XTILES)
    def _():
        tile_scr[...] = x_ref[0]

    @pl.when(j == N_XTILES)
    def _():
        tile_scr[:X_TAIL] = tail_ref[0]
        tile_scr[X_TAIL:] = ctx_ref[0]

    return tile_scr[...]


def _inproj_kernel(*refs, n_h):
    h_refs, refs = refs[:n_h], refs[n_h:]
    (mod_ref, nw_ref, w_ref, cos_ref, sin_ref,
     four_ref, nq_ref, nk_ref, nvt_ref, ng_ref,
     gq_ref, gk_ref, gv_ref, gg_ref, gz_ref, hx_scr, w_scr, wvt_scr) = refs[:18]
    b = pl.program_id(0)
    j = pl.program_id(1)

    @pl.when((b == 0) & (j == 0))
    def _():
        src = np.concatenate([[0], np.cumsum(IN_SIZES)])
        _fill_weight(w_ref, w_scr, COL_FOUR, src[0], 2 * D_FOUR)
        _fill_weight(w_ref, w_scr, COL_NA, src[2], D_NA, NA_HEAD_DIM ** -0.5 * LOG2E)
        _fill_weight(w_ref, w_scr, COL_NA + D_NA, src[3], D_NA)
        _fill_weight(w_ref, w_scr, COL_NA + 2 * D_NA, src[5], D_NA)
        n_freq = GLA_DK // 4
        pad = jnp.zeros((LANE - GLA_HEADS * 2 * n_freq, D_MODEL), F32)
        for dst0, src0 in ((COL_GQ, src[6]), (COL_GK, src[7])):
            for part, offs in enumerate(_qk_pieces()):
                tile = jnp.concatenate(
                    [w_ref[0, src0 + o:src0 + o + n_freq, :] for o in offs] + [pad], axis=0)
                w_scr[:, dst0 + part * LANE:dst0 + (part + 1) * LANE] = tile.T.astype(BF16)
        _fill_weight(w_ref, w_scr, COL_GV, src[8], D_GLA_V)
        _fill_weight(w_ref, w_scr, COL_GG, src[9], D_GLA_V)
        _fill_weight(w_ref, w_scr, COL_Z, src[10], 2 * GLA_RANK)
        for p in range(NA_HEADS // 2):
            blk = w_ref[0, src[4] + p * LANE:src[4] + (p + 1) * LANE, :]
            wvt_scr[p * NA_VT_PAIR:p * NA_VT_PAIR + LANE, :] = blk.astype(BF16)
            wvt_scr[p * NA_VT_PAIR + LANE:(p + 1) * NA_VT_PAIR, :] = jnp.zeros(
                (NA_VT_PAIR - LANE, D_MODEL), BF16)

    x = _stream_tile(h_refs, j, refs[18] if n_h > 1 else None)
    ms = jnp.mean(x * x, axis=-1, keepdims=True)
    y = x * lax.rsqrt(ms + EPS) * nw_ref[0]
    shift = _mod_rows(mod_ref, b, j * TOK_TILE, TOK_TILE, 0)
    scale = _mod_rows(mod_ref, b, j * TOK_TILE, TOK_TILE, D_MODEL)
    hx_scr[...] = (y * (1.0 + scale) + shift).astype(BF16)

    def proj(c0, width):
        return _dot(hx_scr[...], w_scr[:, c0:c0 + width])

    four_ref[0] = proj(COL_FOUR, 2 * D_FOUR).astype(BF16)
    nq_ref[0] = proj(COL_NA, D_NA).astype(BF16)
    nk_ref[0] = proj(COL_NA + D_NA, D_NA).astype(BF16)
    ones_rows = lax.broadcasted_iota(jnp.int32, (NA_VT_ROWS, 1), 0) % NA_VT_PAIR >= LANE
    nvt_ref[0] = (_dot_nt(wvt_scr[...], hx_scr[...]) + jnp.where(ones_rows, 1.0, 0.0)).astype(BF16)
    ng_ref[0] = proj(COL_NA + 2 * D_NA, D_NA).astype(BF16)
    gv_ref[0] = proj(COL_GV, D_GLA_V).astype(BF16)
    gg_ref[0] = proj(COL_GG, D_GLA_V).astype(BF16)
    gz_ref[0] = proj(COL_Z, Z_PAD)

    cos = cos_ref[...]
    sin = sin_ref[...]

    def rope(c0, out_ref, scale_out):
        u = proj(c0, GLA_KP)
        u1, u2 = u[:, :LANE], u[:, LANE:]
        out_ref[0, :, :LANE] = (u1 * cos - u2 * sin) * scale_out
        out_ref[0, :, LANE:] = (u1 * sin + u2 * cos) * scale_out

    rope(COL_GQ, gq_ref, GLA_DK ** -0.5)
    rope(COL_GK, gk_ref, 1.0)


def _inproj(h_in, mod, l, norm_w, w_in, cos_t, sin_t):
    nb = h_in[0].shape[0]
    nt = T_ALL // TOK_TILE
    split = len(h_in) > 1
    tok = lambda width: pl.BlockSpec((1, TOK_TILE, width), lambda b, j: (b, j, 0))
    layer = lambda a: pl.BlockSpec((1,) + a.shape[1:], lambda b, j: (l,) + (0,) * (a.ndim - 1))
    out_widths = [(2 * D_FOUR, BF16), (D_NA, BF16), (D_NA, BF16), (None, BF16), (D_NA, BF16),
                  (GLA_KP, F32), (GLA_KP, F32), (D_GLA_V, BF16), (D_GLA_V, BF16), (Z_PAD, F32)]
    vt_spec = pl.BlockSpec((1, NA_VT_ROWS, TOK_TILE), lambda b, j: (b, 0, j))
    vt_shape = jax.ShapeDtypeStruct((nb, NA_VT_ROWS, T_ALL), BF16)
    return pl.pallas_call(
        functools.partial(_inproj_kernel, n_h=len(h_in)),
        grid=(nb, nt),
        in_specs=(_split_stream_specs() if split else _stream_specs(TOK_TILE)) + [
            layer(mod),
            layer(norm_w),
            layer(w_in),
            pl.BlockSpec((TOK_TILE, LANE), lambda b, j: (j, 0)),
            pl.BlockSpec((TOK_TILE, LANE), lambda b, j: (j, 0)),
        ],
        out_specs=[vt_spec if w is None else tok(w) for w, _ in out_widths],
        out_shape=[vt_shape if w is None else jax.ShapeDtypeStruct((nb, T_ALL, w), dt)
                   for w, dt in out_widths],
        scratch_shapes=[pltpu.VMEM((TOK_TILE, D_MODEL), BF16), pltpu.VMEM((D_MODEL, N_COLS), BF16),
                        pltpu.VMEM((NA_VT_ROWS, D_MODEL), BF16)]
        + ([pltpu.VMEM((TOK_TILE, D_MODEL), F32)] if split else []),
        compiler_params=pltpu.CompilerParams(
            dimension_semantics=("arbitrary", "arbitrary"), vmem_limit_bytes=VMEM_LIMIT),
        name="inproj",
    )(*h_in, mod, norm_w, w_in, cos_t, sin_t)


def _fourier_kernel(p_ref, cbd_ref, sbd_ref, cl_ref, sl_ref, cc_ref, sc_ref, wf_ref, o_ref,
                    uc_scr, us_scr):
    u = p_ref[0, :, :D_FOUR]
    uc_scr[...] = _dot(u, cbd_ref[...]).astype(BF16)
    us_scr[...] = _dot(u, sbd_ref[...]).astype(BF16)

    def finish(f, r0, n):
        four = _dot(f.astype(BF16), wf_ref[...])
        gate = p_ref[0, pl.ds(r0, n), D_FOUR:].astype(F32)
        o_ref[0, pl.ds(r0, n), :] = (four * _silu(gate)).astype(BF16)

    f_x = _dot(cl_ref[...], uc_scr[:SEQ, :]) - _dot(sl_ref[...], us_scr[:SEQ, :])
    finish(f_x, 0, SEQ)
    f_c = _dot(cc_ref[...], uc_scr[SEQ:, :]) - _dot(sc_ref[...], us_scr[SEQ:, :])
    finish(f_c, SEQ, CTX_LEN)


def _fourier(p_four, tabs, wf):
    nb = p_four.shape[0]
    full = lambda a: pl.BlockSpec(a.shape, lambda b: (0,) * a.ndim)
    return pl.pallas_call(
        _fourier_kernel,
        grid=(nb,),
        in_specs=[pl.BlockSpec((1, T_ALL, 2 * D_FOUR), lambda b: (b, 0, 0))]
        + [full(t) for t in tabs] + [full(wf)],
        out_specs=pl.BlockSpec((1, T_ALL, D_FOUR), lambda b: (b, 0, 0)),
        out_shape=jax.ShapeDtypeStruct((nb, T_ALL, D_FOUR), BF16),
        scratch_shapes=[pltpu.VMEM((T_ALL, D_FOUR), BF16), pltpu.VMEM((T_ALL, D_FOUR), BF16)],
        compiler_params=pltpu.CompilerParams(
            dimension_semantics=("arbitrary",), vmem_limit_bytes=VMEM_LIMIT),
        name="fourier_mix",
    )(p_four, *tabs, wf)


def _na_build_bias(ra_ref, rb_ref, tbl_ref):
    w = GRID_W
    ck = lax.broadcasted_iota(jnp.int32, (w, 2 * w), 0)
    ln = lax.broadcasted_iota(jnp.int32, (w, 2 * w), 1)
    cq = ln % w
    cs = jnp.clip(cq - NA_KW // 2, 0, w - NA_KW)
    col_ok = (ck >= cs) & (ck < cs + NA_KW)
    left = ln < w
    neg = jnp.full((w, 2 * w), -jnp.inf, F32)
    offs = _na_row_offsets()
    cache = {}
    for typ in range(3):
        for ap in range(NA_QROWS // 2):
            for jr in range(NA_KROWS):
                d0 = offs[typ][2 * ap][jr]
                d1 = offs[typ][2 * ap + 1][jr]
                for h in range(NA_HEADS):
                    key = (h, d0, d1)
                    if key not in cache:
                        if d0 is None and d1 is None:
                            cache[key] = neg
                        else:
                            ok = col_ok
                            if d0 is None:
                                row = rb_ref[h, d1:d1 + 1, :]
                                ok = ok & jnp.logical_not(left)
                            elif d1 is None:
                                row = ra_ref[h, d0:d0 + 1, :]
                                ok = ok & left
                            else:
                                row = ra_ref[h, d0:d0 + 1, :] + rb_ref[h, d1:d1 + 1, :]
                            skew = pltpu.roll(jnp.broadcast_to(row * LOG2E, (w, 2 * w)),
                                              2 * w - (NA_KW - 1), 1, stride=1, stride_axis=0)
                            cache[key] = jnp.where(ok, skew, neg)
                    tbl_ref[h * 3 + typ, jr * w:(jr + 1) * w, ap * 2 * w:(ap + 1) * 2 * w] = cache[key]


def _na_kernel(q_ref, k_ref, vt_ref, g_ref, ra_ref, rb_ref, o_ref, bias_ref, s_scr):
    lane = lax.broadcasted_iota(jnp.int32, (1, LANE), 1)
    first_head_lanes = lane < NA_HEAD_DIM
    first_head_rows = lax.broadcasted_iota(jnp.int32, (LANE, 1), 0) < NA_HEAD_DIM
    heads = range(NA_HEADS)
    cols = [slice(h // 2 * LANE, (h // 2 + 1) * LANE) for h in heads]

    @pl.when(pl.program_id(0) == 0)
    def _():
        _na_build_bias(ra_ref, rb_ref, bias_ref)

    def block_geometry(j):
        if isinstance(j, int):
            k0 = min(max(j * NA_QROWS - NA_KH // 2, 0), GRID_H - NA_KROWS) * GRID_W
            return pl.ds(j * NA_QB, NA_QB), k0, (0 if j == 0 else 2 if j == NA_NBLK - 1 else 1)
        k0 = jnp.clip(j * NA_QROWS - NA_KH // 2, 0, GRID_H - NA_KROWS) * GRID_W
        typ = jnp.where(j == 0, 0, jnp.where(j == NA_NBLK - 1, 2, 1))
        return pl.ds(pl.multiple_of(j * NA_QB, NA_QB), NA_QB), pl.multiple_of(k0, NA_QB), typ

    def scores(j, buf, local):
        qrows, k0, typ = block_geometry(j)
        for h in heads:
            qp = q_ref[0, qrows, cols[h]]
            keep = first_head_lanes if h % 2 == 0 else jnp.logical_not(first_head_lanes)
            qm = jnp.where(keep, qp, jnp.zeros_like(qp))
            s_scr[buf, h, :CTX_LEN, :] = _dot_nt(k_ref[0, SEQ:, cols[h]], qm)
            if local:
                s_scr[buf, h, CTX_LEN:, :] = (
                    _dot_nt(k_ref[0, pl.ds(k0, NA_KB), cols[h]], qm) + bias_ref[h * 3 + typ])

    def finish(j, buf, local):
        qrows, k0, _ = block_geometry(j)
        n_keys = CTX_LEN + NA_KB if local else CTX_LEN
        e = []
        for h in heads:
            s = s_scr[buf, h, :n_keys, :]
            e.append(jnp.exp2((s - _reduce_rows(s, jnp.max)).astype(BF16)))
        for p in range(NA_HEADS // 2):
            vrows = slice(p * NA_VT_PAIR, (p + 1) * NA_VT_PAIR)
            acc = None
            for h in (2 * p, 2 * p + 1):
                o = _dot(vt_ref[0, vrows, SEQ:], e[h][:CTX_LEN])
                if local:
                    o = o + _dot(vt_ref[0, vrows, pl.ds(k0, NA_KB)], e[h][CTX_LEN:])
                o = o[:LANE] / o[LANE:LANE + 1]
                acc = o if h % 2 == 0 else jnp.where(first_head_rows, acc, o)
            gate = g_ref[0, qrows, cols[2 * p]].astype(F32)
            o_ref[0, qrows, cols[2 * p]] = (acc.T * _silu(gate)).astype(BF16)

    def pair_body(i, carry):
        scores(2 * i + 1, 1, True)
        finish(2 * i, 0, True)
        scores(2 * i + 2, 0, True)
        finish(2 * i + 1, 1, True)
        return carry

    scores(0, 0, True)
    lax.fori_loop(0, NA_NBLK // 2 - 1, pair_body, 0)
    scores(NA_NBLK - 1, 1, True)
    finish(NA_NBLK - 2, 0, True)
    scores(NA_NBLK, 0, False)
    finish(NA_NBLK - 1, 1, True)
    finish(NA_NBLK, 0, False)


def _na(nq, nk, nvt, ng, rpb_a, rpb_b):
    nb = nq.shape[0]
    seq = pl.BlockSpec((1, T_ALL, D_NA), lambda b: (b, 0, 0))
    seq_t = pl.BlockSpec((1, NA_VT_ROWS, T_ALL), lambda b: (b, 0, 0))
    tab = pl.BlockSpec(rpb_a.shape, lambda b: (0, 0, 0))
    return pl.pallas_call(
        _na_kernel,
        grid=(nb,),
        in_specs=[seq, seq, seq_t, seq, tab, tab],
        out_specs=seq,
        out_shape=jax.ShapeDtypeStruct((nb, T_ALL, D_NA), BF16),
        scratch_shapes=[pltpu.VMEM((NA_HEADS * 3, NA_KB, NA_QB), F32),
                        pltpu.VMEM((2, NA_HEADS, CTX_LEN + NA_KB, NA_QB), F32)],
        compiler_params=pltpu.CompilerParams(
            dimension_semantics=("arbitrary",), vmem_limit_bytes=VMEM_LIMIT),
        name="nbr_attention",
    )(nq, nk, nvt, ng, rpb_a, rpb_b)


def _gla_kernel(q_ref, k_ref, v_ref, g_ref, z_ref, wa_ref, ba_ref, nw_ref, o_ref,
                st_scr, o_scr, qd_scr, kd_scr, ke_scr, vt_scr, dec_scr):
    C = GLA_CHUNK
    TR = GLA_TILE
    n_sub = TR // C
    kl = lax.broadcasted_iota(jnp.int32, (1, GLA_KP), 1) % LANE
    k_head = jnp.where(kl < GLA_HEADS * (GLA_DK // 2), kl // (GLA_DK // 2), GLA_HEADS)
    v_head = lax.broadcasted_iota(jnp.int32, (1, D_GLA_V), 1) // GLA_DV
    st_mask = (lax.broadcasted_iota(jnp.int32, (D_GLA_V, 1), 0) // GLA_DV) == k_head
    ti4 = lax.broadcasted_iota(jnp.int32, (C, GLA_HEADS * C), 0)
    si4 = lax.broadcasted_iota(jnp.int32, (C, GLA_HEADS * C), 1) % C
    tr = lax.broadcasted_iota(jnp.int32, (TR, TR), 0)
    sr = lax.broadcasted_iota(jnp.int32, (TR, TR), 1)
    same_chunk = (tr // C) == (sr // C)
    tri = [jnp.where(same_chunk & (sr <= tr), 1.0, 0.0).astype(BF16),
           jnp.where(same_chunk & (sr >= tr), 1.0, 0.0).astype(BF16)]

    def tile_start(i):
        return i * TR if isinstance(i, int) else pl.multiple_of(i * TR, TR)

    def tile_rows(i):
        return pl.ds(tile_start(i), TR)

    def decay_sums(i):
        z = z_ref[0, tile_rows(i), :].astype(BF16)
        g = _dot(z, wa_ref[...]) + ba_ref[...]
        la = (jnp.minimum(g, 0.0) - jnp.log(1.0 + jnp.exp(-jnp.abs(g)))) * (1.0 / GLA_TAU)
        sums = []
        for d in range(2):
            la_hi, la_lo = _split_bf16(la[:, d * GLA_KP:(d + 1) * GLA_KP])
            sums.append(_dot(tri[d], la_hi) + _dot(tri[d], la_lo))
        return sums

    def decayed_operands(i, sums):
        rows = tile_rows(i)
        q = q_ref[0, rows, :]
        k = k_ref[0, rows, :]
        for d in range(2):
            bcum = sums[d]
            ends = [bcum[(c + 1) * C - 1:(c + 1) * C] if d == 0 else bcum[c * C:c * C + 1]
                    for c in range(n_sub)]
            b_end = jnp.concatenate([jnp.broadcast_to(e, (C, GLA_KP)) for e in ends], axis=0)
            qd_scr[d, rows, :] = (q * jnp.exp(bcum)).astype(BF16)
            kd_scr[d, rows, :] = (k * jnp.exp(-bcum)).astype(BF16)
            ke_scr[d, rows, :] = (k * jnp.exp(b_end - bcum)).astype(BF16)
            for c in range(n_sub):
                dec_scr[d, pl.ds(i * n_sub + c, 1), :] = jnp.exp(ends[c])

    def intra_chunk(i):
        r0 = tile_start(i)
        v_bd = []
        for c in range(n_sub):
            v_c = v_ref[0, pl.ds(r0 + c * C, C), :]
            v_bd.append(jnp.concatenate(
                [jnp.where(v_head == h, v_c, jnp.zeros_like(v_c)) for h in range(GLA_HEADS)], axis=0))
            vt_scr[i * n_sub + c] = v_c.astype(F32).T.astype(BF16)
        scores = []
        for d in range(2):
            for c in range(n_sub):
                crow = pl.ds(r0 + c * C, C)
                k_c = kd_scr[d, crow, :]
                k_bd = jnp.concatenate(
                    [jnp.where(k_head == h, k_c, jnp.zeros_like(k_c)) for h in range(GLA_HEADS)], axis=0)
                scores.append(_dot_nt(qd_scr[d, crow, :], k_bd))
        for d in range(2):
            causal = (si4 <= ti4) if d == 0 else (si4 >= ti4)
            for c in range(n_sub):
                a = jnp.where(causal, scores[d * n_sub + c], 0.0).astype(BF16)
                o_scr[d, pl.ds(r0 + c * C, C), :] = _dot(a, v_bd[c])

    def tile_body(i, carry):
        sums = decay_sums(i + 1)
        intra_chunk(i)
        decayed_operands(i + 1, sums)
        return carry

    n_tiles = T_ALL // TR
    decayed_operands(0, decay_sums(0))
    lax.fori_loop(0, n_tiles - 1, tile_body, 0, unroll=2)
    intra_chunk(n_tiles - 1)

    st_scr[...] = jnp.zeros_like(st_scr)
    n_all = N_CHUNKS_C + N_CHUNKS_X

    def step(i, carry):
        in_ctx = i < N_CHUNKS_C
        idx_f = jnp.where(in_ctx, N_CHUNKS_X + i, i - N_CHUNKS_C)
        idx_b = jnp.where(in_ctx, N_CHUNKS_X + N_CHUNKS_C - 1 - i, n_all - 1 - i)
        half = D_GLA_V // 2
        dirs = ((0, idx_f), (1, idx_b))
        offs = [pl.ds(pl.multiple_of(idx * C, C), C) for _, idx in dirs]
        upd = []
        for (d, idx), off in zip(dirs, offs):
            ke = ke_scr[d, off, :]
            upd.append(jnp.concatenate(
                [_dot(vt_scr[idx, :half, :], ke), _dot(vt_scr[idx, half:, :], ke)], axis=0))
        for (d, idx), off in zip(dirs, offs):
            st = st_scr[d]
            o_scr[d, off, :] += _dot_nt(qd_scr[d, off, :], st.astype(BF16))
            st_scr[d] = st * dec_scr[d, pl.ds(idx, 1), :] + jnp.where(st_mask, upd[d], 0.0)
        return carry

    lax.fori_loop(0, n_all, step, 0, unroll=4)

    head_ones = jnp.where(
        (lax.broadcasted_iota(jnp.int32, (D_GLA_V, 1), 0) // GLA_DV) == v_head, 1.0, 0.0).astype(BF16)
    tile = GLA_TILE

    def fin(i, carry):
        r0 = pl.multiple_of(i * tile, tile)
        o = o_scr[0, pl.ds(r0, tile), :] + o_scr[1, pl.ds(r0, tile), :]
        sq_hi, sq_lo = _split_bf16(o * o)
        ms = (_dot(sq_hi, head_ones) + _dot(sq_lo, head_ones)) * (1.0 / GLA_DV)
        y = o * lax.rsqrt(ms + EPS) * nw_ref[...]
        gate = g_ref[0, pl.ds(r0, tile), :].astype(F32)
        o_ref[0, pl.ds(r0, tile), :] = (y * _silu(gate)).astype(BF16)
        return carry

    lax.fori_loop(0, T_ALL // tile, fin, 0, unroll=3)


def _gla(gq, gk, gv, gg, gz, wa, ba, gnw):
    nb = gq.shape[0]
    seq = lambda w: pl.BlockSpec((1, T_ALL, w), lambda b: (b, 0, 0))
    full = lambda a: pl.BlockSpec(a.shape, lambda b: (0,) * a.ndim)
    return pl.pallas_call(
        _gla_kernel,
        grid=(nb,),
        in_specs=[seq(GLA_KP), seq(GLA_KP), seq(D_GLA_V), seq(D_GLA_V), seq(Z_PAD),
                  full(wa), full(ba), full(gnw)],
        out_specs=seq(D_GLA_V),
        out_shape=jax.ShapeDtypeStruct((nb, T_ALL, D_GLA_V), BF16),
        scratch_shapes=[pltpu.VMEM((2, D_GLA_V, GLA_KP), F32),
                        pltpu.VMEM((2, T_ALL, D_GLA_V), F32),
                        pltpu.VMEM((2, T_ALL, GLA_KP), BF16),
                        pltpu.VMEM((2, T_ALL, GLA_KP), BF16),
                        pltpu.VMEM((2, T_ALL, GLA_KP), BF16),
                        pltpu.VMEM((T_ALL // GLA_CHUNK, D_GLA_V, GLA_CHUNK), BF16),
                        pltpu.VMEM((2, T_ALL // GLA_CHUNK, GLA_KP), F32)],

        compiler_params=pltpu.CompilerParams(
            dimension_semantics=("arbitrary",), vmem_limit_bytes=VMEM_LIMIT),
        name="gla_mix",
    )(gq, gk, gv, gg, gz, wa, ba, gnw)


def _outproj_kernel(*refs, n_h, tile, final):
    h_refs, refs = refs[:n_h], refs[n_h:]
    mf_ref, mn_ref, mg_ref, w_ref, mod_ref, nf_ref, o_ref, w_scr = refs[:8]
    b = pl.program_id(0)
    j = pl.program_id(1)

    @pl.when((b == 0) & (j == 0))
    def _():
        for r in range(0, D_MODEL, 256):
            w_scr[r:r + 256, :] = w_ref[0, r:r + 256, :].astype(BF16)

    acc = (_dot(mf_ref[0], w_scr[:D_FOUR, :])
           + _dot(mn_ref[0], w_scr[D_FOUR:D_FOUR + D_NA, :])
           + _dot(mg_ref[0], w_scr[D_FOUR + D_NA:, :]))
    gate = _mod_rows(mod_ref, b, j * tile, tile, 2 * D_MODEL)
    x = _stream_tile(h_refs, j, refs[8] if n_h > 1 else None) + gate * acc
    if final:
        ms = jnp.mean(x * x, axis=-1, keepdims=True)
        x = x * lax.rsqrt(ms + EPS) * nf_ref[...]
    o_ref[0] = x


def _outproj(h_in, mf, mn, mg, w_out, mod, l, nf, final):
    nb = h_in[0].shape[0]
    tile = 512 if final else TOK_TILE
    rows = SEQ if final else T_ALL
    if final:
        h_in = h_in[:1]
    split = len(h_in) > 1
    tok = lambda width: pl.BlockSpec((1, tile, width), lambda b, j: (b, j, 0))
    layer = lambda a: pl.BlockSpec((1,) + a.shape[1:], lambda b, j: (l,) + (0,) * (a.ndim - 1))
    return pl.pallas_call(
        functools.partial(_outproj_kernel, n_h=len(h_in), tile=tile, final=final),
        grid=(nb, rows // tile),
        in_specs=(_split_stream_specs() if split else _stream_specs(tile))
        + [tok(D_FOUR), tok(D_NA), tok(D_GLA_V),
           layer(w_out), layer(mod), pl.BlockSpec((1, D_MODEL), lambda b, j: (0, 0))],
        out_specs=tok(D_MODEL),
        out_shape=jax.ShapeDtypeStruct((nb, rows, D_MODEL), F32),
        scratch_shapes=[pltpu.VMEM((D_MODEL, D_MODEL), BF16)]
        + ([pltpu.VMEM((TOK_TILE, D_MODEL), F32)] if split else []),
        compiler_params=pltpu.CompilerParams(
            dimension_semantics=("arbitrary", "arbitrary"), vmem_limit_bytes=VMEM_LIMIT),
        name="outproj_final" if final else "outproj",
    )(*h_in, mf, mn, mg, w_out, mod, nf)


def _decay_weights(w_f, b_f, w_b, b_b):
    n_freq = GLA_DK // 4
    lane_pad = lambda a: jnp.zeros(a.shape[:-1] + (LANE - GLA_HEADS * 2 * n_freq,), a.dtype)
    lanes = lambda a: jnp.concatenate(
        [blk for offs in _qk_pieces() for blk in [a[..., o:o + n_freq] for o in offs] + [lane_pad(a)]],
        axis=-1)
    zero = jnp.zeros_like(lanes(w_f))
    top = jnp.concatenate([lanes(w_f), zero], axis=-1)
    bot = jnp.concatenate([zero, lanes(w_b)], axis=-1)
    pad = jnp.zeros((w_f.shape[0], Z_PAD - 2 * GLA_RANK, 2 * GLA_KP), w_f.dtype)
    wa = jnp.concatenate([top, bot, pad], axis=1).astype(BF16)
    ba = jnp.concatenate([lanes(b_f), lanes(b_b)], axis=-1)[:, None, :]
    return wa, ba


def kernel(x, c, ctx, c_ctx, w_ada, b_ada, norm_w, w_in, w_four, rpb, w_alpha_fwd, b_alpha_fwd,
           w_alpha_bwd, b_alpha_bwd, gla_norm_w, w_out, norm_f):
    nb = x.shape[0]
    depth = w_in.shape[0]
    assert x.shape == (nb, SEQ, D_MODEL) and ctx.shape == (nb, CTX_LEN, D_MODEL) and nb <= CTX_MOD_ROW

    w_in_t = jnp.swapaxes(w_in, 1, 2)
    w_four_b = w_four.astype(BF16)
    norm_w3 = norm_w.reshape(depth, 1, D_MODEL)
    wa, ba = _decay_weights(w_alpha_fwd, b_alpha_fwd, w_alpha_bwd, b_alpha_bwd)
    gnw = jnp.tile(gla_norm_w, (1, GLA_HEADS)).reshape(depth, 1, D_GLA_V)
    n_off = 2 * NA_KW - 1
    rpb_rev = rpb[..., ::-1]
    rpb_a = jnp.pad(rpb_rev, ((0, 0), (0, 0), (0, 1), (0, LANE - n_off)))
    rpb_b = jnp.pad(rpb_rev, ((0, 0), (0, 0), (0, 1), (GRID_W, LANE - GRID_W - n_off)))

    cos_t, sin_t = (jnp.asarray(t) for t in _rope_tables())
    c64, s64 = _dft_tables(FOUR_DIM)
    cl, sl = _dft_tables(SEQ)
    ccx, scx = _dft_tables(CTX_LEN)
    tabs = [jnp.asarray(t, dtype=F32).astype(BF16) for t in
            (_block_diag(c64, FOUR_GROUPS), _block_diag(s64, FOUR_GROUPS), cl, sl, ccx, scx)]

    cc = jnp.zeros((MOD_ROWS, D_MODEL), F32).at[:nb].set(c).at[CTX_MOD_ROW].set(c_ctx)
    mod = _modulation(cc, w_ada, b_ada)

    h_in = (x, x, ctx)
    for l in range(depth):
        final = l == depth - 1
        (p_four, nq, nk, nvt, ng, gq, gk, gv, gg, gz) = _inproj(
            h_in, mod, l, norm_w3, w_in_t, cos_t, sin_t)
        mf = _fourier(p_four, tabs, w_four_b[l])
        mn = _na(nq, nk, nvt, ng, rpb_a[l], rpb_b[l])
        mg = _gla(gq, gk, gv, gg, gz, wa[l], ba[l], gnw[l])
        h = _outproj(h_in, mf, mn, mg, w_out, mod, l, norm_f.reshape(1, D_MODEL), final)
        h_in = (h,)
    return h
```

```python
import functools

import numpy as np
import jax
import jax.numpy as jnp
from jax import lax
from jax.experimental import pallas as pl
from jax.experimental.pallas import tpu as pltpu

F32 = jnp.float32
BF16 = jnp.bfloat16

D_MODEL = 1024
SEQ = 2048
CTX_LEN = 256
T_ALL = SEQ + CTX_LEN
GRID_W = 64
GRID_H = SEQ // GRID_W
EPS = 1e-6

FOUR_GROUPS, FOUR_DIM = 4, 64
D_FOUR = FOUR_GROUPS * FOUR_DIM
NA_HEADS, NA_HEAD_DIM = 6, 64
D_NA = NA_HEADS * NA_HEAD_DIM
NA_KH, NA_KW = 8, 16
GLA_HEADS, GLA_DK, GLA_DV = 4, 48, 96
D_GLA_K = GLA_HEADS * GLA_DK
D_GLA_V = GLA_HEADS * GLA_DV
GLA_RANK = 16
GLA_TAU = 16.0
GLA_CHUNK = 64
ROPE_BASE = 10000.0
LOG2E = 1.4426950408889634

IN_SIZES = (D_FOUR, D_FOUR, D_NA, D_NA, D_NA, D_NA, D_GLA_K, D_GLA_K, D_GLA_V, D_GLA_V,
            GLA_RANK, GLA_RANK)
D_IN = sum(IN_SIZES)

LANE = 128
BF16_ROWS = 16
CTX_MOD_ROW = 8
MOD_ROWS = 16
GLA_KP = 2 * LANE
Z_PAD = LANE
N_CHUNKS_X = SEQ // GLA_CHUNK
N_CHUNKS_C = CTX_LEN // GLA_CHUNK
GLA_TILE = 256

NA_QROWS = 4
NA_KROWS = NA_QROWS + NA_KH
NA_QB = NA_QROWS * GRID_W
NA_KB = NA_KROWS * GRID_W
NA_NBLK = GRID_H // NA_QROWS
NA_VT_PAIR = LANE + BF16_ROWS
NA_VT_ROWS = NA_HEADS // 2 * NA_VT_PAIR

TOK_TILE = 768
N_XTILES = SEQ // TOK_TILE
X_TAIL = SEQ - N_XTILES * TOK_TILE
VMEM_LIMIT = 56 * 1024 * 1024

COL_FOUR = 0
COL_NA = COL_FOUR + 2 * D_FOUR
COL_GQ = COL_NA + 3 * D_NA
COL_GK = COL_GQ + GLA_KP
COL_GV = COL_GK + GLA_KP
COL_GG = COL_GV + D_GLA_V
COL_Z = COL_GG + D_GLA_V
N_COLS = COL_Z + Z_PAD


def _dot(a, b):
    return jnp.dot(a, b, preferred_element_type=F32)


def _dot_nt(a, b):
    return lax.dot_general(a, b, (((1,), (1,)), ((), ())), preferred_element_type=F32)


def _silu(x):
    return x / (1.0 + jnp.exp(-x))


def _reduce_rows(x, op):
    r, n = x.shape
    part = op(x.reshape(r // 8, 8, n), axis=0)
    return op(part, axis=0, keepdims=True)


def _split_bf16(x):
    hi = x.astype(BF16)
    lo = (x - hi.astype(F32)).astype(BF16)
    return hi, lo


def _qk_pieces():
    n_freq = GLA_DK // 4
    return [[h * GLA_DK + half * 2 * n_freq + part * n_freq
             for h in range(GLA_HEADS) for half in range(2)] for part in range(2)]


def _rope_tables():
    n_freq = GLA_DK // 4
    inv = ROPE_BASE ** (-np.arange(n_freq, dtype=np.float64) / n_freq)
    pos = np.arange(SEQ)
    prow, pcol = pos // GRID_W, pos % GRID_W
    cos = np.ones((T_ALL, LANE), np.float64)
    sin = np.zeros((T_ALL, LANE), np.float64)
    for h in range(GLA_HEADS):
        for half, p in enumerate((prow, pcol)):
            ang = p[:, None] * inv[None, :]
            lo = h * 2 * n_freq + half * n_freq
            cos[:SEQ, lo:lo + n_freq] = np.cos(ang)
            sin[:SEQ, lo:lo + n_freq] = np.sin(ang)
    return cos.astype(np.float32), sin.astype(np.float32)


def _dft_tables(n):
    idx = (np.arange(n)[:, None] * np.arange(n)[None, :]) % n
    ang = 2.0 * np.pi * idx / n
    return np.cos(ang) / np.sqrt(n), np.sin(ang) / np.sqrt(n)


def _block_diag(m, reps):
    n = m.shape[0]
    out = np.zeros((n * reps, n * reps), m.dtype)
    for r in range(reps):
        out[r * n:(r + 1) * n, r * n:(r + 1) * n] = m
    return out


def _na_row_offsets():
    out = []
    for r0 in (0, NA_QROWS, GRID_H - NA_QROWS):
        k0 = int(np.clip(r0 - NA_KH // 2, 0, GRID_H - NA_KROWS))
        per_type = []
        for a in range(NA_QROWS):
            r = r0 + a
            rs = int(np.clip(r - NA_KH // 2, 0, GRID_H - NA_KH))
            per_type.append([(k0 + j) - r + NA_KH - 1 if rs <= k0 + j < rs + NA_KH else None
                             for j in range(NA_KROWS)])
        out.append(per_type)
    return out


def _mod_kernel(c_ref, w_ref, b_ref, o_ref):
    c = c_ref[...]
    s_hi, s_lo = _split_bf16(_silu(c))
    w_hi, w_lo = _split_bf16(w_ref[0])
    acc = _dot(s_hi, w_hi) + _dot(s_lo, w_hi) + _dot(s_hi, w_lo)
    o_ref[0] = acc + b_ref[0]


def _modulation(cc, w_ada, b_ada):
    depth = w_ada.shape[0]
    n_mod = w_ada.shape[2]
    tile = D_MODEL
    return pl.pallas_call(
        _mod_kernel,
        grid=(depth, n_mod // tile),
        in_specs=[
            pl.BlockSpec((MOD_ROWS, D_MODEL), lambda l, j: (0, 0)),
            pl.BlockSpec((1, D_MODEL, tile), lambda l, j: (l, 0, j)),
            pl.BlockSpec((1, 1, tile), lambda l, j: (l, 0, j)),
        ],
        out_specs=pl.BlockSpec((1, MOD_ROWS, tile), lambda l, j: (l, 0, j)),
        out_shape=jax.ShapeDtypeStruct((depth, MOD_ROWS, n_mod), F32),
        compiler_params=pltpu.CompilerParams(
            dimension_semantics=("arbitrary", "arbitrary"), vmem_limit_bytes=VMEM_LIMIT),
        name="adaln_mod",
    )(cc, w_ada, b_ada.reshape(depth, 1, n_mod))


def _mod_rows(mod_ref, b, row0, n_rows, col0):
    rows = row0 + lax.broadcasted_iota(jnp.int32, (n_rows, 1), 0)
    m_x = mod_ref[0, pl.ds(b, 1), col0:col0 + D_MODEL]
    m_c = mod_ref[0, CTX_MOD_ROW:CTX_MOD_ROW + 1, col0:col0 + D_MODEL]
    return jnp.where(rows >= SEQ, m_c, m_x)


def _fill_weight(wt_ref, w_scr, dst0, src0, width, scale=None):
    k = w_scr.shape[0]
    for c in range(0, width, LANE):
        n = min(LANE, width - c)
        blk = wt_ref[0, src0 + c:src0 + c + n, :]
        if scale is not None:
            blk = blk * scale
        if n < LANE:
            blk = jnp.concatenate([blk, jnp.zeros((LANE - n, k), F32)], axis=0)
        w_scr[:, dst0 + c:dst0 + c + LANE] = blk.T.astype(BF16)


def _stream_specs(tile):
    return [pl.BlockSpec((1, tile, D_MODEL), lambda b, j: (b, j, 0))]


def _split_stream_specs():
    assert X_TAIL + CTX_LEN == TOK_TILE and SEQ % X_TAIL == 0
    return [pl.BlockSpec((1, TOK_TILE, D_MODEL), lambda b, j: (b, jnp.minimum(j, N_XTILES - 1), 0)),
            pl.BlockSpec((1, X_TAIL, D_MODEL), lambda b, j: (b, SEQ // X_TAIL - 1, 0)),
            pl.BlockSpec((1, CTX_LEN, D_MODEL), lambda b, j: (b, 0, 0))]


def _stream_tile(h_refs, j, tile_scr):
    if len(h_refs) == 1:
        return h_refs[0][0]
    x_ref, tail_ref, ctx_ref = h_refs

    @pl.when(j < N_XTILES)
    def _():
        tile_scr[...] = x_ref[0]

    @pl.when(j == N_XTILES)
    def _():
        tile_scr[:X_TAIL] = tail_ref[0]
        tile_scr[X_TAIL:] = ctx_ref[0]

    return tile_scr[...]


def _inproj_kernel(*refs, n_h):
    h_refs, refs = refs[:n_h], refs[n_h:]
    (mod_ref, nw_ref, w_ref, cos_ref, sin_ref,
     four_ref, nq_ref, nk_ref, nvt_ref, ng_ref,
     gq_ref, gk_ref, gv_ref, gg_ref, gz_ref, hx_scr, w_scr, wvt_scr) = refs[:18]
    b = pl.program_id(0)
    j = pl.program_id(1)

    @pl.when((b == 0) & (j == 0))
    def _():
        src = np.concatenate([[0], np.cumsum(IN_SIZES)])
        _fill_weight(w_ref, w_scr, COL_FOUR, src[0], 2 * D_FOUR)
        _fill_weight(w_ref, w_scr, COL_NA, src[2], D_NA, NA_HEAD_DIM ** -0.5 * LOG2E)
        _fill_weight(w_ref, w_scr, COL_NA + D_NA, src[3], D_NA)
        _fill_weight(w_ref, w_scr, COL_NA + 2 * D_NA, src[5], D_NA)
        n_freq = GLA_DK // 4
        pad = jnp.zeros((LANE - GLA_HEADS * 2 * n_freq, D_MODEL), F32)
        for dst0, src0 in ((COL_GQ, src[6]), (COL_GK, src[7])):
            for part, offs in enumerate(_qk_pieces()):
                tile = jnp.concatenate(
                    [w_ref[0, src0 + o:src0 + o + n_freq, :] for o in offs] + [pad], axis=0)
                w_scr[:, dst0 + part * LANE:dst0 + (part + 1) * LANE] = tile.T.astype(BF16)
        _fill_weight(w_ref, w_scr, COL_GV, src[8], D_GLA_V)
        _fill_weight(w_ref, w_scr, COL_GG, src[9], D_GLA_V)
        _fill_weight(w_ref, w_scr, COL_Z, src[10], 2 * GLA_RANK)
        for p in range(NA_HEADS // 2):
            blk = w_ref[0, src[4] + p * LANE:src[4] + (p + 1) * LANE, :]
            wvt_scr[p * NA_VT_PAIR:p * NA_VT_PAIR + LANE, :] = blk.astype(BF16)
            wvt_scr[p * NA_VT_PAIR + LANE:(p + 1) * NA_VT_PAIR, :] = jnp.zeros(
                (NA_VT_PAIR - LANE, D_MODEL), BF16)

    x = _stream_tile(h_refs, j, refs[18] if n_h > 1 else None)
    ms = jnp.mean(x * x, axis=-1, keepdims=True)
    y = x * lax.rsqrt(ms + EPS) * nw_ref[0]
    shift = _mod_rows(mod_ref, b, j * TOK_TILE, TOK_TILE, 0)
    scale = _mod_rows(mod_ref, b, j * TOK_TILE, TOK_TILE, D_MODEL)
    hx_scr[...] = (y * (1.0 + scale) + shift).astype(BF16)

    def proj(c0, width):
        return _dot(hx_scr[...], w_scr[:, c0:c0 + width])

    four_ref[0] = proj(COL_FOUR, 2 * D_FOUR).astype(BF16)
    nq_ref[0] = proj(COL_NA, D_NA).astype(BF16)
    nk_ref[0] = proj(COL_NA + D_NA, D_NA).astype(BF16)
    ones_rows = lax.broadcasted_iota(jnp.int32, (NA_VT_ROWS, 1), 0) % NA_VT_PAIR >= LANE
    nvt_ref[0] = (_dot_nt(wvt_scr[...], hx_scr[...]) + jnp.where(ones_rows, 1.0, 0.0)).astype(BF16)
    ng_ref[0] = proj(COL_NA + 2 * D_NA, D_NA).astype(BF16)
    gv_ref[0] = proj(COL_GV, D_GLA_V).astype(BF16)
    gg_ref[0] = proj(COL_GG, D_GLA_V).astype(BF16)
    gz_ref[0] = proj(COL_Z, Z_PAD)

    cos = cos_ref[...]
    sin = sin_ref[...]

    def rope(c0, out_ref, scale_out):
        u = proj(c0, GLA_KP)
        u1, u2 = u[:, :LANE], u[:, LANE:]
        out_ref[0, :, :LANE] = (u1 * cos - u2 * sin) * scale_out
        out_ref[0, :, LANE:] = (u1 * sin + u2 * cos) * scale_out

    rope(COL_GQ, gq_ref, GLA_DK ** -0.5)
    rope(COL_GK, gk_ref, 1.0)


def _inproj(h_in, mod, l, norm_w, w_in, cos_t, sin_t):
    nb = h_in[0].shape[0]
    nt = T_ALL // TOK_TILE
    split = len(h_in) > 1
    tok = lambda width: pl.BlockSpec((1, TOK_TILE, width), lambda b, j: (b, j, 0))
    layer = lambda a: pl.BlockSpec((1,) + a.shape[1:], lambda b, j: (l,) + (0,) * (a.ndim - 1))
    out_widths = [(2 * D_FOUR, BF16), (D_NA, BF16), (D_NA, BF16), (None, BF16), (D_NA, BF16),
                  (GLA_KP, F32), (GLA_KP, F32), (D_GLA_V, BF16), (D_GLA_V, BF16), (Z_PAD, F32)]
    vt_spec = pl.BlockSpec((1, NA_VT_ROWS, TOK_TILE), lambda b, j: (b, 0, j))
    vt_shape = jax.ShapeDtypeStruct((nb, NA_VT_ROWS, T_ALL), BF16)
    return pl.pallas_call(
        functools.partial(_inproj_kernel, n_h=len(h_in)),
        grid=(nb, nt),
        in_specs=(_split_stream_specs() if split else _stream_specs(TOK_TILE)) + [
            layer(mod),
            layer(norm_w),
            layer(w_in),
            pl.BlockSpec((TOK_TILE, LANE), lambda b, j: (j, 0)),
            pl.BlockSpec((TOK_TILE, LANE), lambda b, j: (j, 0)),
        ],
        out_specs=[vt_spec if w is None else tok(w) for w, _ in out_widths],
        out_shape=[vt_shape if w is None else jax.ShapeDtypeStruct((nb, T_ALL, w), dt)
                   for w, dt in out_widths],
        scratch_shapes=[pltpu.VMEM((TOK_TILE, D_MODEL), BF16), pltpu.VMEM((D_MODEL, N_COLS), BF16),
                        pltpu.VMEM((NA_VT_ROWS, D_MODEL), BF16)]
        + ([pltpu.VMEM((TOK_TILE, D_MODEL), F32)] if split else []),
        compiler_params=pltpu.CompilerParams(
            dimension_semantics=("arbitrary", "arbitrary"), vmem_limit_bytes=VMEM_LIMIT),
        name="inproj",
    )(*h_in, mod, norm_w, w_in, cos_t, sin_t)


def _fourier_kernel(p_ref, cbd_ref, sbd_ref, cl_ref, sl_ref, cc_ref, sc_ref, wf_ref, o_ref,
                    uc_scr, us_scr):
    u = p_ref[0, :, :D_FOUR]
    uc_scr[...] = _dot(u, cbd_ref[...]).astype(BF16)
    us_scr[...] = _dot(u, sbd_ref[...]).astype(BF16)

    def finish(f, r0, n):
        four = _dot(f.astype(BF16), wf_ref[...])
        gate = p_ref[0, pl.ds(r0, n), D_FOUR:].astype(F32)
        o_ref[0, pl.ds(r0, n), :] = (four * _silu(gate)).astype(BF16)

    f_x = _dot(cl_ref[...], uc_scr[:SEQ, :]) - _dot(sl_ref[...], us_scr[:SEQ, :])
    finish(f_x, 0, SEQ)
    f_c = _dot(cc_ref[...], uc_scr[SEQ:, :]) - _dot(sc_ref[...], us_scr[SEQ:, :])
    finish(f_c, SEQ, CTX_LEN)


def _fourier(p_four, tabs, wf):
    nb = p_four.shape[0]
    full = lambda a: pl.BlockSpec(a.shape, lambda b: (0,) * a.ndim)
    return pl.pallas_call(
        _fourier_kernel,
        grid=(nb,),
        in_specs=[pl.BlockSpec((1, T_ALL, 2 * D_FOUR), lambda b: (b, 0, 0))]
        + [full(t) for t in tabs] + [full(wf)],
        out_specs=pl.BlockSpec((1, T_ALL, D_FOUR), lambda b: (b, 0, 0)),
        out_shape=jax.ShapeDtypeStruct((nb, T_ALL, D_FOUR), BF16),
        scratch_shapes=[pltpu.VMEM((T_ALL, D_FOUR), BF16), pltpu.VMEM((T_ALL, D_FOUR), BF16)],
        compiler_params=pltpu.CompilerParams(
            dimension_semantics=("arbitrary",), vmem_limit_bytes=VMEM_LIMIT),
        name="fourier_mix",
    )(p_four, *tabs, wf)


def _na_build_bias(ra_ref, rb_ref, tbl_ref):
    w = GRID_W
    ck = lax.broadcasted_iota(jnp.int32, (w, 2 * w), 0)
    ln = lax.broadcasted_iota(jnp.int32, (w, 2 * w), 1)
    cq = ln % w
    cs = jnp.clip(cq - NA_KW // 2, 0, w - NA_KW)
    col_ok = (ck >= cs) & (ck < cs + NA_KW)
    left = ln < w
    neg = jnp.full((w, 2 * w), -jnp.inf, F32)
    offs = _na_row_offsets()
    cache = {}
    for typ in range(3):
        for ap in range(NA_QROWS // 2):
            for jr in range(NA_KROWS):
                d0 = offs[typ][2 * ap][jr]
                d1 = offs[typ][2 * ap + 1][jr]
                for h in range(NA_HEADS):
                    key = (h, d0, d1)
                    if key not in cache:
                        if d0 is None and d1 is None:
                            cache[key] = neg
                        else:
                            ok = col_ok
                            if d0 is None:
                                row = rb_ref[h, d1:d1 + 1, :]
                                ok = ok & jnp.logical_not(left)
                            elif d1 is None:
                                row = ra_ref[h, d0:d0 + 1, :]
                                ok = ok & left
                            else:
                                row = ra_ref[h, d0:d0 + 1, :] + rb_ref[h, d1:d1 + 1, :]
                            skew = pltpu.roll(jnp.broadcast_to(row * LOG2E, (w, 2 * w)),
                                              2 * w - (NA_KW - 1), 1, stride=1, stride_axis=0)
                            cache[key] = jnp.where(ok, skew, neg)
                    tbl_ref[h * 3 + typ, jr * w:(jr + 1) * w, ap * 2 * w:(ap + 1) * 2 * w] = cache[key]


def _na_kernel(q_ref, k_ref, vt_ref, g_ref, ra_ref, rb_ref, o_ref, bias_ref, s_scr):
    lane = lax.broadcasted_iota(jnp.int32, (1, LANE), 1)
    first_head_lanes = lane < NA_HEAD_DIM
    first_head_rows = lax.broadcasted_iota(jnp.int32, (LANE, 1), 0) < NA_HEAD_DIM
    heads = range(NA_HEADS)
    cols = [slice(h // 2 * LANE, (h // 2 + 1) * LANE) for h in heads]

    @pl.when(pl.program_id(0) == 0)
    def _():
        _na_build_bias(ra_ref, rb_ref, bias_ref)

    def block_geometry(j):
        if isinstance(j, int):
            k0 = min(max(j * NA_QROWS - NA_KH // 2, 0), GRID_H - NA_KROWS) * GRID_W
            return pl.ds(j * NA_QB, NA_QB), k0, (0 if j == 0 else 2 if j == NA_NBLK - 1 else 1)
        k0 = jnp.clip(j * NA_QROWS - NA_KH // 2, 0, GRID_H - NA_KROWS) * GRID_W
        typ = jnp.where(j == 0, 0, jnp.where(j == NA_NBLK - 1, 2, 1))
        return pl.ds(pl.multiple_of(j * NA_QB, NA_QB), NA_QB), pl.multiple_of(k0, NA_QB), typ

    def scores(j, buf, local):
        qrows, k0, typ = block_geometry(j)
        for h in heads:
            qp = q_ref[0, qrows, cols[h]]
            keep = first_head_lanes if h % 2 == 0 else jnp.logical_not(first_head_lanes)
            qm = jnp.where(keep, qp, jnp.zeros_like(qp))
            s_scr[buf, h, :CTX_LEN, :] = _dot_nt(k_ref[0, SEQ:, cols[h]], qm)
            if local:
                s_scr[buf, h, CTX_LEN:, :] = (
                    _dot_nt(k_ref[0, pl.ds(k0, NA_KB), cols[h]], qm) + bias_ref[h * 3 + typ])

    def finish(j, buf, local):
        qrows, k0, _ = block_geometry(j)
        n_keys = CTX_LEN + NA_KB if local else CTX_LEN
        e = []
        for h in heads:
            s = s_scr[buf, h, :n_keys, :]
            e.append(jnp.exp2((s - _reduce_rows(s, jnp.max)).astype(BF16)))
        for p in range(NA_HEADS // 2):
            vrows = slice(p * NA_VT_PAIR, (p + 1) * NA_VT_PAIR)
            acc = None
            for h in (2 * p, 2 * p + 1):
                o = _dot(vt_ref[0, vrows, SEQ:], e[h][:CTX_LEN])
                if local:
                    o = o + _dot(vt_ref[0, vrows, pl.ds(k0, NA_KB)], e[h][CTX_LEN:])
                o = o[:LANE] / o[LANE:LANE + 1]
                acc = o if h % 2 == 0 else jnp.where(first_head_rows, acc, o)
            gate = g_ref[0, qrows, cols[2 * p]].astype(F32)
            o_ref[0, qrows, cols[2 * p]] = (acc.T * _silu(gate)).astype(BF16)

    def pair_body(i, carry):
        scores(2 * i + 1, 1, True)
        finish(2 * i, 0, True)
        scores(2 * i + 2, 0, True)
        finish(2 * i + 1, 1, True)
        return carry

    scores(0, 0, True)
    lax.fori_loop(0, NA_NBLK // 2 - 1, pair_body, 0)
    scores(NA_NBLK - 1, 1, True)
    finish(NA_NBLK - 2, 0, True)
    scores(NA_NBLK, 0, False)
    finish(NA_NBLK - 1, 1, True)
    finish(NA_NBLK, 0, False)


def _na(nq, nk, nvt, ng, rpb_a, rpb_b):
    nb = nq.shape[0]
    seq = pl.BlockSpec((1, T_ALL, D_NA), lambda b: (b, 0, 0))
    seq_t = pl.BlockSpec((1, NA_VT_ROWS, T_ALL), lambda b: (b, 0, 0))
    tab = pl.BlockSpec(rpb_a.shape, lambda b: (0, 0, 0))
    return pl.pallas_call(
        _na_kernel,
        grid=(nb,),
        in_specs=[seq, seq, seq_t, seq, tab, tab],
        out_specs=seq,
        out_shape=jax.ShapeDtypeStruct((nb, T_ALL, D_NA), BF16),
        scratch_shapes=[pltpu.VMEM((NA_HEADS * 3, NA_KB, NA_QB), F32),
                        pltpu.VMEM((2, NA_HEADS, CTX_LEN + NA_KB, NA_QB), F32)],
        compiler_params=pltpu.CompilerParams(
            dimension_semantics=("arbitrary",), vmem_limit_bytes=VMEM_LIMIT),
        name="nbr_attention",
    )(nq, nk, nvt, ng, rpb_a, rpb_b)


def _gla_kernel(q_ref, k_ref, v_ref, g_ref, z_ref, wa_ref, ba_ref, nw_ref, o_ref,
                st_scr, o_scr, qd_scr, kd_scr, ke_scr, vt_scr, dec_scr):
    C = GLA_CHUNK
    TR = GLA_TILE
    n_sub = TR // C
    kl = lax.broadcasted_iota(jnp.int32, (1, GLA_KP), 1) % LANE
    k_head = jnp.where(kl < GLA_HEADS * (GLA_DK // 2), kl // (GLA_DK // 2), GLA_HEADS)
    v_head = lax.broadcasted_iota(jnp.int32, (1, D_GLA_V), 1) // GLA_DV
    st_mask = (lax.broadcasted_iota(jnp.int32, (D_GLA_V, 1), 0) // GLA_DV) == k_head
    ti4 = lax.broadcasted_iota(jnp.int32, (C, GLA_HEADS * C), 0)
    si4 = lax.broadcasted_iota(jnp.int32, (C, GLA_HEADS * C), 1) % C
    tr = lax.broadcasted_iota(jnp.int32, (TR, TR), 0)
    sr = lax.broadcasted_iota(jnp.int32, (TR, TR), 1)
    same_chunk = (tr // C) == (sr // C)
    tri = [jnp.where(same_chunk & (sr <= tr), 1.0, 0.0).astype(BF16),
           jnp.where(same_chunk & (sr >= tr), 1.0, 0.0).astype(BF16)]

    def tile_start(i):
        return i * TR if isinstance(i, int) else pl.multiple_of(i * TR, TR)

    def tile_rows(i):
        return pl.ds(tile_start(i), TR)

    def decay_sums(i):
        z = z_ref[0, tile_rows(i), :].astype(BF16)
        g = _dot(z, wa_ref[...]) + ba_ref[...]
        la = (jnp.minimum(g, 0.0) - jnp.log(1.0 + jnp.exp(-jnp.abs(g)))) * (1.0 / GLA_TAU)
        sums = []
        for d in range(2):
            la_hi, la_lo = _split_bf16(la[:, d * GLA_KP:(d + 1) * GLA_KP])
            sums.append(_dot(tri[d], la_hi) + _dot(tri[d], la_lo))
        return sums

    def decayed_operands(i, sums):
        rows = tile_rows(i)
        q = q_ref[0, rows, :]
        k = k_ref[0, rows, :]
        for d in range(2):
            bcum = sums[d]
            ends = [bcum[(c + 1) * C - 1:(c + 1) * C] if d == 0 else bcum[c * C:c * C + 1]
                    for c in range(n_sub)]
            b_end = jnp.concatenate([jnp.broadcast_to(e, (C, GLA_KP)) for e in ends], axis=0)
            qd_scr[d, rows, :] = (q * jnp.exp(bcum)).astype(BF16)
            kd_scr[d, rows, :] = (k * jnp.exp(-bcum)).astype(BF16)
            ke_scr[d, rows, :] = (k * jnp.exp(b_end - bcum)).astype(BF16)
            for c in range(n_sub):
                dec_scr[d, pl.ds(i * n_sub + c, 1), :] = jnp.exp(ends[c])

    def intra_chunk(i):
        r0 = tile_start(i)
        v_bd = []
        for c in range(n_sub):
            v_c = v_ref[0, pl.ds(r0 + c * C, C), :]
            v_bd.append(jnp.concatenate(
                [jnp.where(v_head == h, v_c, jnp.zeros_like(v_c)) for h in range(GLA_HEADS)], axis=0))
            vt_scr[i * n_sub + c] = v_c.astype(F32).T.astype(BF16)
        scores = []
        for d in range(2):
            for c in range(n_sub):
                crow = pl.ds(r0 + c * C, C)
                k_c = kd_scr[d, crow, :]
                k_bd = jnp.concatenate(
                    [jnp.where(k_head == h, k_c, jnp.zeros_like(k_c)) for h in range(GLA_HEADS)], axis=0)
                scores.append(_dot_nt(qd_scr[d, crow, :], k_bd))
        for d in range(2):
            causal = (si4 <= ti4) if d == 0 else (si4 >= ti4)
            for c in range(n_sub):
                a = jnp.where(causal, scores[d * n_sub + c], 0.0).astype(BF16)
                o_scr[d, pl.ds(r0 + c * C, C), :] = _dot(a, v_bd[c])

    def tile_body(i, carry):
        sums = decay_sums(i + 1)
        intra_chunk(i)
        decayed_operands(i + 1, sums)
        return carry

    n_tiles = T_ALL // TR
    decayed_operands(0, decay_sums(0))
    lax.fori_loop(0, n_tiles - 1, tile_body, 0, unroll=2)
    intra_chunk(n_tiles - 1)

    st_scr[...] = jnp.zeros_like(st_scr)
    n_all = N_CHUNKS_C + N_CHUNKS_X

    def step(i, carry):
        in_ctx = i < N_CHUNKS_C
        idx_f = jnp.where(in_ctx, N_CHUNKS_X + i, i - N_CHUNKS_C)
        idx_b = jnp.where(in_ctx, N_CHUNKS_X + N_CHUNKS_C - 1 - i, n_all - 1 - i)
        half = D_GLA_V // 2
        dirs = ((0, idx_f), (1, idx_b))
        offs = [pl.ds(pl.multiple_of(idx * C, C), C) for _, idx in dirs]
        upd = []
        for (d, idx), off in zip(dirs, offs):
            ke = ke_scr[d, off, :]
            upd.append(jnp.concatenate(
                [_dot(vt_scr[idx, :half, :], ke), _dot(vt_scr[idx, half:, :], ke)], axis=0))
        for (d, idx), off in zip(dirs, offs):
            st = st_scr[d]
            o_scr[d, off, :] += _dot_nt(qd_scr[d, off, :], st.astype(BF16))
            st_scr[d] = st * dec_scr[d, pl.ds(idx, 1), :] + jnp.where(st_mask, upd[d], 0.0)
        return carry

    lax.fori_loop(0, n_all, step, 0, unroll=12)

    head_ones = jnp.where(
        (lax.broadcasted_iota(jnp.int32, (D_GLA_V, 1), 0) // GLA_DV) == v_head, 1.0, 0.0).astype(BF16)
    tile = GLA_TILE

    def fin(i, carry):
        r0 = pl.multiple_of(i * tile, tile)
        o = o_scr[0, pl.ds(r0, tile), :] + o_scr[1, pl.ds(r0, tile), :]
        sq_hi, sq_lo = _split_bf16(o * o)
        ms = (_dot(sq_hi, head_ones) + _dot(sq_lo, head_ones)) * (1.0 / GLA_DV)
        y = o * lax.rsqrt(ms + EPS) * nw_ref[...]
        gate = g_ref[0, pl.ds(r0, tile), :].astype(F32)
        o_ref[0, pl.ds(r0, tile), :] = (y * _silu(gate)).astype(BF16)
        return carry

    lax.fori_loop(0, T_ALL // tile, fin, 0, unroll=3)


def _gla(gq, gk, gv, gg, gz, wa, ba, gnw):
    nb = gq.shape[0]
    seq = lambda w: pl.BlockSpec((1, T_ALL, w), lambda b: (b, 0, 0))
    full = lambda a: pl.BlockSpec(a.shape, lambda b: (0,) * a.ndim)
    return pl.pallas_call(
        _gla_kernel,
        grid=(nb,),
        in_specs=[seq(GLA_KP), seq(GLA_KP), seq(D_GLA_V), seq(D_GLA_V), seq(Z_PAD),
                  full(wa), full(ba), full(gnw)],
        out_specs=seq(D_GLA_V),
        out_shape=jax.ShapeDtypeStruct((nb, T_ALL, D_GLA_V), BF16),
        scratch_shapes=[pltpu.VMEM((2, D_GLA_V, GLA_KP), F32),
                        pltpu.VMEM((2, T_ALL, D_GLA_V), F32),
                        pltpu.VMEM((2, T_ALL, GLA_KP), BF16),
                        pltpu.VMEM((2, T_ALL, GLA_KP), BF16),
                        pltpu.VMEM((2, T_ALL, GLA_KP), BF16),
                        pltpu.VMEM((T_ALL // GLA_CHUNK, D_GLA_V, GLA_CHUNK), BF16),
                        pltpu.VMEM((2, T_ALL // GLA_CHUNK, GLA_KP), F32)],

        compiler_params=pltpu.CompilerParams(
            dimension_semantics=("arbitrary",), vmem_limit_bytes=VMEM_LIMIT),
        name="gla_mix",
    )(gq, gk, gv, gg, gz, wa, ba, gnw)


def _outproj_kernel(*refs, n_h, tile, final):
    h_refs, refs = refs[:n_h], refs[n_h:]
    mf_ref, mn_ref, mg_ref, w_ref, mod_ref, nf_ref, o_ref, w_scr = refs[:8]
    b = pl.program_id(0)
    j = pl.program_id(1)

    @pl.when((b == 0) & (j == 0))
    def _():
        for r in range(0, D_MODEL, 256):
            w_scr[r:r + 256, :] = w_ref[0, r:r + 256, :].astype(BF16)

    acc = (_dot(mf_ref[0], w_scr[:D_FOUR, :])
           + _dot(mn_ref[0], w_scr[D_FOUR:D_FOUR + D_NA, :])
           + _dot(mg_ref[0], w_scr[D_FOUR + D_NA:, :]))
    gate = _mod_rows(mod_ref, b, j * tile, tile, 2 * D_MODEL)
    x = _stream_tile(h_refs, j, refs[8] if n_h > 1 else None) + gate * acc
    if final:
        ms = jnp.mean(x * x, axis=-1, keepdims=True)
        x = x * lax.rsqrt(ms + EPS) * nf_ref[...]
    o_ref[0] = x


def _outproj(h_in, mf, mn, mg, w_out, mod, l, nf, final):
    nb = h_in[0].shape[0]
    tile = 512 if final else TOK_TILE
    rows = SEQ if final else T_ALL
    if final:
        h_in = h_in[:1]
    split = len(h_in) > 1
    tok = lambda width: pl.BlockSpec((1, tile, width), lambda b, j: (b, j, 0))
    layer = lambda a: pl.BlockSpec((1,) + a.shape[1:], lambda b, j: (l,) + (0,) * (a.ndim - 1))
    return pl.pallas_call(
        functools.partial(_outproj_kernel, n_h=len(h_in), tile=tile, final=final),
        grid=(nb, rows // tile),
        in_specs=(_split_stream_specs() if split else _stream_specs(tile))
        + [tok(D_FOUR), tok(D_NA), tok(D_GLA_V),
           layer(w_out), layer(mod), pl.BlockSpec((1, D_MODEL), lambda b, j: (0, 0))],
        out_specs=tok(D_MODEL),
        out_shape=jax.ShapeDtypeStruct((nb, rows, D_MODEL), F32),
        scratch_shapes=[pltpu.VMEM((D_MODEL, D_MODEL), BF16)]
        + ([pltpu.VMEM((TOK_TILE, D_MODEL), F32)] if split else []),
        compiler_params=pltpu.CompilerParams(
            dimension_semantics=("arbitrary", "arbitrary"), vmem_limit_bytes=VMEM_LIMIT),
        name="outproj_final" if final else "outproj",
    )(*h_in, mf, mn, mg, w_out, mod, nf)


def _decay_weights(w_f, b_f, w_b, b_b):
    n_freq = GLA_DK // 4
    lane_pad = lambda a: jnp.zeros(a.shape[:-1] + (LANE - GLA_HEADS * 2 * n_freq,), a.dtype)
    lanes = lambda a: jnp.concatenate(
        [blk for offs in _qk_pieces() for blk in [a[..., o:o + n_freq] for o in offs] + [lane_pad(a)]],
        axis=-1)
    zero = jnp.zeros_like(lanes(w_f))
    top = jnp.concatenate([lanes(w_f), zero], axis=-1)
    bot = jnp.concatenate([zero, lanes(w_b)], axis=-1)
    pad = jnp.zeros((w_f.shape[0], Z_PAD - 2 * GLA_RANK, 2 * GLA_KP), w_f.dtype)
    wa = jnp.concatenate([top, bot, pad], axis=1).astype(BF16)
    ba = jnp.concatenate([lanes(b_f), lanes(b_b)], axis=-1)[:, None, :]
    return wa, ba


def kernel(x, c, ctx, c_ctx, w_ada, b_ada, norm_w, w_in, w_four, rpb, w_alpha_fwd, b_alpha_fwd,
           w_alpha_bwd, b_alpha_bwd, gla_norm_w, w_out, norm_f):
    nb = x.shape[0]
    depth = w_in.shape[0]
    assert x.shape == (nb, SEQ, D_MODEL) and ctx.shape == (nb, CTX_LEN, D_MODEL) and nb <= CTX_MOD_ROW

    w_in_t = jnp.swapaxes(w_in, 1, 2)
    w_four_b = w_four.astype(BF16)
    norm_w3 = norm_w.reshape(depth, 1, D_MODEL)
    wa, ba = _decay_weights(w_alpha_fwd, b_alpha_fwd, w_alpha_bwd, b_alpha_bwd)
    gnw = jnp.tile(gla_norm_w, (1, GLA_HEADS)).reshape(depth, 1, D_GLA_V)
    n_off = 2 * NA_KW - 1
    rpb_rev = rpb[..., ::-1]
    rpb_a = jnp.pad(rpb_rev, ((0, 0), (0, 0), (0, 1), (0, LANE - n_off)))
    rpb_b = jnp.pad(rpb_rev, ((0, 0), (0, 0), (0, 1), (GRID_W, LANE - GRID_W - n_off)))

    cos_t, sin_t = (jnp.asarray(t) for t in _rope_tables())
    c64, s64 = _dft_tables(FOUR_DIM)
    cl, sl = _dft_tables(SEQ)
    ccx, scx = _dft_tables(CTX_LEN)
    tabs = [jnp.asarray(t, dtype=F32).astype(BF16) for t in
            (_block_diag(c64, FOUR_GROUPS), _block_diag(s64, FOUR_GROUPS), cl, sl, ccx, scx)]

    cc = jnp.zeros((MOD_ROWS, D_MODEL), F32).at[:nb].set(c).at[CTX_MOD_ROW].set(c_ctx)
    mod = _modulation(cc, w_ada, b_ada)

    h_in = (x, x, ctx)
    for l in range(depth):
        final = l == depth - 1
        (p_four, nq, nk, nvt, ng, gq, gk, gv, gg, gz) = _inproj(
            h_in, mod, l, norm_w3, w_in_t, cos_t, sin_t)
        mf = _fourier(p_four, tabs, w_four_b[l])
        mn = _na(nq, nk, nvt, ng, rpb_a[l], rpb_b[l])
        mg = _gla(gq, gk, gv, gg, gz, wa[l], ba[l], gnw[l])
        h = _outproj(h_in, mf, mn, mg, w_out, mod, l, norm_f.reshape(1, D_MODEL), final)
        h_in = (h,)
    return h
```

```python
import functools

import numpy as np
import jax
import jax.numpy as jnp
from jax import lax
from jax.experimental import pallas as pl
from jax.experimental.pallas import tpu as pltpu

F32 = jnp.float32
BF16 = jnp.bfloat16

D_MODEL = 1024
SEQ = 2048
CTX_LEN = 256
T_ALL = SEQ + CTX_LEN
GRID_W = 64
GRID_H = SEQ // GRID_W
EPS = 1e-6

FOUR_GROUPS, FOUR_DIM = 4, 64
D_FOUR = FOUR_GROUPS * FOUR_DIM
NA_HEADS, NA_HEAD_DIM = 6, 64
D_NA = NA_HEADS * NA_HEAD_DIM
NA_KH, NA_KW = 8, 16
GLA_HEADS, GLA_DK, GLA_DV = 4, 48, 96
D_GLA_K = GLA_HEADS * GLA_DK
D_GLA_V = GLA_HEADS * GLA_DV
GLA_RANK = 16
GLA_TAU = 16.0
GLA_CHUNK = 64
ROPE_BASE = 10000.0
LOG2E = 1.4426950408889634

IN_SIZES = (D_FOUR, D_FOUR, D_NA, D_NA, D_NA, D_NA, D_GLA_K, D_GLA_K, D_GLA_V, D_GLA_V,
            GLA_RANK, GLA_RANK)
D_IN = sum(IN_SIZES)

LANE = 128
BF16_ROWS = 16
CTX_MOD_ROW = 8
MOD_ROWS = 16
GLA_KP = 2 * LANE
Z_PAD = LANE
N_CHUNKS_X = SEQ // GLA_CHUNK
N_CHUNKS_C = CTX_LEN // GLA_CHUNK
GLA_TILE = 256

NA_QROWS = 4
NA_KROWS = NA_QROWS + NA_KH
NA_QB = NA_QROWS * GRID_W
NA_KB = NA_KROWS * GRID_W
NA_NBLK = GRID_H // NA_QROWS
NA_VT_PAIR = LANE + BF16_ROWS
NA_VT_ROWS = NA_HEADS // 2 * NA_VT_PAIR

TOK_TILE = 768
N_XTILES = SEQ // TOK_TILE
X_TAIL = SEQ - N_XTILES * TOK_TILE
VMEM_LIMIT = 56 * 1024 * 1024

COL_FOUR = 0
COL_NA = COL_FOUR + 2 * D_FOUR
COL_GQ = COL_NA + 3 * D_NA
COL_GK = COL_GQ + GLA_KP
COL_GV = COL_GK + GLA_KP
COL_GG = COL_GV + D_GLA_V
COL_Z = COL_GG + D_GLA_V
N_COLS = COL_Z + Z_PAD


def _dot(a, b):
    return jnp.dot(a, b, preferred_element_type=F32)


def _dot_nt(a, b):
    return lax.dot_general(a, b, (((1,), (1,)), ((), ())), preferred_element_type=F32)


def _silu(x):
    return x / (1.0 + jnp.exp(-x))


def _reduce_rows(x, op):
    r, n = x.shape
    part = op(x.reshape(r // 8, 8, n), axis=0)
    return op(part, axis=0, keepdims=True)


def _split_bf16(x):
    hi = x.astype(BF16)
    lo = (x - hi.astype(F32)).astype(BF16)
    return hi, lo


def _qk_pieces():
    n_freq = GLA_DK // 4
    return [[h * GLA_DK + half * 2 * n_freq + part * n_freq
             for h in range(GLA_HEADS) for half in range(2)] for part in range(2)]


def _rope_tables():
    n_freq = GLA_DK // 4
    inv = ROPE_BASE ** (-np.arange(n_freq, dtype=np.float64) / n_freq)
    pos = np.arange(SEQ)
    prow, pcol = pos // GRID_W, pos % GRID_W
    cos = np.ones((T_ALL, LANE), np.float64)
    sin = np.zeros((T_ALL, LANE), np.float64)
    for h in range(GLA_HEADS):
        for half, p in enumerate((prow, pcol)):
            ang = p[:, None] * inv[None, :]
            lo = h * 2 * n_freq + half * n_freq
            cos[:SEQ, lo:lo + n_freq] = np.cos(ang)
            sin[:SEQ, lo:lo + n_freq] = np.sin(ang)
    return cos.astype(np.float32), sin.astype(np.float32)


def _dft_tables(n):
    idx = (np.arange(n)[:, None] * np.arange(n)[None, :]) % n
    ang = 2.0 * np.pi * idx / n
    return np.cos(ang) / np.sqrt(n), np.sin(ang) / np.sqrt(n)


def _block_diag(m, reps):
    n = m.shape[0]
    out = np.zeros((n * reps, n * reps), m.dtype)
    for r in range(reps):
        out[r * n:(r + 1) * n, r * n:(r + 1) * n] = m
    return out


def _na_row_offsets():
    out = []
    for r0 in (0, NA_QROWS, GRID_H - NA_QROWS):
        k0 = int(np.clip(r0 - NA_KH // 2, 0, GRID_H - NA_KROWS))
        per_type = []
        for a in range(NA_QROWS):
            r = r0 + a
            rs = int(np.clip(r - NA_KH // 2, 0, GRID_H - NA_KH))
            per_type.append([(k0 + j) - r + NA_KH - 1 if rs <= k0 + j < rs + NA_KH else None
                             for j in range(NA_KROWS)])
        out.append(per_type)
    return out


def _mod_kernel(c_ref, w_ref, b_ref, o_ref):
    c = c_ref[...]
    s_hi, s_lo = _split_bf16(_silu(c))
    w_hi, w_lo = _split_bf16(w_ref[0])
    acc = _dot(s_hi, w_hi) + _dot(s_lo, w_hi) + _dot(s_hi, w_lo)
    o_ref[0] = acc + b_ref[0]


def _modulation(cc, w_ada, b_ada):
    depth = w_ada.shape[0]
    n_mod = w_ada.shape[2]
    tile = D_MODEL
    return pl.pallas_call(
        _mod_kernel,
        grid=(depth, n_mod // tile),
        in_specs=[
            pl.BlockSpec((MOD_ROWS, D_MODEL), lambda l, j: (0, 0)),
            pl.BlockSpec((1, D_MODEL, tile), lambda l, j: (l, 0, j)),
            pl.BlockSpec((1, 1, tile), lambda l, j: (l, 0, j)),
        ],
        out_specs=pl.BlockSpec((1, MOD_ROWS, tile), lambda l, j: (l, 0, j)),
        out_shape=jax.ShapeDtypeStruct((depth, MOD_ROWS, n_mod), F32),
        compiler_params=pltpu.CompilerParams(
            dimension_semantics=("arbitrary", "arbitrary"), vmem_limit_bytes=VMEM_LIMIT),
        name="adaln_mod",
    )(cc, w_ada, b_ada.reshape(depth, 1, n_mod))


def _mod_rows(mod_ref, b, row0, n_rows, col0):
    rows = row0 + lax.broadcasted_iota(jnp.int32, (n_rows, 1), 0)
    m_x = mod_ref[0, pl.ds(b, 1), col0:col0 + D_MODEL]
    m_c = mod_ref[0, CTX_MOD_ROW:CTX_MOD_ROW + 1, col0:col0 + D_MODEL]
    return jnp.where(rows >= SEQ, m_c, m_x)


def _fill_weight(wt_ref, w_scr, dst0, src0, width, scale=None):
    k = w_scr.shape[0]
    for c in range(0, width, LANE):
        n = min(LANE, width - c)
        blk = wt_ref[0, src0 + c:src0 + c + n, :]
        if scale is not None:
            blk = blk * scale
        if n < LANE:
            blk = jnp.concatenate([blk, jnp.zeros((LANE - n, k), F32)], axis=0)
        w_scr[:, dst0 + c:dst0 + c + LANE] = blk.T.astype(BF16)


def _stream_specs(tile):
    return [pl.BlockSpec((1, tile, D_MODEL), lambda b, j: (b, j, 0))]


def _split_stream_specs():
    assert X_TAIL + CTX_LEN == TOK_TILE and SEQ % X_TAIL == 0
    return [pl.BlockSpec((1, TOK_TILE, D_MODEL), lambda b, j: (b, jnp.minimum(j, N_XTILES - 1), 0)),
            pl.BlockSpec((1, X_TAIL, D_MODEL), lambda b, j: (b, SEQ // X_TAIL - 1, 0)),
            pl.BlockSpec((1, CTX_LEN, D_MODEL), lambda b, j: (b, 0, 0))]


def _stream_tile(h_refs, j, tile_scr):
    if len(h_refs) == 1:
        return h_refs[0][0]
    x_ref, tail_ref, ctx_ref = h_refs

    @pl.when(j < N_XTILES)
    def _():
        tile_scr[...] = x_ref[0]

    @pl.when(j == N_XTILES)
    def _():
        tile_scr[:X_TAIL] = tail_ref[0]
        tile_scr[X_TAIL:] = ctx_ref[0]

    return tile_scr[...]


def _inproj_kernel(*refs, n_h):
    h_refs, refs = refs[:n_h], refs[n_h:]
    (mod_ref, nw_ref, w_ref, cos_ref, sin_ref,
     four_ref, nq_ref, nk_ref, nvt_ref, ng_ref,
     gq_ref, gk_ref, gv_ref, gg_ref, gz_ref, hx_scr, w_scr, wvt_scr) = refs[:18]
    b = pl.program_id(0)
    j = pl.program_id(1)

    @pl.when((b == 0) & (j == 0))
    def _():
        src = np.concatenate([[0], np.cumsum(IN_SIZES)])
        _fill_weight(w_ref, w_scr, COL_FOUR, src[0], 2 * D_FOUR)
        _fill_weight(w_ref, w_scr, COL_NA, src[2], D_NA, NA_HEAD_DIM ** -0.5 * LOG2E)
        _fill_weight(w_ref, w_scr, COL_NA + D_NA, src[3], D_NA)
        _fill_weight(w_ref, w_scr, COL_NA + 2 * D_NA, src[5], D_NA)
        n_freq = GLA_DK // 4
        pad = jnp.zeros((LANE - GLA_HEADS * 2 * n_freq, D_MODEL), F32)
        for dst0, src0 in ((COL_GQ, src[6]), (COL_GK, src[7])):
            for part, offs in enumerate(_qk_pieces()):
                tile = jnp.concatenate(
                    [w_ref[0, src0 + o:src0 + o + n_freq, :] for o in offs] + [pad], axis=0)
                w_scr[:, dst0 + part * LANE:dst0 + (part + 1) * LANE] = tile.T.astype(BF16)
        _fill_weight(w_ref, w_scr, COL_GV, src[8], D_GLA_V)
        _fill_weight(w_ref, w_scr, COL_GG, src[9], D_GLA_V)
        _fill_weight(w_ref, w_scr, COL_Z, src[10], 2 * GLA_RANK)
        for p in range(NA_HEADS // 2):
            blk = w_ref[0, src[4] + p * LANE:src[4] + (p + 1) * LANE, :]
            wvt_scr[p * NA_VT_PAIR:p * NA_VT_PAIR + LANE, :] = blk.astype(BF16)
            wvt_scr[p * NA_VT_PAIR + LANE:(p + 1) * NA_VT_PAIR, :] = jnp.zeros(
                (NA_VT_PAIR - LANE, D_MODEL), BF16)

    x = _stream_tile(h_refs, j, refs[18] if n_h > 1 else None)
    ms = jnp.mean(x * x, axis=-1, keepdims=True)
    y = x * lax.rsqrt(ms + EPS) * nw_ref[0]
    shift = _mod_rows(mod_ref, b, j * TOK_TILE, TOK_TILE, 0)
    scale = _mod_rows(mod_ref, b, j * TOK_TILE, TOK_TILE, D_MODEL)
    hx_scr[...] = (y * (1.0 + scale) + shift).astype(BF16)

    def proj(c0, width):
        return _dot(hx_scr[...], w_scr[:, c0:c0 + width])

    four_ref[0] = proj(COL_FOUR, 2 * D_FOUR).astype(BF16)
    nq_ref[0] = proj(COL_NA, D_NA).astype(BF16)
    nk_ref[0] = proj(COL_NA + D_NA, D_NA).astype(BF16)
    ones_rows = lax.broadcasted_iota(jnp.int32, (NA_VT_ROWS, 1), 0) % NA_VT_PAIR >= LANE
    nvt_ref[0] = (_dot_nt(wvt_scr[...], hx_scr[...]) + jnp.where(ones_rows, 1.0, 0.0)).astype(BF16)
    ng_ref[0] = proj(COL_NA + 2 * D_NA, D_NA).astype(BF16)
    gv_ref[0] = proj(COL_GV, D_GLA_V).astype(BF16)
    gg_ref[0] = proj(COL_GG, D_GLA_V).astype(BF16)
    gz_ref[0] = proj(COL_Z, Z_PAD)

    cos = cos_ref[...]
    sin = sin_ref[...]

    def rope(c0, out_ref, scale_out):
        u = proj(c0, GLA_KP)
        u1, u2 = u[:, :LANE], u[:, LANE:]
        out_ref[0, :, :LANE] = (u1 * cos - u2 * sin) * scale_out
        out_ref[0, :, LANE:] = (u1 * sin + u2 * cos) * scale_out

    rope(COL_GQ, gq_ref, GLA_DK ** -0.5)
    rope(COL_GK, gk_ref, 1.0)


def _inproj(h_in, mod, l, norm_w, w_in, cos_t, sin_t):
    nb = h_in[0].shape[0]
    nt = T_ALL // TOK_TILE
    split = len(h_in) > 1
    tok = lambda width: pl.BlockSpec((1, TOK_TILE, width), lambda b, j: (b, j, 0))
    layer = lambda a: pl.BlockSpec((1,) + a.shape[1:], lambda b, j: (l,) + (0,) * (a.ndim - 1))
    out_widths = [(2 * D_FOUR, BF16), (D_NA, BF16), (D_NA, BF16), (None, BF16), (D_NA, BF16),
                  (GLA_KP, F32), (GLA_KP, F32), (D_GLA_V, BF16), (D_GLA_V, BF16), (Z_PAD, F32)]
    vt_spec = pl.BlockSpec((1, NA_VT_ROWS, TOK_TILE), lambda b, j: (b, 0, j))
    vt_shape = jax.ShapeDtypeStruct((nb, NA_VT_ROWS, T_ALL), BF16)
    return pl.pallas_call(
        functools.partial(_inproj_kernel, n_h=len(h_in)),
        grid=(nb, nt),
        in_specs=(_split_stream_specs() if split else _stream_specs(TOK_TILE)) + [
            layer(mod),
            layer(norm_w),
            layer(w_in),
            pl.BlockSpec((TOK_TILE, LANE), lambda b, j: (j, 0)),
            pl.BlockSpec((TOK_TILE, LANE), lambda b, j: (j, 0)),
        ],
        out_specs=[vt_spec if w is None else tok(w) for w, _ in out_widths],
        out_shape=[vt_shape if w is None else jax.ShapeDtypeStruct((nb, T_ALL, w), dt)
                   for w, dt in out_widths],
        scratch_shapes=[pltpu.VMEM((TOK_TILE, D_MODEL), BF16), pltpu.VMEM((D_MODEL, N_COLS), BF16),
                        pltpu.VMEM((NA_VT_ROWS, D_MODEL), BF16)]
        + ([pltpu.VMEM((TOK_TILE, D_MODEL), F32)] if split else []),
        compiler_params=pltpu.CompilerParams(
            dimension_semantics=("arbitrary", "arbitrary"), vmem_limit_bytes=VMEM_LIMIT),
        name="inproj",
    )(*h_in, mod, norm_w, w_in, cos_t, sin_t)


def _fourier_kernel(p_ref, cbd_ref, sbd_ref, cl_ref, sl_ref, cc_ref, sc_ref, wf_ref, o_ref,
                    uc_scr, us_scr):
    u = p_ref[0, :, :D_FOUR]
    uc_scr[...] = _dot(u, cbd_ref[...]).astype(BF16)
    us_scr[...] = _dot(u, sbd_ref[...]).astype(BF16)

    def finish(f, r0, n):
        four = _dot(f.astype(BF16), wf_ref[...])
        gate = p_ref[0, pl.ds(r0, n), D_FOUR:].astype(F32)
        o_ref[0, pl.ds(r0, n), :] = (four * _silu(gate)).astype(BF16)

    f_x = _dot(cl_ref[...], uc_scr[:SEQ, :]) - _dot(sl_ref[...], us_scr[:SEQ, :])
    finish(f_x, 0, SEQ)
    f_c = _dot(cc_ref[...], uc_scr[SEQ:, :]) - _dot(sc_ref[...], us_scr[SEQ:, :])
    finish(f_c, SEQ, CTX_LEN)


def _fourier(p_four, tabs, wf):
    nb = p_four.shape[0]
    full = lambda a: pl.BlockSpec(a.shape, lambda b: (0,) * a.ndim)
    return pl.pallas_call(
        _fourier_kernel,
        grid=(nb,),
        in_specs=[pl.BlockSpec((1, T_ALL, 2 * D_FOUR), lambda b: (b, 0, 0))]
        + [full(t) for t in tabs] + [full(wf)],
        out_specs=pl.BlockSpec((1, T_ALL, D_FOUR), lambda b: (b, 0, 0)),
        out_shape=jax.ShapeDtypeStruct((nb, T_ALL, D_FOUR), BF16),
        scratch_shapes=[pltpu.VMEM((T_ALL, D_FOUR), BF16), pltpu.VMEM((T_ALL, D_FOUR), BF16)],
        compiler_params=pltpu.CompilerParams(
            dimension_semantics=("arbitrary",), vmem_limit_bytes=VMEM_LIMIT),
        name="fourier_mix",
    )(p_four, *tabs, wf)


def _na_build_bias(ra_ref, rb_ref, tbl_ref):
    w = GRID_W
    ck = lax.broadcasted_iota(jnp.int32, (w, 2 * w), 0)
    ln = lax.broadcasted_iota(jnp.int32, (w, 2 * w), 1)
    cq = ln % w
    cs = jnp.clip(cq - NA_KW // 2, 0, w - NA_KW)
    col_ok = (ck >= cs) & (ck < cs + NA_KW)
    left = ln < w
    neg = jnp.full((w, 2 * w), -jnp.inf, F32)
    offs = _na_row_offsets()
    cache = {}
    for typ in range(3):
        for ap in range(NA_QROWS // 2):
            for jr in range(NA_KROWS):
                d0 = offs[typ][2 * ap][jr]
                d1 = offs[typ][2 * ap + 1][jr]
                for h in range(NA_HEADS):
                    key = (h, d0, d1)
                    if key not in cache:
                        if d0 is None and d1 is None:
                            cache[key] = neg
                        else:
                            ok = col_ok
                            if d0 is None:
                                row = rb_ref[h, d1:d1 + 1, :]
                                ok = ok & jnp.logical_not(left)
                            elif d1 is None:
                                row = ra_ref[h, d0:d0 + 1, :]
                                ok = ok & left
                            else:
                                row = ra_ref[h, d0:d0 + 1, :] + rb_ref[h, d1:d1 + 1, :]
                            skew = pltpu.roll(jnp.broadcast_to(row * LOG2E, (w, 2 * w)),
                                              2 * w - (NA_KW - 1), 1, stride=1, stride_axis=0)
                            cache[key] = jnp.where(ok, skew, neg)
                    tbl_ref[h * 3 + typ, jr * w:(jr + 1) * w, ap * 2 * w:(ap + 1) * 2 * w] = cache[key]


def _na_kernel(q_ref, k_ref, vt_ref, g_ref, ra_ref, rb_ref, o_ref, bias_ref, s_scr):
    lane = lax.broadcasted_iota(jnp.int32, (1, LANE), 1)
    first_head_lanes = lane < NA_HEAD_DIM
    first_head_rows = lax.broadcasted_iota(jnp.int32, (LANE, 1), 0) < NA_HEAD_DIM
    heads = range(NA_HEADS)
    cols = [slice(h // 2 * LANE, (h // 2 + 1) * LANE) for h in heads]

    @pl.when(pl.program_id(0) == 0)
    def _():
        _na_build_bias(ra_ref, rb_ref, bias_ref)

    def block_geometry(j):
        if isinstance(j, int):
            k0 = min(max(j * NA_QROWS - NA_KH // 2, 0), GRID_H - NA_KROWS) * GRID_W
            return pl.ds(j * NA_QB, NA_QB), k0, (0 if j == 0 else 2 if j == NA_NBLK - 1 else 1)
        k0 = jnp.clip(j * NA_QROWS - NA_KH // 2, 0, GRID_H - NA_KROWS) * GRID_W
        typ = jnp.where(j == 0, 0, jnp.where(j == NA_NBLK - 1, 2, 1))
        return pl.ds(pl.multiple_of(j * NA_QB, NA_QB), NA_QB), pl.multiple_of(k0, NA_QB), typ

    def scores(j, buf, local):
        qrows, k0, typ = block_geometry(j)
        for h in heads:
            qp = q_ref[0, qrows, cols[h]]
            keep = first_head_lanes if h % 2 == 0 else jnp.logical_not(first_head_lanes)
            qm = jnp.where(keep, qp, jnp.zeros_like(qp))
            s_scr[buf, h, :CTX_LEN, :] = _dot_nt(k_ref[0, SEQ:, cols[h]], qm)
            if local:
                s_scr[buf, h, CTX_LEN:, :] = (
                    _dot_nt(k_ref[0, pl.ds(k0, NA_KB), cols[h]], qm) + bias_ref[h * 3 + typ])

    def finish(j, buf, local):
        qrows, k0, _ = block_geometry(j)
        n_keys = CTX_LEN + NA_KB if local else CTX_LEN
        e = []
        for h in heads:
            s = s_scr[buf, h, :n_keys, :]
            e.append(jnp.exp2((s - _reduce_rows(s, jnp.max)).astype(BF16)))
        for p in range(NA_HEADS // 2):
            vrows = slice(p * NA_VT_PAIR, (p + 1) * NA_VT_PAIR)
            acc = None
            for h in (2 * p, 2 * p + 1):
                o = _dot(vt_ref[0, vrows, SEQ:], e[h][:CTX_LEN])
                if local:
                    o = o + _dot(vt_ref[0, vrows, pl.ds(k0, NA_KB)], e[h][CTX_LEN:])
                o = o[:LANE] / o[LANE:LANE + 1]
                acc = o if h % 2 == 0 else jnp.where(first_head_rows, acc, o)
            gate = g_ref[0, qrows, cols[2 * p]].astype(F32)
            o_ref[0, qrows, cols[2 * p]] = (acc.T * _silu(gate)).astype(BF16)

    def pair_body(i, carry):
        scores(2 * i + 1, 1, True)
        finish(2 * i, 0, True)
        scores(2 * i + 2, 0, True)
        finish(2 * i + 1, 1, True)
        return carry

    scores(0, 0, True)
    lax.fori_loop(0, NA_NBLK // 2 - 1, pair_body, 0)
    scores(NA_NBLK - 1, 1, True)
    finish(NA_NBLK - 2, 0, True)
    scores(NA_NBLK, 0, False)
    finish(NA_NBLK - 1, 1, True)
    finish(NA_NBLK, 0, False)


def _na(nq, nk, nvt, ng, rpb_a, rpb_b):
    nb = nq.shape[0]
    seq = pl.BlockSpec((1, T_ALL, D_NA), lambda b: (b, 0, 0))
    seq_t = pl.BlockSpec((1, NA_VT_ROWS, T_ALL), lambda b: (b, 0, 0))
    tab = pl.BlockSpec(rpb_a.shape, lambda b: (0, 0, 0))
    return pl.pallas_call(
        _na_kernel,
        grid=(nb,),
        in_specs=[seq, seq, seq_t, seq, tab, tab],
        out_specs=seq,
        out_shape=jax.ShapeDtypeStruct((nb, T_ALL, D_NA), BF16),
        scratch_shapes=[pltpu.VMEM((NA_HEADS * 3, NA_KB, NA_QB), F32),
                        pltpu.VMEM((2, NA_HEADS, CTX_LEN + NA_KB, NA_QB), F32)],
        compiler_params=pltpu.CompilerParams(
            dimension_semantics=("arbitrary",), vmem_limit_bytes=VMEM_LIMIT),
        name="nbr_attention",
    )(nq, nk, nvt, ng, rpb_a, rpb_b)


def _gla_kernel(q_ref, k_ref, v_ref, g_ref, z_ref, wa_ref, ba_ref, nw_ref, o_ref,
                st_scr, o_scr, qd_scr, kd_scr, ke_scr, vt_scr, dec_scr):
    C = GLA_CHUNK
    TR = GLA_TILE
    n_sub = TR // C
    kl = lax.broadcasted_iota(jnp.int32, (1, GLA_KP), 1) % LANE
    k_head = jnp.where(kl < GLA_HEADS * (GLA_DK // 2), kl // (GLA_DK // 2), GLA_HEADS)
    v_head = lax.broadcasted_iota(jnp.int32, (1, D_GLA_V), 1) // GLA_DV
    st_mask = (lax.broadcasted_iota(jnp.int32, (D_GLA_V, 1), 0) // GLA_DV) == k_head
    ti4 = lax.broadcasted_iota(jnp.int32, (C, GLA_HEADS * C), 0)
    si4 = lax.broadcasted_iota(jnp.int32, (C, GLA_HEADS * C), 1) % C
    tr = lax.broadcasted_iota(jnp.int32, (TR, TR), 0)
    sr = lax.broadcasted_iota(jnp.int32, (TR, TR), 1)
    same_chunk = (tr // C) == (sr // C)
    tri = [jnp.where(same_chunk & (sr <= tr), 1.0, 0.0).astype(BF16),
           jnp.where(same_chunk & (sr >= tr), 1.0, 0.0).astype(BF16)]

    def tile_start(i):
        return i * TR if isinstance(i, int) else pl.multiple_of(i * TR, TR)

    def tile_rows(i):
        return pl.ds(tile_start(i), TR)

    def decay_sums(i):
        z = z_ref[0, tile_rows(i), :].astype(BF16)
        g = _dot(z, wa_ref[...]) + ba_ref[...]
        la = (jnp.minimum(g, 0.0) - jnp.log(1.0 + jnp.exp(-jnp.abs(g)))) * (1.0 / GLA_TAU)
        sums = []
        for d in range(2):
            la_hi, la_lo = _split_bf16(la[:, d * GLA_KP:(d + 1) * GLA_KP])
            sums.append(_dot(tri[d], la_hi) + _dot(tri[d], la_lo))
        return sums

    def decayed_operands(i, sums):
        rows = tile_rows(i)
        q = q_ref[0, rows, :]
        k = k_ref[0, rows, :]
        for d in range(2):
            bcum = sums[d]
            ends = [bcum[(c + 1) * C - 1:(c + 1) * C] if d == 0 else bcum[c * C:c * C + 1]
                    for c in range(n_sub)]
            b_end = jnp.concatenate([jnp.broadcast_to(e, (C, GLA_KP)) for e in ends], axis=0)
            qd_scr[d, rows, :] = (q * jnp.exp(bcum)).astype(BF16)
            kd_scr[d, rows, :] = (k * jnp.exp(-bcum)).astype(BF16)
            ke_scr[d, rows, :] = (k * jnp.exp(b_end - bcum)).astype(BF16)
            for c in range(n_sub):
                dec_scr[d, pl.ds(i * n_sub + c, 1), :] = jnp.exp(ends[c])

    def intra_chunk(i):
        r0 = tile_start(i)
        v_bd = []
        for c in range(n_sub):
            v_c = v_ref[0, pl.ds(r0 + c * C, C), :]
            v_bd.append(jnp.concatenate(
                [jnp.where(v_head == h, v_c, jnp.zeros_like(v_c)) for h in range(GLA_HEADS)], axis=0))
            vt_scr[i * n_sub + c] = v_c.astype(F32).T.astype(BF16)
        scores = []
        for d in range(2):
            for c in range(n_sub):
                crow = pl.ds(r0 + c * C, C)
                k_c = kd_scr[d, crow, :]
                k_bd = jnp.concatenate(
                    [jnp.where(k_head == h, k_c, jnp.zeros_like(k_c)) for h in range(GLA_HEADS)], axis=0)
                scores.append(_dot_nt(qd_scr[d, crow, :], k_bd))
        for d in range(2):
            causal = (si4 <= ti4) if d == 0 else (si4 >= ti4)
            for c in range(n_sub):
                a = jnp.where(causal, scores[d * n_sub + c], 0.0).astype(BF16)
                o_scr[d, pl.ds(r0 + c * C, C), :] = _dot(a, v_bd[c])

    def tile_body(i, carry):
        sums = decay_sums(i + 1)
        intra_chunk(i)
        decayed_operands(i + 1, sums)
        return carry

    n_tiles = T_ALL // TR
    decayed_operands(0, decay_sums(0))
    lax.fori_loop(0, n_tiles - 1, tile_body, 0, unroll=2)
    intra_chunk(n_tiles - 1)

    st_scr[...] = jnp.zeros_like(st_scr)
    n_all = N_CHUNKS_C + N_CHUNKS_X

    def step(i, carry):
        in_ctx = i < N_CHUNKS_C
        idx_f = jnp.where(in_ctx, N_CHUNKS_X + i, i - N_CHUNKS_C)
        idx_b = jnp.where(in_ctx, N_CHUNKS_X + N_CHUNKS_C - 1 - i, n_all - 1 - i)
        half = D_GLA_V // 2
        dirs = ((0, idx_f), (1, idx_b))
        offs = [pl.ds(pl.multiple_of(idx * C, C), C) for _, idx in dirs]
        upd = []
        for (d, idx), off in zip(dirs, offs):
            ke = ke_scr[d, off, :]
            upd.append(jnp.concatenate(
                [_dot(vt_scr[idx, :half, :], ke), _dot(vt_scr[idx, half:, :], ke)], axis=0))
        for (d, idx), off in zip(dirs, offs):
            st = st_scr[d]
            o_scr[d, off, :] += _dot_nt(qd_scr[d, off, :], st.astype(BF16))
            st_scr[d] = st * dec_scr[d, pl.ds(idx, 1), :] + jnp.where(st_mask, upd[d], 0.0)
        return carry

    lax.fori_loop(0, n_all, step, 0, unroll=12)

    head_ones = jnp.where(
        (lax.broadcasted_iota(jnp.int32, (D_GLA_V, 1), 0) // GLA_DV) == v_head, 1.0, 0.0).astype(BF16)
    tile = GLA_TILE

    def fin(i, carry):
        r0 = pl.multiple_of(i * tile, tile)
        o = o_scr[0, pl.ds(r0, tile), :] + o_scr[1, pl.ds(r0, tile), :]
        sq_hi, sq_lo = _split_bf16(o * o)
        ms = (_dot(sq_hi, head_ones) + _dot(sq_lo, head_ones)) * (1.0 / GLA_DV)
        y = o * lax.rsqrt(ms + EPS) * nw_ref[...]
        gate = g_ref[0, pl.ds(r0, tile), :].astype(F32)
        o_ref[0, pl.ds(r0, tile), :] = (y * _silu(gate)).astype(BF16)
        return carry

    lax.fori_loop(0, T_ALL // tile, fin, 0, unroll=True)


def _gla(gq, gk, gv, gg, gz, wa, ba, gnw):
    nb = gq.shape[0]
    seq = lambda w: pl.BlockSpec((1, T_ALL, w), lambda b: (b, 0, 0))
    full = lambda a: pl.BlockSpec(a.shape, lambda b: (0,) * a.ndim)
    return pl.pallas_call(
        _gla_kernel,
        grid=(nb,),
        in_specs=[seq(GLA_KP), seq(GLA_KP), seq(D_GLA_V), seq(D_GLA_V), seq(Z_PAD),
                  full(wa), full(ba), full(gnw)],
        out_specs=seq(D_GLA_V),
        out_shape=jax.ShapeDtypeStruct((nb, T_ALL, D_GLA_V), BF16),
        scratch_shapes=[pltpu.VMEM((2, D_GLA_V, GLA_KP), F32),
                        pltpu.VMEM((2, T_ALL, D_GLA_V), F32),
                        pltpu.VMEM((2, T_ALL, GLA_KP), BF16),
                        pltpu.VMEM((2, T_ALL, GLA_KP), BF16),
                        pltpu.VMEM((2, T_ALL, GLA_KP), BF16),
                        pltpu.VMEM((T_ALL // GLA_CHUNK, D_GLA_V, GLA_CHUNK), BF16),
                        pltpu.VMEM((2, T_ALL // GLA_CHUNK, GLA_KP), F32)],

        compiler_params=pltpu.CompilerParams(
            dimension_semantics=("arbitrary",), vmem_limit_bytes=VMEM_LIMIT),
        name="gla_mix",
    )(gq, gk, gv, gg, gz, wa, ba, gnw)


def _outproj_kernel(*refs, n_h, tile, final):
    h_refs, refs = refs[:n_h], refs[n_h:]
    mf_ref, mn_ref, mg_ref, w_ref, mod_ref, nf_ref, o_ref, w_scr = refs[:8]
    b = pl.program_id(0)
    j = pl.program_id(1)

    @pl.when((b == 0) & (j == 0))
    def _():
        for r in range(0, D_MODEL, 256):
            w_scr[r:r + 256, :] = w_ref[0, r:r + 256, :].astype(BF16)

    acc = (_dot(mf_ref[0], w_scr[:D_FOUR, :])
           + _dot(mn_ref[0], w_scr[D_FOUR:D_FOUR + D_NA, :])
           + _dot(mg_ref[0], w_scr[D_FOUR + D_NA:, :]))
    gate = _mod_rows(mod_ref, b, j * tile, tile, 2 * D_MODEL)
    x = _stream_tile(h_refs, j, refs[8] if n_h > 1 else None) + gate * acc
    if final:
        ms = jnp.mean(x * x, axis=-1, keepdims=True)
        x = x * lax.rsqrt(ms + EPS) * nf_ref[...]
    o_ref[0] = x


def _outproj(h_in, mf, mn, mg, w_out, mod, l, nf, final):
    nb = h_in[0].shape[0]
    tile = 512 if final else TOK_TILE
    rows = SEQ if final else T_ALL
    if final:
        h_in = h_in[:1]
    split = len(h_in) > 1
    tok = lambda width: pl.BlockSpec((1, tile, width), lambda b, j: (b, j, 0))
    layer = lambda a: pl.BlockSpec((1,) + a.shape[1:], lambda b, j: (l,) + (0,) * (a.ndim - 1))
    return pl.pallas_call(
        functools.partial(_outproj_kernel, n_h=len(h_in), tile=tile, final=final),
        grid=(nb, rows // tile),
        in_specs=(_split_stream_specs() if split else _stream_specs(tile))
        + [tok(D_FOUR), tok(D_NA), tok(D_GLA_V),
           layer(w_out), layer(mod), pl.BlockSpec((1, D_MODEL), lambda b, j: (0, 0))],
        out_specs=tok(D_MODEL),
        out_shape=jax.ShapeDtypeStruct((nb, rows, D_MODEL), F32),
        scratch_shapes=[pltpu.VMEM((D_MODEL, D_MODEL), BF16)]
        + ([pltpu.VMEM((TOK_TILE, D_MODEL), F32)] if split else []),
        compiler_params=pltpu.CompilerParams(
            dimension_semantics=("arbitrary", "arbitrary"), vmem_limit_bytes=VMEM_LIMIT),
        name="outproj_final" if final else "outproj",
    )(*h_in, mf, mn, mg, w_out, mod, nf)


def _decay_weights(w_f, b_f, w_b, b_b):
    n_freq = GLA_DK // 4
    lane_pad = lambda a: jnp.zeros(a.shape[:-1] + (LANE - GLA_HEADS * 2 * n_freq,), a.dtype)
    lanes = lambda a: jnp.concatenate(
        [blk for offs in _qk_pieces() for blk in [a[..., o:o + n_freq] for o in offs] + [lane_pad(a)]],
        axis=-1)
    zero = jnp.zeros_like(lanes(w_f))
    top = jnp.concatenate([lanes(w_f), zero], axis=-1)
    bot = jnp.concatenate([zero, lanes(w_b)], axis=-1)
    pad = jnp.zeros((w_f.shape[0], Z_PAD - 2 * GLA_RANK, 2 * GLA_KP), w_f.dtype)
    wa = jnp.concatenate([top, bot, pad], axis=1).astype(BF16)
    ba = jnp.concatenate([lanes(b_f), lanes(b_b)], axis=-1)[:, None, :]
    return wa, ba


def kernel(x, c, ctx, c_ctx, w_ada, b_ada, norm_w, w_in, w_four, rpb, w_alpha_fwd, b_alpha_fwd,
           w_alpha_bwd, b_alpha_bwd, gla_norm_w, w_out, norm_f):
    nb = x.shape[0]
    depth = w_in.shape[0]
    assert x.shape == (nb, SEQ, D_MODEL) and ctx.shape == (nb, CTX_LEN, D_MODEL) and nb <= CTX_MOD_ROW

    w_in_t = jnp.swapaxes(w_in, 1, 2)
    w_four_b = w_four.astype(BF16)
    norm_w3 = norm_w.reshape(depth, 1, D_MODEL)
    wa, ba = _decay_weights(w_alpha_fwd, b_alpha_fwd, w_alpha_bwd, b_alpha_bwd)
    gnw = jnp.tile(gla_norm_w, (1, GLA_HEADS)).reshape(depth, 1, D_GLA_V)
    n_off = 2 * NA_KW - 1
    rpb_rev = rpb[..., ::-1]
    rpb_a = jnp.pad(rpb_rev, ((0, 0), (0, 0), (0, 1), (0, LANE - n_off)))
    rpb_b = jnp.pad(rpb_rev, ((0, 0), (0, 0), (0, 1), (GRID_W, LANE - GRID_W - n_off)))

    cos_t, sin_t = (jnp.asarray(t) for t in _rope_tables())
    c64, s64 = _dft_tables(FOUR_DIM)
    cl, sl = _dft_tables(SEQ)
    ccx, scx = _dft_tables(CTX_LEN)
    tabs = [jnp.asarray(t, dtype=F32).astype(BF16) for t in
            (_block_diag(c64, FOUR_GROUPS), _block_diag(s64, FOUR_GROUPS), cl, sl, ccx, scx)]

    cc = jnp.zeros((MOD_ROWS, D_MODEL), F32).at[:nb].set(c).at[CTX_MOD_ROW].set(c_ctx)
    mod = _modulation(cc, w_ada, b_ada)

    h_in = (x, x, ctx)
    for l in range(depth):
        final = l == depth - 1
        (p_four, nq, nk, nvt, ng, gq, gk, gv, gg, gz) = _inproj(
            h_in, mod, l, norm_w3, w_in_t, cos_t, sin_t)
        mf = _fourier(p_four, tabs, w_four_b[l])
        mn = _na(nq, nk, nvt, ng, rpb_a[l], rpb_b[l])
        mg = _gla(gq, gk, gv, gg, gz, wa[l], ba[l], gnw[l])
        h = _outproj(h_in, mf, mn, mg, w_out, mod, l, norm_f.reshape(1, D_MODEL), final)
        h_in = (h,)
    return h
```
